```python
import math
import jax, jax.numpy as jnp
from jax import lax
import numpy as np

D_MODEL = 1024
BATCH = 8
SEQ = 4096
DEPTH = 4

GRID_W = 64
CTX_LEN = 256
N_MIXERS = 3
EPS = 1e-6
F32 = jnp.float32

D_RNN = 1024
RNN_BLOCKS = 8
RNN_BW = D_RNN // RNN_BLOCKS
RNN_CONV = 4
LRU_C = 8.0

N_HEADS = 16
N_KV_HEADS = 4
HEAD_DIM = 64
WINDOW = 128
ATTN_BLOCK = 128
ROPE_BASE = 10000.0

HYENA_ORDER = 2
HYENA_CONV = 3
FILTER_BANDS = 16
FILTER_EMB = 1 + 2 * FILTER_BANDS
FILTER_WIDTH = 64
DECAY_TARGET = 1e-2
FAST_DECAY_PCT = 0.3
SLOW_DECAY_PCT = 1.5

D_FF = 2816
N_EXPERTS = 8
TOP_K = 2
D_FF_EXPERT = 3584

kernel_name = "hybrid_flow_backbone_rglru_swa_hyena_moe"


def _rmsnorm(x, g):
    x32 = x.astype(F32)
    y = x32 * lax.rsqrt(jnp.mean(x32 * x32, axis=-1, keepdims=True) + EPS)
    return (y * g.astype(F32)).astype(x.dtype)


def _ada_norm(x, g, shift, scale):
    return _rmsnorm(x, g) * (1 + scale) + shift


def _dwconv(x, w, b):
    k = w.shape[0]
    y = lax.conv_general_dilated(
        x, w[:, None, :].astype(x.dtype), window_strides=(1,),
        padding=[(k // 2, k - 1 - k // 2)],
        dimension_numbers=("NWC", "WIO", "NWC"), feature_group_count=x.shape[-1])
    return y + b.astype(x.dtype)


def _axial_rope_tables(rows):
    q = HEAD_DIM // 4
    inv_freq = ROPE_BASE ** (-jnp.arange(q, dtype=F32) / q)
    row = jnp.repeat(jnp.arange(rows, dtype=F32), GRID_W)
    col = jnp.tile(jnp.arange(GRID_W, dtype=F32), rows)
    ang = jnp.concatenate([row[:, None] * inv_freq, col[:, None] * inv_freq], axis=-1)
    return jnp.cos(ang), jnp.sin(ang)


def _apply_rope(x, cos, sin):
    q = HEAD_DIM // 4
    x32 = x.astype(F32)

    def rot(xa, c, s):
        c = c[None, :, None, :]
        s = s[None, :, None, :]
        x1, x2 = xa[..., :q], xa[..., q:]
        return jnp.concatenate([x1 * c - x2 * s, x2 * c + x1 * s], axis=-1)

    out = jnp.concatenate([rot(x32[..., :2 * q], cos[:, :q], sin[:, :q]),
                           rot(x32[..., 2 * q:], cos[:, q:], sin[:, q:])], axis=-1)
    return out.astype(x.dtype)


def _lin_scan(a, b, h0, reverse):
    def combine(e1, e2):
        a1, b1 = e1
        a2, b2 = e2
        return a1 * a2, a2 * b1 + b2
    a_cum, b_cum = lax.associative_scan(combine, (a, b), reverse=reverse, axis=1)
    return a_cum * h0[:, None, :] + b_cum


def _rglru_coeffs(u, w_a, b_a, w_x, b_x, lam):
    ub = u.reshape(u.shape[:-1] + (RNN_BLOCKS, RNN_BW))
    r = jax.nn.sigmoid(jnp.einsum("blhi,hij->blhj", ub, w_a.astype(F32)).reshape(u.shape) + b_a.astype(F32))
    i = jax.nn.sigmoid(jnp.einsum("blhi,hij->blhj", ub, w_x.astype(F32)).reshape(u.shape) + b_x.astype(F32))
    log_a = -LRU_C * r * jax.nn.softplus(-lam.astype(F32))
    return jnp.exp(log_a), jnp.sqrt(-jnp.expm1(2.0 * log_a)) * (i * u)


def _rglru_mixer(h_ctx, h_lat, w_in, conv_w, conv_b, w_a, b_a, w_x, b_x, lam, w_out):
    def branches(h):
        gate, u = jnp.split(h @ w_in, 2, axis=-1)
        return gate, _dwconv(u, conv_w, conv_b).astype(F32)
    gate_c, u_c = branches(h_ctx)
    gate_l, u_l = branches(h_lat)
    h_zero = jnp.zeros((h_ctx.shape[0], D_RNN), F32)
    y_c = jnp.zeros_like(u_c)
    y_l = jnp.zeros_like(u_l)
    for d, reverse in ((0, False), (1, True)):
        a_c, b_c = _rglru_coeffs(u_c, w_a[d], b_a[d], w_x[d], b_x[d], lam[d])
        hs_c = _lin_scan(a_c, b_c, h_zero, reverse)
        h_ctx_final = hs_c[:, 0] if reverse else hs_c[:, -1]
        a_l, b_l = _rglru_coeffs(u_l, w_a[d], b_a[d], w_x[d], b_x[d], lam[d])
        y_l = y_l + _lin_scan(a_l, b_l, h_ctx_final, reverse)
        y_c = y_c + hs_c
    out_c = (y_c.astype(h_ctx.dtype) * jax.nn.gelu(gate_c)) @ w_out
    out_l = (y_l.astype(h_lat.dtype) * jax.nn.gelu(gate_l)) @ w_out
    return out_c, out_l


def _sink_softmax(s, sink):
    sk = jnp.broadcast_to(sink[None, :, :, None, None], s.shape[:-1] + (1,))
    p = jax.nn.softmax(jnp.concatenate([s, sk], axis=-1), axis=-1)
    return p[..., :-1]


def _swa_mixer(h_ctx, h_lat, rope_cos, rope_sin, w_qkv, sinks, w_o):
    B, L, _ = h_lat.shape
    G = N_HEADS // N_KV_HEADS
    scale = HEAD_DIM ** -0.5
    sink = sinks.astype(F32).reshape(N_KV_HEADS, G)

    def qkv(h):
        q, k, v = jnp.split(h @ w_qkv, [N_HEADS * HEAD_DIM, (N_HEADS + N_KV_HEADS) * HEAD_DIM], axis=-1)
        n = h.shape[1]
        return (q.reshape(B, n, N_HEADS, HEAD_DIM), k.reshape(B, n, N_KV_HEADS, HEAD_DIM),
                v.reshape(B, n, N_KV_HEADS, HEAD_DIM))

    q_c, k_c, v_c = qkv(h_ctx)
    q_c = q_c.reshape(B, -1, N_KV_HEADS, G, HEAD_DIM)
    q_l, k_l, v_l = qkv(h_lat)
    q_l = _apply_rope(q_l, rope_cos, rope_sin).reshape(B, L, N_KV_HEADS, G, HEAD_DIM)
    k_l = _apply_rope(k_l, rope_cos, rope_sin)

    s_c = jnp.einsum("bqkgd,bskd->bkgqs", q_c, k_c).astype(F32) * scale
    p_c = _sink_softmax(s_c, sink).astype(v_c.dtype)
    o_c = jnp.einsum("bkgqs,bskd->bqkgd", p_c, v_c).reshape(B, -1, N_HEADS * HEAD_DIM)

    pad = ((0, 0), (ATTN_BLOCK, ATTN_BLOCK), (0, 0), (0, 0))
    k_pad = jnp.pad(k_l, pad)
    v_pad = jnp.pad(v_l, pad)
    offs_q = jnp.arange(ATTN_BLOCK)
    offs_k = jnp.arange(3 * ATTN_BLOCK) - ATTN_BLOCK
    band = jnp.abs(offs_q[:, None] - offs_k[None, :]) <= WINDOW

    def block(i):
        start = i * ATTN_BLOCK
        qb = lax.dynamic_slice_in_dim(q_l, start, ATTN_BLOCK, axis=1)
        kb = lax.dynamic_slice_in_dim(k_pad, start, 3 * ATTN_BLOCK, axis=1)
        vb = lax.dynamic_slice_in_dim(v_pad, start, 3 * ATTN_BLOCK, axis=1)
        kpos = start + offs_k
        valid = band & ((kpos >= 0) & (kpos < L))[None, :]
        s_loc = jnp.einsum("bqkgd,bskd->bkgqs", qb, kb).astype(F32) * scale
        s_loc = jnp.where(valid, s_loc, -1e30)
        s_ctx = jnp.einsum("bqkgd,bckd->bkgqc", qb, k_c).astype(F32) * scale
        p = _sink_softmax(jnp.concatenate([s_loc, s_ctx], axis=-1), sink).astype(vb.dtype)
        return (jnp.einsum("bkgqs,bskd->bqkgd", p[..., :3 * ATTN_BLOCK], vb)
                + jnp.einsum("bkgqc,bckd->bqkgd", p[..., 3 * ATTN_BLOCK:], v_c))

    o_l = lax.map(block, jnp.arange(L // ATTN_BLOCK))
    o_l = jnp.moveaxis(o_l, 0, 1).reshape(B, L, N_HEADS * HEAD_DIM)
    return o_c @ w_o, o_l @ w_o


def _hyena_filter_spectra(L, w1, b1, w2, b2, w3, b3, freq, w4):
    t = jnp.linspace(0.0, 1.0, L, dtype=F32)[:, None]
    omega = (2.0 * math.pi / L) * jnp.arange(L, dtype=F32)[:, None]
    bands = jnp.linspace(1e-4, FILTER_BANDS - 1, FILTER_BANDS, dtype=F32)[None, :]
    z = jnp.concatenate([t, jnp.cos(bands * omega), -jnp.sin(bands * omega)], axis=-1)
    fr = freq.astype(F32)
    hid = jnp.sin(fr * (z @ w1.astype(F32) + b1.astype(F32)))
    hid = jnp.sin(fr * (hid @ w2.astype(F32) + b2.astype(F32)))
    hid = jnp.sin(fr * (hid @ w3.astype(F32) + b3.astype(F32)))
    h = (hid @ w4.astype(F32)).reshape(L, HYENA_ORDER, 2, D_MODEL)
    deltas = jnp.linspace(math.log(DECAY_TARGET) / SLOW_DECAY_PCT,
                          math.log(DECAY_TARGET) / FAST_DECAY_PCT, D_MODEL, dtype=F32)
    h = h * jnp.exp(-t * jnp.abs(deltas))[:, None, None, :]
    k = jnp.concatenate([h[:, :, 0], jnp.zeros((1, HYENA_ORDER, D_MODEL), F32), h[:0:-1, :, 1]], axis=0)
    k = k / jnp.sum(jnp.abs(k), axis=0, keepdims=True)
    return jnp.fft.rfft(k, axis=0)


def _fft_conv(z, k_f, bias):
    L = z.shape[1]
    zf = jnp.fft.rfft(z, n=2 * L, axis=1)
    y = jnp.fft.irfft(zf * k_f[None], n=2 * L, axis=1)[:, :L]
    return y + bias * z


def _hyena_mixer(h_ctx, h_lat, w_in, b_in, conv_w, conv_b, f_w1, f_b1, f_w2, f_b2, f_w3, f_b3,
                 f_freq, f_w4, skip, w_out, b_out):
    def run(h):
        L = h.shape[1]
        u = _dwconv(h @ w_in + b_in, conv_w, conv_b).astype(F32)
        g1, g2, v = jnp.split(u, 3, axis=-1)
        k_f = _hyena_filter_spectra(L, f_w1, f_b1, f_w2, f_b2, f_w3, f_b3, f_freq, f_w4)
        z = g1 * _fft_conv(v, k_f[:, 0], skip[0].astype(F32))
        z = g2 * _fft_conv(z, k_f[:, 1], skip[1].astype(F32))
        return z.astype(h.dtype) @ w_out + b_out
    return run(h_ctx), run(h_lat)


def _swiglu(h, w_gate, w_up, w_down):
    return (jax.nn.silu(h @ w_gate) * (h @ w_up)) @ w_down


def _moe(h, router, w_gate, w_up, w_down):
    logits = (h @ router).astype(F32)
    top_v, top_i = lax.top_k(logits, TOP_K)
    wts = jax.nn.softmax(top_v, axis=-1)
    gate = jnp.sum(jax.nn.one_hot(top_i, N_EXPERTS, dtype=F32) * wts[..., None], axis=-2)
    out = jnp.zeros_like(h)
    for e in range(N_EXPERTS):
        out = out + gate[..., e:e + 1].astype(h.dtype) * _swiglu(h, w_gate[e], w_up[e], w_down[e])
    return out


def setup_inputs(seed: int = 0) -> dict:
    key = jax.random.key(seed)
    keys = iter(jax.random.split(key, 48))
    n_a = len(range(0, DEPTH, N_MIXERS))
    n_b = len(range(1, DEPTH, N_MIXERS))
    n_c = len(range(2, DEPTH, N_MIXERS))
    n_dense = len(range(0, DEPTH, 2))
    n_moe = len(range(1, DEPTH, 2))
    qkv_width = (N_HEADS + 2 * N_KV_HEADS) * HEAD_DIM

    def dense(shape, fan_in, gain=1.0):
        return jax.random.normal(next(keys), shape, F32) * (gain * fan_in ** -0.5)

    def noise(shape, std):
        return jax.random.normal(next(keys), shape, F32) * std

    def norm_gain(shape):
        return 1.0 + noise(shape, 0.05)

    a_init = jax.random.uniform(next(keys), (n_a, 2, D_RNN), F32, 0.9, 0.999) ** (1.0 / LRU_C)
    return {
        "x": noise((BATCH, SEQ, D_MODEL), 1.0),
        "c": noise((BATCH, D_MODEL), 1.0),
        "ctx": noise((BATCH, CTX_LEN, D_MODEL), 1.0),
        "c_ctx": noise((D_MODEL,), 1.0),
        "ada_w": dense((DEPTH, D_MODEL, 6 * D_MODEL), D_MODEL, 0.5),
        "ada_b": noise((DEPTH, 6 * D_MODEL), 0.02),
        "norm_mix": norm_gain((DEPTH, D_MODEL)),
        "norm_ffn": norm_gain((DEPTH, D_MODEL)),
        "norm_final": norm_gain((D_MODEL,)),
        "lru_w_in": dense((n_a, D_MODEL, 2 * D_RNN), D_MODEL),
        "lru_conv_w": dense((n_a, RNN_CONV, D_RNN), RNN_CONV),
        "lru_conv_b": noise((n_a, D_RNN), 0.02),
        "lru_w_a": dense((n_a, 2, RNN_BLOCKS, RNN_BW, RNN_BW), RNN_BW),
        "lru_b_a": noise((n_a, 2, D_RNN), 0.02),
        "lru_w_x": dense((n_a, 2, RNN_BLOCKS, RNN_BW, RNN_BW), RNN_BW),
        "lru_b_x": noise((n_a, 2, D_RNN), 0.02),
        "lru_lambda": jnp.log(a_init) - jnp.log1p(-a_init),
        "lru_w_out": dense((n_a, D_RNN, D_MODEL), D_RNN),
        "attn_w_qkv": dense((n_b, D_MODEL, qkv_width), D_MODEL),
        "attn_sinks": noise((n_b, N_HEADS), 0.5),
        "attn_w_o": dense((n_b, N_HEADS * HEAD_DIM, D_MODEL), N_HEADS * HEAD_DIM),
        "hy_w_in": dense((n_c, D_MODEL, 3 * D_MODEL), D_MODEL),
        "hy_b_in": noise((n_c, 3 * D_MODEL), 0.02),
        "hy_conv_w": dense((n_c, HYENA_CONV, 3 * D_MODEL), HYENA_CONV),
        "hy_conv_b": noise((n_c, 3 * D_MODEL), 0.02),
        "hy_f_w1": dense((n_c, FILTER_EMB, FILTER_WIDTH), FILTER_EMB),
        "hy_f_b1": noise((n_c, FILTER_WIDTH), 0.1),
        "hy_f_w2": dense((n_c, FILTER_WIDTH, FILTER_WIDTH), FILTER_WIDTH),
        "hy_f_b2": noise((n_c, FILTER_WIDTH), 0.1),
        "hy_f_w3": dense((n_c, FILTER_WIDTH, FILTER_WIDTH), FILTER_WIDTH),
        "hy_f_b3": noise((n_c, FILTER_WIDTH), 0.1),
        "hy_f_freq": 1.0 + noise((n_c, FILTER_WIDTH), 0.05),
        "hy_f_w4": dense((n_c, FILTER_WIDTH, HYENA_ORDER * 2 * D_MODEL), FILTER_WIDTH),
        "hy_skip": noise((n_c, HYENA_ORDER, D_MODEL), 1.0),
        "hy_w_out": dense((n_c, D_MODEL, D_MODEL), D_MODEL),
        "hy_b_out": noise((n_c, D_MODEL), 0.02),
        "ffn_w_gate": dense((n_dense, D_MODEL, D_FF), D_MODEL),
        "ffn_w_up": dense((n_dense, D_MODEL, D_FF), D_MODEL),
        "ffn_w_down": dense((n_dense, D_FF, D_MODEL), D_FF),
        "moe_router": dense((n_moe, D_MODEL, N_EXPERTS), D_MODEL),
        "moe_w_gate": dense((n_moe, N_EXPERTS, D_MODEL, D_FF_EXPERT), D_MODEL),
        "moe_w_up": dense((n_moe, N_EXPERTS, D_MODEL, D_FF_EXPERT), D_MODEL),
        "moe_w_down": dense((n_moe, N_EXPERTS, D_FF_EXPERT, D_MODEL), D_FF_EXPERT),
    }


def reference(x, c, ctx, c_ctx, ada_w, ada_b, norm_mix, norm_ffn, norm_final,
              lru_w_in, lru_conv_w, lru_conv_b, lru_w_a, lru_b_a, lru_w_x, lru_b_x, lru_lambda, lru_w_out,
              attn_w_qkv, attn_sinks, attn_w_o,
              hy_w_in, hy_b_in, hy_conv_w, hy_conv_b, hy_f_w1, hy_f_b1, hy_f_w2, hy_f_b2, hy_f_w3, hy_f_b3,
              hy_f_freq, hy_f_w4, hy_skip, hy_w_out, hy_b_out,
              ffn_w_gate, ffn_w_up, ffn_w_down,
              moe_router, moe_w_gate, moe_w_up, moe_w_down):
    rows = x.shape[1] // GRID_W
    rope_cos, rope_sin = _axial_rope_tables(rows)
    s_lat = jax.nn.silu(c)
    s_ctx = jax.nn.silu(c_ctx)
    for i in range(DEPTH):
        j = i // N_MIXERS
        last = i == DEPTH - 1
        mod_l = jnp.split((s_lat @ ada_w[i] + ada_b[i])[:, None, :], 6, axis=-1)
        mod_c = jnp.split(s_ctx @ ada_w[i] + ada_b[i], 6, axis=-1)
        h_c = _ada_norm(ctx, norm_mix[i], mod_c[0], mod_c[1])
        h_l = _ada_norm(x, norm_mix[i], mod_l[0], mod_l[1])
        if i % N_MIXERS == 0:
            y_c, y_l = _rglru_mixer(h_c, h_l, lru_w_in[j], lru_conv_w[j], lru_conv_b[j], lru_w_a[j],
                                    lru_b_a[j], lru_w_x[j], lru_b_x[j], lru_lambda[j], lru_w_out[j])
        elif i % N_MIXERS == 1:
            y_c, y_l = _swa_mixer(h_c, h_l, rope_cos, rope_sin, attn_w_qkv[j], attn_sinks[j], attn_w_o[j])
        else:
            y_c, y_l = _hyena_mixer(h_c, h_l, hy_w_in[j], hy_b_in[j], hy_conv_w[j], hy_conv_b[j],
                                    hy_f_w1[j], hy_f_b1[j], hy_f_w2[j], hy_f_b2[j], hy_f_w3[j], hy_f_b3[j],
                                    hy_f_freq[j], hy_f_w4[j], hy_skip[j], hy_w_out[j], hy_b_out[j])
        x = x + mod_l[2] * y_l

        def channel_mixer(h):
            if i % 2 == 0:
                return _swiglu(h, ffn_w_gate[i // 2], ffn_w_up[i // 2], ffn_w_down[i // 2])
            return _moe(h, moe_router[i // 2], moe_w_gate[i // 2], moe_w_up[i // 2], moe_w_down[i // 2])

        x = x + mod_l[5] * channel_mixer(_ada_norm(x, norm_ffn[i], mod_l[3], mod_l[4]))
        if not last:
            ctx = ctx + mod_c[2] * y_c
            ctx = ctx + mod_c[5] * channel_mixer(_ada_norm(ctx, norm_ffn[i], mod_c[3], mod_c[4]))
    return _rmsnorm(x, norm_final)
```

```python
import functools
import math

import jax
import jax.numpy as jnp
import numpy as np
from jax import lax
from jax.experimental import pallas as pl
from jax.experimental.pallas import tpu as pltpu

F32 = jnp.float32
BF16 = jnp.bfloat16
HIGHEST = lax.Precision.HIGHEST

N_MIXERS = 3
EPS = 1e-6
GRID_W = 64
RNN_BLOCKS = 8
LRU_C = 8.0
N_HEADS = 16
N_KV_HEADS = 4
HEAD_DIM = 64
WINDOW = 128
ROPE_BASE = 10000.0
FILTER_BANDS = 16
DECAY_TARGET = 1e-2
FAST_DECAY_PCT = 0.3
SLOW_DECAY_PCT = 1.5
N_EXPERTS = 8
TOP_K = 2

LANES = 128
SUBLANES = 8
VMEM_LIMIT = 56 * 1024 * 1024
FFT_N1 = 128


def _cparams(sem, vmem=VMEM_LIMIT):
    return pltpu.CompilerParams(dimension_semantics=sem, vmem_limit_bytes=vmem)


def _pick_tile(n, cands):
    for c in cands:
        if n % c == 0:
            return c
    raise ValueError(f"no tile for {n} in {cands}")


def _row_is_ctx(j, tm, ctx_len, shape):
    rows = j * tm + lax.broadcasted_iota(jnp.int32, shape, 0)
    return rows < ctx_len


def _sel_mod(mod_ref, is_ctx, idx):
    return jnp.where(is_ctx, mod_ref[0, idx:idx + 1, :], mod_ref[1, idx:idx + 1, :])


def _adanorm(x, g, shift, scale):
    ms = jnp.mean(x * x, axis=-1, keepdims=True)
    y = x * lax.rsqrt(ms + EPS)
    return (y * g) * (1.0 + scale) + shift


def _mods(c, c_ctx, ada_w, ada_b):
    depth, d, d6 = ada_w.shape
    b = c.shape[0]
    rows = ((b + 1 + SUBLANES - 1) // SUBLANES) * SUBLANES
    cond = jnp.concatenate([c, c_ctx[None, :], jnp.zeros((rows - b - 1, d), F32)], axis=0)
    tn = _pick_tile(d6, (1024, 512, 256, 128))

    def kern(c_ref, w_ref, b_ref, o_ref):
        x = c_ref[...]
        s = x * jax.nn.sigmoid(x)
        o_ref[...] = jnp.dot(s, w_ref[...], preferred_element_type=F32, precision=HIGHEST) + b_ref[...]

    out = pl.pallas_call(
        kern,
        grid=(depth, d6 // tn),
        in_specs=[
            pl.BlockSpec((rows, d), lambda i, n: (0, 0)),
            pl.BlockSpec((None, d, tn), lambda i, n: (i, 0, n)),
            pl.BlockSpec((None, 1, tn), lambda i, n: (i, 0, n)),
        ],
        out_specs=pl.BlockSpec((None, rows, tn), lambda i, n: (i, 0, n)),
        out_shape=jax.ShapeDtypeStruct((depth, rows, d6), F32),
        compiler_params=_cparams(("parallel", "parallel")),
        name="mods",
    )(cond, ada_w, ada_b.reshape(depth, 1, d6))
    m = out.reshape(depth, rows, 6, d)
    lat = m[:, :b]
    ctx = jnp.broadcast_to(m[:, b:b + 1], lat.shape)
    both = jnp.stack([ctx, lat], axis=2)
    return jnp.pad(both, ((0, 0), (0, 0), (0, 0), (0, 2), (0, 0)))


def _norm_mm_kernel(x_ref, mod_ref, g_ref, w_ref, b_ref, o_ref, *, tm, ctx_len):
    j = pl.program_id(1)
    x = x_ref[...]
    is_ctx = _row_is_ctx(j, tm, ctx_len, x.shape)
    h = _adanorm(x, g_ref[...], _sel_mod(mod_ref, is_ctx, 0), _sel_mod(mod_ref, is_ctx, 1))
    acc = jnp.dot(h.astype(BF16), w_ref[...], preferred_element_type=F32) + b_ref[...]
    o_ref[...] = acc.astype(o_ref.dtype)


def _norm_mm(x_all, mods_i, g, w, bias, ctx_len, out_dtype):
    b, s, d = x_all.shape
    n = w.shape[1]
    tm = _pick_tile(s, (544, 384, 256, 128))
    return pl.pallas_call(
        functools.partial(_norm_mm_kernel, tm=tm, ctx_len=ctx_len),
        grid=(b, s // tm),
        in_specs=[
            pl.BlockSpec((None, tm, d), lambda bi, j: (bi, j, 0)),
            pl.BlockSpec((None, 2, 8, d), lambda bi, j: (bi, 0, 0, 0)),
            pl.BlockSpec((1, d), lambda bi, j: (0, 0)),
            pl.BlockSpec((d, n), lambda bi, j: (0, 0)),
            pl.BlockSpec((1, n), lambda bi, j: (0, 0)),
        ],
        out_specs=pl.BlockSpec((None, tm, n), lambda bi, j: (bi, j, 0)),
        out_shape=jax.ShapeDtypeStruct((b, s, n), out_dtype),
        compiler_params=_cparams(("parallel", "parallel")),
        name="norm_mm",
    )(x_all, mods_i, g.reshape(1, d), w.astype(BF16), bias.reshape(1, n).astype(F32))


def _mm_res_body(y_bf16, x_ref, mod_ref, g_ref, w_ref, b_ref, xo_ref, ho_ref, tm, ctx_len):
    j = pl.program_id(1)
    x = x_ref[...]
    is_ctx = _row_is_ctx(j, tm, ctx_len, x.shape)
    acc = jnp.dot(y_bf16, w_ref[...], preferred_element_type=F32) + b_ref[...]
    x_new = x + _sel_mod(mod_ref, is_ctx, 2) * acc
    xo_ref[...] = x_new
    h2 = _adanorm(x_new, g_ref[...], _sel_mod(mod_ref, is_ctx, 3), _sel_mod(mod_ref, is_ctx, 4))
    ho_ref[...] = h2.astype(ho_ref.dtype)


def _mm_res_kernel(y_ref, x_ref, mod_ref, g_ref, w_ref, b_ref, xo_ref, ho_ref, *, tm, ctx_len):
    _mm_res_body(y_ref[...].astype(BF16), x_ref, mod_ref, g_ref, w_ref, b_ref, xo_ref, ho_ref, tm, ctx_len)


def _lru_mm_res_kernel(hf_ref, hb_ref, gate_ref, x_ref, mod_ref, g_ref, w_ref, b_ref, xo_ref, ho_ref, *, tm, ctx_len):
    y = (hf_ref[...] + hb_ref[...]) * jax.nn.gelu(gate_ref[...])
    _mm_res_body(y.astype(BF16), x_ref, mod_ref, g_ref, w_ref, b_ref, xo_ref, ho_ref, tm, ctx_len)


def _mm_res_call(kern, ys, y_specs, x_all, mods_i, g_ffn, w, bias, ctx_len, tm, h_dtype):
    b, s, d = x_all.shape
    k = w.shape[0]
    return pl.pallas_call(
        functools.partial(kern, tm=tm, ctx_len=ctx_len),
        grid=(b, s // tm),
        in_specs=y_specs + [
            pl.BlockSpec((None, tm, d), lambda bi, j: (bi, j, 0)),
            pl.BlockSpec((None, 2, 8, d), lambda bi, j: (bi, 0, 0, 0)),
            pl.BlockSpec((1, d), lambda bi, j: (0, 0)),
            pl.BlockSpec((k, d), lambda bi, j: (0, 0)),
            pl.BlockSpec((1, d), lambda bi, j: (0, 0)),
        ],
        out_specs=[
            pl.BlockSpec((None, tm, d), lambda bi, j: (bi, j, 0)),
            pl.BlockSpec((None, tm, d), lambda bi, j: (bi, j, 0)),
        ],
        out_shape=[jax.ShapeDtypeStruct((b, s, d), F32), jax.ShapeDtypeStruct((b, s, d), h_dtype)],
        compiler_params=_cparams(("parallel", "parallel")),
        name="mm_res",
    )(*ys, x_all, mods_i, g_ffn.reshape(1, d), w.astype(BF16), bias.reshape(1, d).astype(F32))


def _mm_res(y, x_all, mods_i, g_ffn, w, bias, ctx_len, h_dtype=BF16):
    b, s, d = x_all.shape
    k = w.shape[0]
    tm = _pick_tile(s, (544, 384, 256, 128))
    spec = [pl.BlockSpec((None, tm, k), lambda bi, j: (bi, j, 0))]
    return _mm_res_call(_mm_res_kernel, [y], spec, x_all, mods_i, g_ffn, w, bias, ctx_len, tm, h_dtype)


def _lru_mm_res(hs_f, hs_b, gu, x_all, mods_i, g_ffn, w, ctx_len, h_dtype=BF16):
    b, s, d = x_all.shape
    k = w.shape[0]
    tm = _pick_tile(s, (544, 384, 256, 128))
    specs = [
        pl.BlockSpec((None, tm, k), lambda bi, j: (bi, j, 0)),
        pl.BlockSpec((None, tm, k), lambda bi, j: (bi, j, 0)),
        pl.BlockSpec((None, tm, k), lambda bi, j: (bi, j, 0)),
    ]
    return _mm_res_call(_lru_mm_res_kernel, [hs_f, hs_b, gu], specs, x_all, mods_i, g_ffn, w,
                        jnp.zeros((d,), F32), ctx_len, tm, h_dtype)


def _ffn_kernel(h_ref, x_ref, mod_ref, wg_ref, wu_ref, wd_ref, o_ref, *, tm, ctx_len):
    j = pl.program_id(1)
    h = h_ref[...]
    a = jnp.dot(h, wg_ref[...], preferred_element_type=F32)
    u = jnp.dot(h, wu_ref[...], preferred_element_type=F32)
    act = (a * jax.nn.sigmoid(a) * u).astype(BF16)
    y = jnp.dot(act, wd_ref[...], preferred_element_type=F32)
    x = x_ref[...]
    is_ctx = _row_is_ctx(j, tm, ctx_len, x.shape)
    o_ref[...] = x + _sel_mod(mod_ref, is_ctx, 5) * y


def _ffn_dense(h2, x_all, mods_i, wg, wu, wd, ctx_len):
    b, s, d = x_all.shape
    f = wg.shape[1]
    tm = _pick_tile(s, (544, 384, 256, 128))
    resident = dict(pipeline_mode=pl.Buffered(1))
    return pl.pallas_call(
        functools.partial(_ffn_kernel, tm=tm, ctx_len=ctx_len),
        grid=(b, s // tm),
        in_specs=[
            pl.BlockSpec((None, tm, d), lambda bi, j: (bi, j, 0)),
            pl.BlockSpec((None, tm, d), lambda bi, j: (bi, j, 0)),
            pl.BlockSpec((None, 2, 8, d), lambda bi, j: (bi, 0, 0, 0)),
            pl.BlockSpec((d, f), lambda bi, j: (0, 0), **resident),
            pl.BlockSpec((d, f), lambda bi, j: (0, 0), **resident),
            pl.BlockSpec((f, d), lambda bi, j: (0, 0), **resident),
        ],
        out_specs=pl.BlockSpec((None, tm, d), lambda bi, j: (bi, j, 0)),
        out_shape=jax.ShapeDtypeStruct((b, s, d), F32),
        compiler_params=_cparams(("parallel", "parallel")),
        name="ffn_dense",
    )(h2, x_all, mods_i, wg.astype(BF16), wu.astype(BF16), wd.astype(BF16))


def _lru_bwd_tile(j, nt, nct):
    return jnp.where(j < nct, nct - 1 - j, nt - 1 - (j - nct))


def _lru_scan_kernel(um_f, up_f, un_f, um_b, up_b, un_b, cw_ref, cb_ref, wax_f, wax_b, ba_ref, bx_ref, lam_ref,
                     hsf_ref, hsb_ref, ext, af_s, bf_s, ab_s, bb_s, carry, *, tt, nt, nct, wl, nb):
    j = pl.program_id(1)

    @pl.when(j == 0)
    def _():
        carry[...] = jnp.zeros_like(carry)

    def coeffs(um, up, un, t, d, wax_ref, a_s, b_s):
        prev_ok = jnp.logical_and(t != 0, t != nct)
        next_ok = jnp.logical_and(t != nct - 1, t != nt - 1)
        ext[:, SUBLANES:SUBLANES + tt, :] = um[...]
        ext[:, 0:SUBLANES, :] = jnp.where(prev_ok, up[...], 0.0)
        ext[:, SUBLANES + tt:2 * SUBLANES + tt, :] = jnp.where(next_ok, un[...], 0.0)
        uc = cb_ref[...].reshape(1, 1, wl)
        for k in range(4):
            uc = uc + cw_ref[k:k + 1, :].reshape(1, 1, wl) * ext[:, SUBLANES - 2 + k:SUBLANES - 2 + k + tt, :]
        sp = jax.nn.softplus(-lam_ref[d:d + 1, :])
        for hb in range(wl // LANES):
            sl = slice(hb * LANES, (hb + 1) * LANES)
            ub = uc[:, :, sl].reshape(nb * tt, LANES)
            pre = jnp.dot(ub.astype(BF16), wax_ref[hb], preferred_element_type=F32)
            r = jax.nn.sigmoid(pre[:, :LANES] + ba_ref[d:d + 1, sl])
            i = jax.nn.sigmoid(pre[:, LANES:] + bx_ref[d:d + 1, sl])
            log_a = (-LRU_C) * r * sp[:, sl]
            a = jnp.exp(log_a)
            b = jnp.sqrt(-jnp.tanh(log_a) * (a * a + 1.0)) * (i * ub)
            a_s[:, :, sl] = a.reshape(nb, tt, LANES)
            b_s[:, :, sl] = b.reshape(nb, tt, LANES)

    tb = _lru_bwd_tile(j, nt, nct)
    coeffs(um_f, up_f, un_f, j, 0, wax_f, af_s, bf_s)
    coeffs(um_b, up_b, un_b, tb, 1, wax_b, ab_s, bb_s)

    def step(t, c):
        hf, hb = c
        hf = af_s[:, pl.ds(t, 1), :].reshape(nb, wl) * hf + bf_s[:, pl.ds(t, 1), :].reshape(nb, wl)
        hsf_ref[:, pl.ds(t, 1), :] = hf.reshape(nb, 1, wl)
        r = tt - 1 - t
        hb = ab_s[:, pl.ds(r, 1), :].reshape(nb, wl) * hb + bb_s[:, pl.ds(r, 1), :].reshape(nb, wl)
        hsb_ref[:, pl.ds(r, 1), :] = hb.reshape(nb, 1, wl)
        return hf, hb

    hf, hb = lax.fori_loop(0, tt, step, (carry[0], carry[1]), unroll=8)
    carry[0] = hf
    carry[1] = hb


def _lru_scan(gu, conv_w, conv_b, w_a, b_a, w_x, b_x, lam, ctx_len):
    nb, s, d2 = gu.shape
    d = d2 // 2
    tt = 256
    assert s % tt == 0 and ctx_len % tt == 0 and d % (RNN_BLOCKS * LANES) == 0 and d // RNN_BLOCKS == LANES
    nt, nct = s // tt, ctx_len // tt
    wl = 2 * LANES
    ncg = d // wl
    coff = d // wl
    r8 = tt // SUBLANES
    last8 = s // SUBLANES - 1
    wax = jnp.concatenate([w_a, w_x], axis=-1).astype(BF16)

    def main_f(cg, j): return (0, j, coff + cg)
    def prev_f(cg, j): return (0, jnp.maximum(j * r8 - 1, 0), coff + cg)
    def next_f(cg, j): return (0, jnp.minimum((j + 1) * r8, last8), coff + cg)
    def tb_(j): return _lru_bwd_tile(j, nt, nct)
    def main_b(cg, j): return (0, tb_(j), coff + cg)
    def prev_b(cg, j): return (0, jnp.maximum(tb_(j) * r8 - 1, 0), coff + cg)
    def next_b(cg, j): return (0, jnp.minimum((tb_(j) + 1) * r8, last8), coff + cg)

    big = lambda im: pl.BlockSpec((nb, tt, wl), im)
    halo = lambda im: pl.BlockSpec((nb, SUBLANES, wl), im)
    vec2 = pl.BlockSpec((2, wl), lambda cg, j: (0, cg))
    return pl.pallas_call(
        functools.partial(_lru_scan_kernel, tt=tt, nt=nt, nct=nct, wl=wl, nb=nb),
        grid=(ncg, nt),
        in_specs=[
            big(main_f), halo(prev_f), halo(next_f), big(main_b), halo(prev_b), halo(next_b),
            pl.BlockSpec((4, wl), lambda cg, j: (0, cg)),
            pl.BlockSpec((1, wl), lambda cg, j: (0, cg)),
            pl.BlockSpec((None, wl // LANES, LANES, 2 * LANES), lambda cg, j: (0, cg, 0, 0)),
            pl.BlockSpec((None, wl // LANES, LANES, 2 * LANES), lambda cg, j: (1, cg, 0, 0)),
            vec2, vec2, vec2,
        ],
        out_specs=[
            pl.BlockSpec((nb, tt, wl), lambda cg, j: (0, j, cg)),
            pl.BlockSpec((nb, tt, wl), lambda cg, j: (0, tb_(j), cg)),
        ],
        out_shape=[jax.ShapeDtypeStruct((nb, s, d), F32), jax.ShapeDtypeStruct((nb, s, d), F32)],
        scratch_shapes=[
            pltpu.VMEM((nb, tt + 2 * SUBLANES, wl), F32),
            pltpu.VMEM((nb, tt, wl), F32), pltpu.VMEM((nb, tt, wl), F32),
            pltpu.VMEM((nb, tt, wl), F32), pltpu.VMEM((nb, tt, wl), F32),
            pltpu.VMEM((2, nb, wl), F32),
        ],
        compiler_params=_cparams(("parallel", "arbitrary")),
        name="lru_scan",
    )(gu, gu, gu, gu, gu, gu, conv_w, conv_b.reshape(1, d), wax, wax, b_a, b_x, lam)


def _rope_tables(seq, ctx_len):
    q = HEAD_DIM // 4
    inv_freq = ROPE_BASE ** (-jnp.arange(q, dtype=F32) / q)
    pos = jnp.arange(seq, dtype=jnp.int32)
    row = (pos // GRID_W).astype(F32)[:, None] * inv_freq
    col = (pos % GRID_W).astype(F32)[:, None] * inv_freq
    ang = jnp.concatenate([row, row, col, col], axis=-1)
    sign = jnp.concatenate([-jnp.ones((q,), F32), jnp.ones((q,), F32)] * 2)
    cos = jnp.concatenate([jnp.ones((ctx_len, HEAD_DIM), F32), jnp.cos(ang)], axis=0)
    sin = jnp.concatenate([jnp.zeros((ctx_len, HEAD_DIM), F32), jnp.sin(ang) * sign], axis=0)
    return jnp.tile(cos, (1, 2)), jnp.tile(sin, (1, 2))


def _qkv_kernel(x_ref, mod_ref, g_ref, w_ref, cos_ref, sin_ref, o_ref, *, tm, ctx_len, n_q, n_rope):
    j = pl.program_id(1)
    x = x_ref[...]
    is_ctx = _row_is_ctx(j, tm, ctx_len, x.shape)
    h = _adanorm(x, g_ref[...], _sel_mod(mod_ref, is_ctx, 0), _sel_mod(mod_ref, is_ctx, 1))
    acc = jnp.dot(h.astype(BF16), w_ref[...], preferred_element_type=F32)
    cos = cos_ref[...]
    sin = sin_ref[...]
    lane = lax.broadcasted_iota(jnp.int32, (tm, LANES), 1)
    first_half = (lane % (HEAD_DIM // 2)) < (HEAD_DIM // 4)
    q_scale = HEAD_DIM ** -0.5
    for c in range(acc.shape[1] // LANES):
        v = acc[:, c * LANES:(c + 1) * LANES]
        if c < n_rope:
            partner = jnp.where(first_half, pltpu.roll(v, LANES - HEAD_DIM // 4, 1), pltpu.roll(v, HEAD_DIM // 4, 1))
            v = v * cos + partner * sin
            if c < n_q:
                v = v * q_scale
        o_ref[:, c * LANES:(c + 1) * LANES] = v.astype(o_ref.dtype)


def _qkv_proj(x_all, mods_i, g, w, cos_t, sin_t, ctx_len):
    b, s, d = x_all.shape
    n = w.shape[1]
    tm = _pick_tile(s, (544, 384, 256, 128))
    n_q = N_HEADS * HEAD_DIM // LANES
    n_rope = (N_HEADS + N_KV_HEADS) * HEAD_DIM // LANES
    return pl.pallas_call(
        functools.partial(_qkv_kernel, tm=tm, ctx_len=ctx_len, n_q=n_q, n_rope=n_rope),
        grid=(b, s // tm),
        in_specs=[
            pl.BlockSpec((None, tm, d), lambda bi, j: (bi, j, 0)),
            pl.BlockSpec((None, 2, 8, d), lambda bi, j: (bi, 0, 0, 0)),
            pl.BlockSpec((1, d), lambda bi, j: (0, 0)),
            pl.BlockSpec((d, n), lambda bi, j: (0, 0)),
            pl.BlockSpec((tm, LANES), lambda bi, j: (j, 0)),
            pl.BlockSpec((tm, LANES), lambda bi, j: (j, 0)),
        ],
        out_specs=pl.BlockSpec((None, tm, n), lambda bi, j: (bi, j, 0)),
        out_shape=jax.ShapeDtypeStruct((b, s, n), BF16),
        compiler_params=_cparams(("parallel", "parallel")),
        name="qkv_proj",
    )(x_all, mods_i, g.reshape(1, d), w.astype(BF16), cos_t, sin_t)


def _attn_kernel(q_ref, kp_ref, km_ref, kn_ref, kc_ref, vp_ref, vm_ref, vn_ref, vc_ref, sink_ref, o_ref,
                 *, tq, nct, seq, ctx_len):
    j = pl.program_id(1)
    k_all = jnp.concatenate([kp_ref[...], km_ref[...], kn_ref[...], kc_ref[...]], axis=0)
    v_all = jnp.concatenate([vp_ref[...], vm_ref[...], vn_ref[...], vc_ref[...]], axis=0)
    n_loc = 2 * tq
    n_keys = n_loc + ctx_len
    row = lax.broadcasted_iota(jnp.int32, (tq, n_keys), 0)
    col = lax.broadcasted_iota(jnp.int32, (tq, n_keys), 1)
    p0 = (j - nct) * tq
    kpos = p0 - tq // 2 + col
    band = jnp.abs(row - (col - tq // 2)) <= WINDOW
    in_seq = jnp.logical_and(kpos >= 0, kpos < seq)
    valid_loc = jnp.logical_and(jnp.logical_and(band, in_seq), j >= nct)
    valid = jnp.logical_or(col >= n_loc, valid_loc)
    group = N_HEADS // N_KV_HEADS
    outs = []
    for h in range(N_HEADS):
        kvh = h // group
        qh = q_ref[:, h * HEAD_DIM:(h + 1) * HEAD_DIM]
        kk = k_all[:, kvh * HEAD_DIM:(kvh + 1) * HEAD_DIM]
        vv = v_all[:, kvh * HEAD_DIM:(kvh + 1) * HEAD_DIM]
        s = lax.dot_general(qh, kk, (((1,), (1,)), ((), ())), preferred_element_type=F32)
        s = jnp.where(valid, s, -1e30)
        sink = sink_ref[h:h + 1, 0:1]
        m = jnp.maximum(jnp.max(s, axis=1, keepdims=True), sink)
        p = jnp.exp(s - m)
        denom = jnp.sum(p, axis=1, keepdims=True) + jnp.exp(sink - m)
        o = jnp.dot(p.astype(BF16), vv, preferred_element_type=F32)
        outs.append((o / denom).astype(o_ref.dtype))
    o_ref[...] = jnp.concatenate(outs, axis=1)


def _attention(qkv, sinks, ctx_len):
    b, s, _ = qkv.shape
    seq = s - ctx_len
    tq = 2 * WINDOW
    half = tq // 2
    assert ctx_len % tq == 0 and seq % tq == 0
    nct = ctx_len // tq
    dq = N_HEADS * HEAD_DIM
    dkv = N_KV_HEADS * HEAD_DIM
    kcol, vcol = dq // dkv, dq // dkv + 1
    last_half = s // half - 1
    sink_b = jnp.broadcast_to(sinks.astype(F32)[:, None], (N_HEADS, LANES))

    def prev(col): return lambda bi, j: (bi, jnp.maximum(2 * j - 1, 0), col)
    def main(col): return lambda bi, j: (bi, j, col)
    def nxt(col): return lambda bi, j: (bi, jnp.minimum(2 * j + 2, last_half), col)
    def ctx(col): return lambda bi, j: (bi, 0, col)
    kv_specs = lambda col: [
        pl.BlockSpec((None, half, dkv), prev(col)), pl.BlockSpec((None, tq, dkv), main(col)),
        pl.BlockSpec((None, half, dkv), nxt(col)), pl.BlockSpec((None, ctx_len, dkv), ctx(col)),
    ]
    return pl.pallas_call(
        functools.partial(_attn_kernel, tq=tq, nct=nct, seq=seq, ctx_len=ctx_len),
        grid=(b, s // tq),
        in_specs=[pl.BlockSpec((None, tq, dq), lambda bi, j: (bi, j, 0))] + kv_specs(kcol) + kv_specs(vcol)
        + [pl.BlockSpec((N_HEADS, LANES), lambda bi, j: (0, 0))],
        out_specs=pl.BlockSpec((None, tq, dq), lambda bi, j: (bi, j, 0)),
        out_shape=jax.ShapeDtypeStruct((b, s, dq), BF16),
        compiler_params=_cparams(("parallel", "parallel")),
        name="swa_attention",
    )(qkv, *([qkv] * 8), sink_b)


def _dwconv3_kernel(um, up, un, cw_ref, cb_ref, g1_ref, g2_ref, v_ref, ext, *, tt, nt, nct, c):
    t = pl.program_id(1)
    prev_ok = jnp.logical_and(t != 0, t != nct)
    next_ok = jnp.logical_and(t != nct - 1, t != nt - 1)
    ext[SUBLANES:SUBLANES + tt, :] = um[...]
    ext[0:SUBLANES, :] = jnp.where(prev_ok, up[...], 0.0)
    ext[SUBLANES + tt:2 * SUBLANES + tt, :] = jnp.where(next_ok, un[...], 0.0)
    y = cb_ref[...]
    for k in range(3):
        y = y + cw_ref[k:k + 1, :] * ext[SUBLANES - 1 + k:SUBLANES - 1 + k + tt, :]
    g1_ref[...] = y[:, 0:c]
    g2_ref[...] = y[:, c:2 * c]
    v_ref[...] = y[:, 2 * c:3 * c]


def _dwconv3(u0, conv_w, conv_b, ctx_len):
    b, s, c3 = u0.shape
    c = c3 // 3
    tt = 256
    assert s % tt == 0 and ctx_len % tt == 0
    nt, nct = s // tt, ctx_len // tt
    r8 = tt // SUBLANES
    last8 = s // SUBLANES - 1
    out = jax.ShapeDtypeStruct((b, s, c), F32)
    ospec = pl.BlockSpec((None, tt, c), lambda bi, t: (bi, t, 0))
    return pl.pallas_call(
        functools.partial(_dwconv3_kernel, tt=tt, nt=nt, nct=nct, c=c),
        grid=(b, nt),
        in_specs=[
            pl.BlockSpec((None, tt, c3), lambda bi, t: (bi, t, 0)),
            pl.BlockSpec((None, SUBLANES, c3), lambda bi, t: (bi, jnp.maximum(t * r8 - 1, 0), 0)),
            pl.BlockSpec((None, SUBLANES, c3), lambda bi, t: (bi, jnp.minimum((t + 1) * r8, last8), 0)),
            pl.BlockSpec((3, c3), lambda bi, t: (0, 0)),
            pl.BlockSpec((1, c3), lambda bi, t: (0, 0)),
        ],
        out_specs=[ospec, ospec, ospec],
        out_shape=[out, out, out],
        scratch_shapes=[pltpu.VMEM((tt + 2 * SUBLANES, c3), F32)],
        compiler_params=_cparams(("parallel", "parallel")),
        name="hyena_dwconv",
    )(u0, u0, u0, conv_w, conv_b.reshape(1, c3))


def _filter_features(seq, ctx_len):
    def feats(length):
        t = jnp.linspace(0.0, 1.0, length, dtype=F32)[:, None]
        omega = (2.0 * math.pi / length) * jnp.arange(length, dtype=F32)[:, None]
        bands = jnp.linspace(1e-4, FILTER_BANDS - 1, FILTER_BANDS, dtype=F32)[None, :]
        return jnp.concatenate([t, jnp.cos(bands * omega), -jnp.sin(bands * omega)], axis=-1)
    z = jnp.concatenate([feats(ctx_len), feats(seq)], axis=0)
    return jnp.pad(z, ((0, 0), (0, 40 - z.shape[1])))


def _filter_kernel(z_ref, w1_ref, b1_ref, w2_ref, b2_ref, w3_ref, b3_ref, fr_ref, w4_ref, dl_ref, hw_ref, sum_ref,
                   *, tm, ctx_len, c):
    j = pl.program_id(0)

    @pl.when(j == 0)
    def _():
        sum_ref[...] = jnp.zeros_like(sum_ref)

    z = z_ref[...]
    fr = fr_ref[...]
    dot = functools.partial(jnp.dot, preferred_element_type=F32, precision=HIGHEST)
    hid = jnp.sin(fr * (dot(z, w1_ref[...]) + b1_ref[...]))
    hid = jnp.sin(fr * (dot(hid, w2_ref[...]) + b2_ref[...]))
    hid = jnp.sin(fr * (dot(hid, w3_ref[...]) + b3_ref[...]))
    h = dot(hid, w4_ref[...])
    window = jnp.exp(-z[:, 0:1] * jnp.abs(dl_ref[...]))
    rows = j * tm + lax.broadcasted_iota(jnp.int32, (tm, c), 0)
    first = jnp.logical_or(rows == 0, rows == ctx_len)
    is_ctx = j * tm < ctx_len
    srow = jnp.where(is_ctx, 0, 1)
    for q in range(4):
        hq = h[:, q * c:(q + 1) * c] * window
        if q % 2 == 1:
            hq = jnp.where(first, 0.0, hq)
        hw_ref[:, q * c:(q + 1) * c] = hq
        part = jnp.sum(jnp.abs(hq), axis=0, keepdims=True)
        for r in range(2):
            sum_ref[r:r + 1, q * c:(q + 1) * c] += jnp.where(srow == r, part, 0.0)


def _hyena_filters(seq, ctx_len, w1, b1, w2, b2, w3, b3, freq, w4, c):
    s = seq + ctx_len
    z = _filter_features(seq, ctx_len)
    tm = 256
    assert ctx_len % tm == 0 and seq % tm == 0
    fw = w1.shape[1]
    w1p = jnp.pad(w1.astype(F32), ((0, 40 - w1.shape[0]), (0, 0)))
    deltas = jnp.linspace(math.log(DECAY_TARGET) / SLOW_DECAY_PCT, math.log(DECAY_TARGET) / FAST_DECAY_PCT, c,
                          dtype=F32)[None, :]
    full = lambda shape: pl.BlockSpec(shape, lambda j: tuple(0 for _ in shape))
    return pl.pallas_call(
        functools.partial(_filter_kernel, tm=tm, ctx_len=ctx_len, c=c),
        grid=(s // tm,),
        in_specs=[
            pl.BlockSpec((tm, 40), lambda j: (j, 0)),
            full((40, fw)), full((1, fw)), full((fw, fw)), full((1, fw)), full((fw, fw)), full((1, fw)),
            full((1, fw)), full((fw, 4 * c)), full((1, c)),
        ],
        out_specs=[pl.BlockSpec((tm, 4 * c), lambda j: (j, 0)), pl.BlockSpec((SUBLANES, 4 * c), lambda j: (0, 0))],
        out_shape=[jax.ShapeDtypeStruct((s, 4 * c), F32), jax.ShapeDtypeStruct((SUBLANES, 4 * c), F32)],
        compiler_params=_cparams(("arbitrary",)),
        name="hyena_filter_mlp",
    )(z, w1p, b1.reshape(1, fw), w2, b2.reshape(1, fw), w3, b3.reshape(1, fw), freq.reshape(1, fw), w4, deltas)


def _fft_tables(seq, ctx_len):
    n1 = FFT_N1
    rc, rl = ctx_len // n1, seq // n1
    n2c, n2l = 2 * rc, 2 * rl
    k2n = n2l + n2c
    r = rc + rl

    def cis(num, den):
        ang = (-2.0 * math.pi / den) * (num % den).astype(F32)
        return jnp.cos(ang), jnp.sin(ang)

    ar = jnp.arange
    fl_r, fl_i = cis(ar(n2l)[:, None] * ar(rl)[None, :], n2l)
    fc_r, fc_i = cis(ar(n2c)[:, None] * ar(rc)[None, :], n2c)
    zl, zc = jnp.zeros((n2l, rc), F32), jnp.zeros((n2c, rl), F32)
    f_r = jnp.concatenate([jnp.concatenate([zl, fl_r], 1), jnp.concatenate([fc_r, zc], 1)], 0)
    f_i = jnp.concatenate([jnp.concatenate([zl, fl_i], 1), jnp.concatenate([fc_i, zc], 1)], 0)
    scale = jnp.concatenate([jnp.full((n2l,), 1.0 / (n1 * n2l), F32), jnp.full((n2c,), 1.0 / (n1 * n2c), F32)])
    g_r, g_i = (f_r * scale[:, None]).T, (f_i * scale[:, None]).T
    tl_r, tl_i = cis(ar(n2l)[:, None] * ar(n1)[None, :], n1 * n2l)
    tc_r, tc_i = cis(ar(n2c)[:, None] * ar(n1)[None, :], n1 * n2c)
    tw_r, tw_i = jnp.concatenate([tl_r, tc_r], 0), jnp.concatenate([tl_i, tc_i], 0)
    bl = lambda a: jnp.broadcast_to(a[:, :, None], a.shape + (LANES,))
    f1_r, f1_i = cis(ar(n1)[:, None] * ar(n1)[None, :], n1)
    return dict(f_r=f_r, f_i=f_i, g_r=g_r, g_i=g_i, f1_r=f1_r, f1_i=f1_i,
                ta_r=bl(tw_r.T), ta_i=bl(tw_i.T),
                tb_r=bl(tw_r), tb_i=bl(tw_i),
                k2n=k2n, n2l=n2l, r=r)


def _hdot(a, b):
    return jnp.dot(a, b, preferred_element_type=F32, precision=HIGHEST)


def _fft_a_kernel(x_ref, fr_ref, fi_ref, tr_ref, ti_ref, o_ref, *, packed, q, c):
    fr, fi = fr_ref[...], fi_ref[...]
    if packed:
        xr, xi = x_ref[0], x_ref[1]
        a_r = _hdot(fr, xr) - _hdot(fi, xi)
        a_i = _hdot(fi, xr) + _hdot(fr, xi)
    else:
        x = x_ref[...]
        a_r, a_i = _hdot(fr, x), _hdot(fi, x)
    for qq in range(q):
        tr, ti = tr_ref[qq], ti_ref[qq]
        for l in range(c // LANES):
            sl = slice(qq * c + l * LANES, qq * c + (l + 1) * LANES)
            cr, ci = a_r[:, sl], a_i[:, sl]
            o_ref[0, :, sl] = cr * tr - ci * ti
            o_ref[1, :, sl] = cr * ti + ci * tr


def _fft_stage_a(xv, tb, packed, c):
    k2n, r = tb["k2n"], tb["r"]
    q = max(1, 4096 // c)
    ncb = FFT_N1 // q
    mat = pl.BlockSpec((k2n, r), lambda *a: (0, 0))
    if packed:
        p = xv.shape[1]
        grid = (p, ncb)
        x_spec = pl.BlockSpec((2, None, r, q * c), lambda pi, cb: (0, pi, 0, cb))
        t_spec = pl.BlockSpec((q, k2n, LANES), lambda pi, cb: (cb, 0, 0))
        o_spec = pl.BlockSpec((None, 2, k2n, q * c), lambda pi, cb: (pi, 0, 0, cb))
        o_shape = jax.ShapeDtypeStruct((p, 2, k2n, FFT_N1 * c), F32)
        sem = ("parallel", "parallel")
    else:
        grid = (ncb,)
        x_spec = pl.BlockSpec((r, q * c), lambda cb: (0, cb))
        t_spec = pl.BlockSpec((q, k2n, LANES), lambda cb: (cb, 0, 0))
        o_spec = pl.BlockSpec((2, k2n, q * c), lambda cb: (0, 0, cb))
        o_shape = jax.ShapeDtypeStruct((2, k2n, FFT_N1 * c), F32)
        sem = ("parallel",)
    return pl.pallas_call(
        functools.partial(_fft_a_kernel, packed=packed, q=q, c=c),
        grid=grid,
        in_specs=[x_spec, mat, mat, t_spec, t_spec],
        out_specs=o_spec,
        out_shape=o_shape,
        compiler_params=_cparams(sem),
        name="hyena_fft_a",
    )(xv, tb["f_r"], tb["f_i"], tb["ta_r"], tb["ta_i"])


def _fft_bf_kernel(a_ref, f1r_ref, f1i_ref, o_ref):
    ar_, ai_ = a_ref[0], a_ref[1]
    f1r, f1i = f1r_ref[...], f1i_ref[...]
    o_ref[0] = _hdot(f1r, ar_) - _hdot(f1i, ai_)
    o_ref[1] = _hdot(f1i, ar_) + _hdot(f1r, ai_)


def _fft_stage_b_filter(af, tb, c4):
    k2n = tb["k2n"]
    cc = min(c4, 1024)
    mat = pl.BlockSpec((FFT_N1, FFT_N1), lambda k, cb: (0, 0))
    blk = pl.BlockSpec((2, FFT_N1, cc), lambda k, cb: (0, k, cb))
    return pl.pallas_call(
        _fft_bf_kernel,
        grid=(k2n, c4 // cc),
        in_specs=[blk, mat, mat],
        out_specs=blk,
        out_shape=jax.ShapeDtypeStruct(af.shape, F32),
        compiler_params=_cparams(("parallel", "parallel")),
        name="hyena_fft_b_filter",
    )(af, tb["f1_r"], tb["f1_i"])


def _fft_mid_kernel(a_ref, h0_ref, h1_ref, s0_ref, s1_ref, f1r_ref, f1i_ref, tr_ref, ti_ref, o_ref, *, n2l, cc):
    k2 = pl.program_id(1)
    f1r, f1i = f1r_ref[...], f1i_ref[...]
    ar_, ai_ = a_ref[0], a_ref[1]
    xr = _hdot(f1r, ar_) - _hdot(f1i, ai_)
    xi = _hdot(f1i, ar_) + _hdot(f1r, ai_)
    norm = s0_ref[...] + s1_ref[...]
    inv = 1.0 / jnp.where(k2 >= n2l, norm[0:1, :], norm[1:2, :])
    kr = (h0_ref[0] + h1_ref[0]) * inv
    ki = (h0_ref[1] - h1_ref[1]) * inv
    yr = xr * kr - xi * ki
    yi = xr * ki + xi * kr
    br = _hdot(f1r, yr) + _hdot(f1i, yi)
    bi = _hdot(f1r, yi) - _hdot(f1i, yr)
    tr, ti = tr_ref[...], ti_ref[...]
    for l in range(cc // LANES):
        sl = slice(l * LANES, (l + 1) * LANES)
        cr, ci = br[:, sl], bi[:, sl]
        o_ref[0, :, sl] = cr * tr + ci * ti
        o_ref[1, :, sl] = ci * tr - cr * ti


def _fft_mid(a, hspec, sums, tb, order, c):
    p = a.shape[0]
    k2n, n2l = tb["k2n"], tb["n2l"]
    cc = min(c, 1024)
    ncc = c // cc
    mat = pl.BlockSpec((FFT_N1, FFT_N1), lambda pi, k, cb: (0, 0))
    blk = pl.BlockSpec((None, 2, FFT_N1, cc), lambda pi, k, cb: (pi, 0, k, cb))
    hs = lambda side: pl.BlockSpec((2, FFT_N1, cc), lambda pi, k, cb: (0, k, (2 * order + side) * ncc + cb))
    ss = lambda side: pl.BlockSpec((SUBLANES, cc), lambda pi, k, cb: (0, (2 * order + side) * ncc + cb))
    tw = pl.BlockSpec((None, FFT_N1, LANES), lambda pi, k, cb: (k, 0, 0))
    return pl.pallas_call(
        functools.partial(_fft_mid_kernel, n2l=n2l, cc=cc),
        grid=(p, k2n, ncc),
        in_specs=[blk, hs(0), hs(1), ss(0), ss(1), mat, mat, tw, tw],
        out_specs=blk,
        out_shape=jax.ShapeDtypeStruct(a.shape, F32),
        compiler_params=_cparams(("parallel", "parallel", "parallel")),
        name="hyena_fft_mid",
    )(a, hspec, hspec, sums, sums, tb["f1_r"], tb["f1_i"], tb["tb_r"], tb["tb_i"])


def _fft_inv_kernel(b_ref, gr_ref, gi_ref, g_ref, v_ref, skip_ref, o_ref, *, q, c):
    gr, gi = gr_ref[...], gi_ref[...]
    br, bi = b_ref[0], b_ref[1]
    yr = _hdot(gr, br) + _hdot(gi, bi)
    yi = _hdot(gr, bi) - _hdot(gi, br)
    skip = skip_ref[...]
    for qq in range(q):
        sl = slice(qq * c, (qq + 1) * c)
        o_ref[0, :, sl] = (g_ref[0, :, sl] * (yr[:, sl] + skip * v_ref[0, :, sl])).astype(o_ref.dtype)
        o_ref[1, :, sl] = (g_ref[1, :, sl] * (yi[:, sl] + skip * v_ref[1, :, sl])).astype(o_ref.dtype)


def _fft_inverse_gate(bv, gv, vv, skip, tb, c):
    p = bv.shape[0]
    k2n, r = tb["k2n"], tb["r"]
    q = max(1, 4096 // c)
    ncb = FFT_N1 // q
    mat = pl.BlockSpec((r, k2n), lambda pi, cb: (0, 0))
    xs = pl.BlockSpec((2, None, r, q * c), lambda pi, cb: (0, pi, 0, cb))
    return pl.pallas_call(
        functools.partial(_fft_inv_kernel, q=q, c=c),
        grid=(p, ncb),
        in_specs=[pl.BlockSpec((None, 2, k2n, q * c), lambda pi, cb: (pi, 0, 0, cb)), mat, mat, xs, xs,
                  pl.BlockSpec((1, c), lambda pi, cb: (0, 0))],
        out_specs=xs,
        out_shape=jax.ShapeDtypeStruct(gv.shape, F32),
        compiler_params=_cparams(("parallel", "parallel")),
        name="hyena_fft_inv",
    )(bv, tb["g_r"], tb["g_i"], gv, vv, skip.reshape(1, c))


def _hyena_mix(u0, conv_w, conv_b, fw1, fb1, fw2, fb2, fw3, fb3, ffreq, fw4, skip, ctx_len):
    b, s, c3 = u0.shape
    c = c3 // 3
    seq = s - ctx_len
    assert b % 2 == 0 and seq % FFT_N1 == 0 and ctx_len % FFT_N1 == 0
    p = b // 2
    r = s // FFT_N1
    g1, g2, v = _dwconv3(u0, conv_w, conv_b, ctx_len)
    hw, sums = _hyena_filters(seq, ctx_len, fw1, fb1, fw2, fb2, fw3, fb3, ffreq, fw4, c)
    tb = _fft_tables(seq, ctx_len)
    k2n = tb["k2n"]
    af = _fft_stage_a(hw.reshape(r, FFT_N1 * 4 * c), tb, False, 4 * c)
    hspec = _fft_stage_b_filter(af.reshape(2, k2n * FFT_N1, 4 * c), tb, 4 * c)
    view = lambda t: t.reshape(2, p, r, FFT_N1 * c)
    z = v
    for order, gate in ((0, g1), (1, g2)):
        a = _fft_stage_a(view(z), tb, True, c)
        bm = _fft_mid(a.reshape(p, 2, k2n * FFT_N1, c), hspec, sums, tb, order, c)
        z = _fft_inverse_gate(bm.reshape(p, 2, k2n, FFT_N1 * c), view(gate), view(z), skip[order], tb, c)
        z = z.reshape(b, s, c)
    return z


def _router_kernel(h_ref, r_ref, g_ref, *, ne):
    logits = jnp.dot(h_ref[...], r_ref[...], preferred_element_type=F32, precision=HIGHEST)
    lane = lax.broadcasted_iota(jnp.int32, logits.shape, 1)
    neg = -jnp.inf
    logits = jnp.where(lane < ne, logits, neg)
    m1 = jnp.max(logits, axis=1, keepdims=True)
    i1 = jnp.min(jnp.where(logits == m1, lane, LANES), axis=1, keepdims=True)
    rest = jnp.where(lane == i1, neg, logits)
    m2 = jnp.max(rest, axis=1, keepdims=True)
    i2 = jnp.min(jnp.where(rest == m2, lane, LANES), axis=1, keepdims=True)
    e2 = jnp.exp(m2 - m1)
    w1 = 1.0 / (1.0 + e2)
    g_ref[...] = jnp.where(lane == i1, w1, 0.0) + jnp.where(lane == i2, e2 * w1, 0.0)


def _router(h2, router):
    b, s, d = h2.shape
    ne = router.shape[1]
    tm = _pick_tile(s, (544, 384, 256, 128))
    rp = jnp.pad(router.astype(F32), ((0, 0), (0, LANES - ne)))
    return pl.pallas_call(
        functools.partial(_router_kernel, ne=ne),
        grid=(b, s // tm),
        in_specs=[pl.BlockSpec((None, tm, d), lambda bi, j: (bi, j, 0)), pl.BlockSpec((d, LANES), lambda bi, j: (0, 0))],
        out_specs=pl.BlockSpec((None, tm, LANES), lambda bi, j: (bi, j, 0)),
        out_shape=jax.ShapeDtypeStruct((b, s, LANES), F32),
        compiler_params=_cparams(("parallel", "parallel")),
        name="moe_router",
    )(h2, rp)


def _moe_dense_kernel(h_ref, x_ref, mod_ref, gate_ref, wg_ref, wu_ref, wd_ref, o_ref, acc, *, tm, ctx_len, ne, nf):
    j, e, f = pl.program_id(1), pl.program_id(2), pl.program_id(3)

    @pl.when(jnp.logical_and(e == 0, f == 0))
    def _():
        acc[...] = jnp.zeros_like(acc)

    h = h_ref[...].astype(BF16)
    a = jnp.dot(h, wg_ref[...], preferred_element_type=F32)
    u = jnp.dot(h, wu_ref[...], preferred_element_type=F32)
    gate = gate_ref[...]
    lane = lax.broadcasted_iota(jnp.int32, gate.shape, 1)
    ge = jnp.sum(jnp.where(lane == e, gate, 0.0), axis=1, keepdims=True)
    act = (a * jax.nn.sigmoid(a) * u * ge).astype(BF16)
    acc[...] += jnp.dot(act, wd_ref[...], preferred_element_type=F32)

    @pl.when(jnp.logical_and(e == ne - 1, f == nf - 1))
    def _():
        x = x_ref[...]
        is_ctx = _row_is_ctx(j, tm, ctx_len, x.shape)
        o_ref[...] = x + _sel_mod(mod_ref, is_ctx, 5) * acc[...]


def _moe_dense(h2, x_all, mods_i, gate, wg, wu, wd, ctx_len):
    b, s, d = x_all.shape
    ne, _, f = wg.shape
    tm = _pick_tile(s, (1088, 768, 512, 256))
    fc = _pick_tile(f, (512, 256, 128))
    nf = f // fc
    row = lambda bi, j, e, k: (bi, j, 0)
    return pl.pallas_call(
        functools.partial(_moe_dense_kernel, tm=tm, ctx_len=ctx_len, ne=ne, nf=nf),
        grid=(b, s // tm, ne, nf),
        in_specs=[
            pl.BlockSpec((None, tm, d), row),
            pl.BlockSpec((None, tm, d), row),
            pl.BlockSpec((None, 2, 8, d), lambda bi, j, e, k: (bi, 0, 0, 0)),
            pl.BlockSpec((None, tm, LANES), row),
            pl.BlockSpec((None, d, fc), lambda bi, j, e, k: (e, 0, k)),
            pl.BlockSpec((None, d, fc), lambda bi, j, e, k: (e, 0, k)),
            pl.BlockSpec((None, fc, d), lambda bi, j, e, k: (e, k, 0)),
        ],
        out_specs=pl.BlockSpec((None, tm, d), row),
        out_shape=jax.ShapeDtypeStruct((b, s, d), F32),
        scratch_shapes=[pltpu.VMEM((tm, d), F32)],
        compiler_params=_cparams(("parallel", "parallel", "arbitrary", "arbitrary")),
        name="moe_experts",
    )(h2, x_all, mods_i, gate, wg.astype(BF16), wu.astype(BF16), wd.astype(BF16))


def _final_norm_kernel(x_ref, g_ref, o_ref):
    x = x_ref[...]
    ms = jnp.mean(x * x, axis=-1, keepdims=True)
    o_ref[...] = (x * lax.rsqrt(ms + EPS)) * g_ref[...]


def _final_norm(x_all, g, ctx_len):
    b, s, d = x_all.shape
    seq = s - ctx_len
    tm = _pick_tile(math.gcd(seq, ctx_len), (512, 256, 128))
    off = ctx_len // tm
    return pl.pallas_call(
        _final_norm_kernel,
        grid=(b, seq // tm),
        in_specs=[pl.BlockSpec((None, tm, d), lambda bi, j: (bi, j + off, 0)), pl.BlockSpec((1, d), lambda bi, j: (0, 0))],
        out_specs=pl.BlockSpec((None, tm, d), lambda bi, j: (bi, j, 0)),
        out_shape=jax.ShapeDtypeStruct((b, seq, d), F32),
        compiler_params=_cparams(("parallel", "parallel")),
        name="final_norm",
    )(x_all, g.reshape(1, d))


def kernel(x, c, ctx, c_ctx, ada_w, ada_b, norm_mix, norm_ffn, norm_final,
           lru_w_in, lru_conv_w, lru_conv_b, lru_w_a, lru_b_a, lru_w_x, lru_b_x, lru_lambda, lru_w_out,
           attn_w_qkv, attn_sinks, attn_w_o,
           hy_w_in, hy_b_in, hy_conv_w, hy_conv_b, hy_f_w1, hy_f_b1, hy_f_w2, hy_f_b2, hy_f_w3, hy_f_b3,
           hy_f_freq, hy_f_w4, hy_skip, hy_w_out, hy_b_out,
           ffn_w_gate, ffn_w_up, ffn_w_down,
           moe_router, moe_w_gate, moe_w_up, moe_w_down):
    depth = ada_w.shape[0]
    ctx_len, seq, d = ctx.shape[1], x.shape[1], x.shape[2]
    x_all = jnp.concatenate([ctx, x], axis=1)
    mods = _mods(c, c_ctx, ada_w, ada_b)
    zero_d = jnp.zeros((d,), F32)
    for i in range(depth):
        j = i // N_MIXERS
        moe = i % 2 == 1
        h_dtype = F32 if moe else BF16
        if i % N_MIXERS == 0:
            gu = _norm_mm(x_all, mods[i], norm_mix[i], lru_w_in[j], jnp.zeros((lru_w_in.shape[2],), F32), ctx_len, F32)
            hs_f, hs_b = _lru_scan(gu, lru_conv_w[j], lru_conv_b[j], lru_w_a[j], lru_b_a[j], lru_w_x[j], lru_b_x[j],
                                   lru_lambda[j], ctx_len)
            x_all, h2 = _lru_mm_res(hs_f, hs_b, gu, x_all, mods[i], norm_ffn[i], lru_w_out[j], ctx_len, h_dtype)
        elif i % N_MIXERS == 1:
            cos_t, sin_t = _rope_tables(seq, ctx_len)
            qkv = _qkv_proj(x_all, mods[i], norm_mix[i], attn_w_qkv[j], cos_t, sin_t, ctx_len)
            o = _attention(qkv, attn_sinks[j], ctx_len)
            x_all, h2 = _mm_res(o, x_all, mods[i], norm_ffn[i], attn_w_o[j], zero_d, ctx_len, h_dtype)
        else:
            u0 = _norm_mm(x_all, mods[i], norm_mix[i], hy_w_in[j], hy_b_in[j], ctx_len, F32)
            z = _hyena_mix(u0, hy_conv_w[j], hy_conv_b[j], hy_f_w1[j], hy_f_b1[j], hy_f_w2[j], hy_f_b2[j],
                           hy_f_w3[j], hy_f_b3[j], hy_f_freq[j], hy_f_w4[j], hy_skip[j], ctx_len)
            x_all, h2 = _mm_res(z, x_all, mods[i], norm_ffn[i], hy_w_out[j], hy_b_out[j], ctx_len, h_dtype)
        if moe:
            gate = _router(h2, moe_router[i // 2])
            x_all = _moe_dense(h2, x_all, mods[i], gate, moe_w_gate[i // 2], moe_w_up[i // 2], moe_w_down[i // 2], ctx_len)
        else:
            x_all = _ffn_dense(h2, x_all, mods[i], ffn_w_gate[i // 2], ffn_w_up[i // 2], ffn_w_down[i // 2], ctx_len)
    return _final_norm(x_all, norm_final, ctx_len)
```

```python
import functools
import math

import jax
import jax.numpy as jnp
import numpy as np
from jax import lax
from jax.experimental import pallas as pl
from jax.experimental.pallas import tpu as pltpu

F32 = jnp.float32
BF16 = jnp.bfloat16
HIGHEST = lax.Precision.HIGHEST

N_MIXERS = 3
EPS = 1e-6
GRID_W = 64
RNN_BLOCKS = 8
LRU_C = 8.0
N_HEADS = 16
N_KV_HEADS = 4
HEAD_DIM = 64
WINDOW = 128
ROPE_BASE = 10000.0
FILTER_BANDS = 16
DECAY_TARGET = 1e-2
FAST_DECAY_PCT = 0.3
SLOW_DECAY_PCT = 1.5
N_EXPERTS = 8
TOP_K = 2

LANES = 128
SUBLANES = 8
VMEM_LIMIT = 56 * 1024 * 1024
FFT_N1 = 128


def _cparams(sem, vmem=VMEM_LIMIT):
    return pltpu.CompilerParams(dimension_semantics=sem, vmem_limit_bytes=vmem)


def _pick_tile(n, cands):
    for c in cands:
        if n % c == 0:
            return c
    raise ValueError(f"no tile for {n} in {cands}")


def _row_is_ctx(j, tm, ctx_len, shape):
    rows = j * tm + lax.broadcasted_iota(jnp.int32, shape, 0)
    return rows < ctx_len


def _sel_mod(mod_ref, is_ctx, idx):
    return jnp.where(is_ctx, mod_ref[0, idx:idx + 1, :], mod_ref[1, idx:idx + 1, :])


def _adanorm(x, g, shift, scale):
    ms = jnp.mean(x * x, axis=-1, keepdims=True)
    y = x * lax.rsqrt(ms + EPS)
    return (y * g) * (1.0 + scale) + shift


def _mods(c, c_ctx, ada_w, ada_b):
    depth, d, d6 = ada_w.shape
    b = c.shape[0]
    rows = ((b + 1 + SUBLANES - 1) // SUBLANES) * SUBLANES
    cond = jnp.concatenate([c, c_ctx[None, :], jnp.zeros((rows - b - 1, d), F32)], axis=0)
    tn = _pick_tile(d6, (1024, 512, 256, 128))

    def kern(c_ref, w_ref, b_ref, o_ref):
        x = c_ref[...]
        s = x * jax.nn.sigmoid(x)
        o_ref[...] = jnp.dot(s, w_ref[...], preferred_element_type=F32, precision=HIGHEST) + b_ref[...]

    out = pl.pallas_call(
        kern,
        grid=(depth, d6 // tn),
        in_specs=[
            pl.BlockSpec((rows, d), lambda i, n: (0, 0)),
            pl.BlockSpec((None, d, tn), lambda i, n: (i, 0, n)),
            pl.BlockSpec((None, 1, tn), lambda i, n: (i, 0, n)),
        ],
        out_specs=pl.BlockSpec((None, rows, tn), lambda i, n: (i, 0, n)),
        out_shape=jax.ShapeDtypeStruct((depth, rows, d6), F32),
        compiler_params=_cparams(("parallel", "parallel")),
        name="mods",
    )(cond, ada_w, ada_b.reshape(depth, 1, d6))
    m = out.reshape(depth, rows, 6, d)
    lat = m[:, :b]
    ctx = jnp.broadcast_to(m[:, b:b + 1], lat.shape)
    both = jnp.stack([ctx, lat], axis=2)
    return jnp.pad(both, ((0, 0), (0, 0), (0, 0), (0, 2), (0, 0)))


def _norm_mm_kernel(x_ref, mod_ref, g_ref, w_ref, b_ref, o_ref, *, tm, ctx_len):
    j = pl.program_id(1)
    x = x_ref[...]
    is_ctx = _row_is_ctx(j, tm, ctx_len, x.shape)
    h = _adanorm(x, g_ref[...], _sel_mod(mod_ref, is_ctx, 0), _sel_mod(mod_ref, is_ctx, 1))
    acc = jnp.dot(h.astype(BF16), w_ref[...], preferred_element_type=F32) + b_ref[...]
    o_ref[...] = acc.astype(o_ref.dtype)


def _norm_mm(x_all, mods_i, g, w, bias, ctx_len, out_dtype):
    b, s, d = x_all.shape
    n = w.shape[1]
    tm = _pick_tile(s, (544, 384, 256, 128))
    return pl.pallas_call(
        functools.partial(_norm_mm_kernel, tm=tm, ctx_len=ctx_len),
        grid=(b, s // tm),
        in_specs=[
            pl.BlockSpec((None, tm, d), lambda bi, j: (bi, j, 0)),
            pl.BlockSpec((None, 2, 8, d), lambda bi, j: (bi, 0, 0, 0)),
            pl.BlockSpec((1, d), lambda bi, j: (0, 0)),
            pl.BlockSpec((d, n), lambda bi, j: (0, 0)),
            pl.BlockSpec((1, n), lambda bi, j: (0, 0)),
        ],
        out_specs=pl.BlockSpec((None, tm, n), lambda bi, j: (bi, j, 0)),
        out_shape=jax.ShapeDtypeStruct((b, s, n), out_dtype),
        compiler_params=_cparams(("parallel", "parallel")),
        name="norm_mm",
    )(x_all, mods_i, g.reshape(1, d), w.astype(BF16), bias.reshape(1, n).astype(F32))


def _mm_res_body(y_bf16, x_ref, mod_ref, g_ref, w_ref, b_ref, xo_ref, ho_ref, tm, ctx_len):
    j = pl.program_id(1)
    x = x_ref[...]
    is_ctx = _row_is_ctx(j, tm, ctx_len, x.shape)
    acc = jnp.dot(y_bf16, w_ref[...], preferred_element_type=F32) + b_ref[...]
    x_new = x + _sel_mod(mod_ref, is_ctx, 2) * acc
    xo_ref[...] = x_new
    h2 = _adanorm(x_new, g_ref[...], _sel_mod(mod_ref, is_ctx, 3), _sel_mod(mod_ref, is_ctx, 4))
    ho_ref[...] = h2.astype(ho_ref.dtype)


def _mm_res_kernel(y_ref, x_ref, mod_ref, g_ref, w_ref, b_ref, xo_ref, ho_ref, *, tm, ctx_len):
    _mm_res_body(y_ref[...].astype(BF16), x_ref, mod_ref, g_ref, w_ref, b_ref, xo_ref, ho_ref, tm, ctx_len)


def _lru_mm_res_kernel(hf_ref, hb_ref, gate_ref, x_ref, mod_ref, g_ref, w_ref, b_ref, xo_ref, ho_ref, *, tm, ctx_len):
    y = (hf_ref[...] + hb_ref[...]) * jax.nn.gelu(gate_ref[...])
    _mm_res_body(y.astype(BF16), x_ref, mod_ref, g_ref, w_ref, b_ref, xo_ref, ho_ref, tm, ctx_len)


def _mm_res_call(kern, ys, y_specs, x_all, mods_i, g_ffn, w, bias, ctx_len, tm, h_dtype):
    b, s, d = x_all.shape
    k = w.shape[0]
    return pl.pallas_call(
        functools.partial(kern, tm=tm, ctx_len=ctx_len),
        grid=(b, s // tm),
        in_specs=y_specs + [
            pl.BlockSpec((None, tm, d), lambda bi, j: (bi, j, 0)),
            pl.BlockSpec((None, 2, 8, d), lambda bi, j: (bi, 0, 0, 0)),
            pl.BlockSpec((1, d), lambda bi, j: (0, 0)),
            pl.BlockSpec((k, d), lambda bi, j: (0, 0)),
            pl.BlockSpec((1, d), lambda bi, j: (0, 0)),
        ],
        out_specs=[
            pl.BlockSpec((None, tm, d), lambda bi, j: (bi, j, 0)),
            pl.BlockSpec((None, tm, d), lambda bi, j: (bi, j, 0)),
        ],
        out_shape=[jax.ShapeDtypeStruct((b, s, d), F32), jax.ShapeDtypeStruct((b, s, d), h_dtype)],
        compiler_params=_cparams(("parallel", "parallel")),
        name="mm_res",
    )(*ys, x_all, mods_i, g_ffn.reshape(1, d), w.astype(BF16), bias.reshape(1, d).astype(F32))


def _mm_res(y, x_all, mods_i, g_ffn, w, bias, ctx_len, h_dtype=BF16):
    b, s, d = x_all.shape
    k = w.shape[0]
    tm = _pick_tile(s, (544, 384, 256, 128))
    spec = [pl.BlockSpec((None, tm, k), lambda bi, j: (bi, j, 0))]
    return _mm_res_call(_mm_res_kernel, [y], spec, x_all, mods_i, g_ffn, w, bias, ctx_len, tm, h_dtype)


def _lru_mm_res(hs_f, hs_b, gu, x_all, mods_i, g_ffn, w, ctx_len, h_dtype=BF16):
    b, s, d = x_all.shape
    k = w.shape[0]
    tm = _pick_tile(s, (544, 384, 256, 128))
    specs = [
        pl.BlockSpec((None, tm, k), lambda bi, j: (bi, j, 0)),
        pl.BlockSpec((None, tm, k), lambda bi, j: (bi, j, 0)),
        pl.BlockSpec((None, tm, k), lambda bi, j: (bi, j, 0)),
    ]
    return _mm_res_call(_lru_mm_res_kernel, [hs_f, hs_b, gu], specs, x_all, mods_i, g_ffn, w,
                        jnp.zeros((d,), F32), ctx_len, tm, h_dtype)


def _ffn_kernel(h_ref, x_ref, mod_ref, wg_ref, wu_ref, wd_ref, o_ref, *, tm, ctx_len):
    j = pl.program_id(1)
    h = h_ref[...]
    a = jnp.dot(h, wg_ref[...], preferred_element_type=F32)
    u = jnp.dot(h, wu_ref[...], preferred_element_type=F32)
    act = (a * jax.nn.sigmoid(a) * u).astype(BF16)
    y = jnp.dot(act, wd_ref[...], preferred_element_type=F32)
    x = x_ref[...]
    is_ctx = _row_is_ctx(j, tm, ctx_len, x.shape)
    o_ref[...] = x + _sel_mod(mod_ref, is_ctx, 5) * y


def _ffn_dense(h2, x_all, mods_i, wg, wu, wd, ctx_len):
    b, s, d = x_all.shape
    f = wg.shape[1]
    tm = _pick_tile(s, (544, 384, 256, 128))
    resident = dict(pipeline_mode=pl.Buffered(1))
    return pl.pallas_call(
        functools.partial(_ffn_kernel, tm=tm, ctx_len=ctx_len),
        grid=(b, s // tm),
        in_specs=[
            pl.BlockSpec((None, tm, d), lambda bi, j: (bi, j, 0)),
            pl.BlockSpec((None, tm, d), lambda bi, j: (bi, j, 0)),
            pl.BlockSpec((None, 2, 8, d), lambda bi, j: (bi, 0, 0, 0)),
            pl.BlockSpec((d, f), lambda bi, j: (0, 0), **resident),
            pl.BlockSpec((d, f), lambda bi, j: (0, 0), **resident),
            pl.BlockSpec((f, d), lambda bi, j: (0, 0), **resident),
        ],
        out_specs=pl.BlockSpec((None, tm, d), lambda bi, j: (bi, j, 0)),
        out_shape=jax.ShapeDtypeStruct((b, s, d), F32),
        compiler_params=_cparams(("parallel", "parallel")),
        name="ffn_dense",
    )(h2, x_all, mods_i, wg.astype(BF16), wu.astype(BF16), wd.astype(BF16))


def _lru_bwd_tile(j, nt, nct):
    return jnp.where(j < nct, nct - 1 - j, nt - 1 - (j - nct))


def _lru_scan_kernel(um_f, up_f, un_f, um_b, up_b, un_b, cw_ref, cb_ref, wax_f, wax_b, ba_ref, bx_ref, lam_ref,
                     hsf_ref, hsb_ref, ext, af_s, bf_s, ab_s, bb_s, carry, *, tt, nt, nct, wl, nb):
    j = pl.program_id(1)

    @pl.when(j == 0)
    def _():
        carry[...] = jnp.zeros_like(carry)

    def coeffs(um, up, un, t, d, wax_ref, a_s, b_s):
        prev_ok = jnp.logical_and(t != 0, t != nct)
        next_ok = jnp.logical_and(t != nct - 1, t != nt - 1)
        ext[:, SUBLANES:SUBLANES + tt, :] = um[...]
        ext[:, 0:SUBLANES, :] = jnp.where(prev_ok, up[...], 0.0)
        ext[:, SUBLANES + tt:2 * SUBLANES + tt, :] = jnp.where(next_ok, un[...], 0.0)
        uc = cb_ref[...].reshape(1, 1, wl)
        for k in range(4):
            uc = uc + cw_ref[k:k + 1, :].reshape(1, 1, wl) * ext[:, SUBLANES - 2 + k:SUBLANES - 2 + k + tt, :]
        sp = jax.nn.softplus(-lam_ref[d:d + 1, :])
        for hb in range(wl // LANES):
            sl = slice(hb * LANES, (hb + 1) * LANES)
            ub = uc[:, :, sl].reshape(nb * tt, LANES)
            pre = jnp.dot(ub.astype(BF16), wax_ref[hb], preferred_element_type=F32)
            r = jax.nn.sigmoid(pre[:, :LANES] + ba_ref[d:d + 1, sl])
            i = jax.nn.sigmoid(pre[:, LANES:] + bx_ref[d:d + 1, sl])
            log_a = (-LRU_C) * r * sp[:, sl]
            a = jnp.exp(log_a)
            b = jnp.sqrt(-jnp.tanh(log_a) * (a * a + 1.0)) * (i * ub)
            a_s[:, :, sl] = a.reshape(nb, tt, LANES)
            b_s[:, :, sl] = b.reshape(nb, tt, LANES)

    tb = _lru_bwd_tile(j, nt, nct)
    coeffs(um_f, up_f, un_f, j, 0, wax_f, af_s, bf_s)
    coeffs(um_b, up_b, un_b, tb, 1, wax_b, ab_s, bb_s)

    def step(t, c):
        hf, hb = c
        hf = af_s[:, pl.ds(t, 1), :].reshape(nb, wl) * hf + bf_s[:, pl.ds(t, 1), :].reshape(nb, wl)
        hsf_ref[:, pl.ds(t, 1), :] = hf.reshape(nb, 1, wl)
        r = tt - 1 - t
        hb = ab_s[:, pl.ds(r, 1), :].reshape(nb, wl) * hb + bb_s[:, pl.ds(r, 1), :].reshape(nb, wl)
        hsb_ref[:, pl.ds(r, 1), :] = hb.reshape(nb, 1, wl)
        return hf, hb

    hf, hb = lax.fori_loop(0, tt, step, (carry[0], carry[1]), unroll=8)
    carry[0] = hf
    carry[1] = hb


def _lru_scan(gu, conv_w, conv_b, w_a, b_a, w_x, b_x, lam, ctx_len):
    nb, s, d2 = gu.shape
    d = d2 // 2
    tt = 256
    assert s % tt == 0 and ctx_len % tt == 0 and d % (RNN_BLOCKS * LANES) == 0 and d // RNN_BLOCKS == LANES
    nt, nct = s // tt, ctx_len // tt
    wl = 2 * LANES
    ncg = d // wl
    coff = d // wl
    r8 = tt // SUBLANES
    last8 = s // SUBLANES - 1
    wax = jnp.concatenate([w_a, w_x], axis=-1).astype(BF16)

    def main_f(cg, j): return (0, j, coff + cg)
    def prev_f(cg, j): return (0, jnp.maximum(j * r8 - 1, 0), coff + cg)
    def next_f(cg, j): return (0, jnp.minimum((j + 1) * r8, last8), coff + cg)
    def tb_(j): return _lru_bwd_tile(j, nt, nct)
    def main_b(cg, j): return (0, tb_(j), coff + cg)
    def prev_b(cg, j): return (0, jnp.maximum(tb_(j) * r8 - 1, 0), coff + cg)
    def next_b(cg, j): return (0, jnp.minimum((tb_(j) + 1) * r8, last8), coff + cg)

    big = lambda im: pl.BlockSpec((nb, tt, wl), im)
    halo = lambda im: pl.BlockSpec((nb, SUBLANES, wl), im)
    vec2 = pl.BlockSpec((2, wl), lambda cg, j: (0, cg))
    return pl.pallas_call(
        functools.partial(_lru_scan_kernel, tt=tt, nt=nt, nct=nct, wl=wl, nb=nb),
        grid=(ncg, nt),
        in_specs=[
            big(main_f), halo(prev_f), halo(next_f), big(main_b), halo(prev_b), halo(next_b),
            pl.BlockSpec((4, wl), lambda cg, j: (0, cg)),
            pl.BlockSpec((1, wl), lambda cg, j: (0, cg)),
            pl.BlockSpec((None, wl // LANES, LANES, 2 * LANES), lambda cg, j: (0, cg, 0, 0)),
            pl.BlockSpec((None, wl // LANES, LANES, 2 * LANES), lambda cg, j: (1, cg, 0, 0)),
            vec2, vec2, vec2,
        ],
        out_specs=[
            pl.BlockSpec((nb, tt, wl), lambda cg, j: (0, j, cg)),
            pl.BlockSpec((nb, tt, wl), lambda cg, j: (0, tb_(j), cg)),
        ],
        out_shape=[jax.ShapeDtypeStruct((nb, s, d), F32), jax.ShapeDtypeStruct((nb, s, d), F32)],
        scratch_shapes=[
            pltpu.VMEM((nb, tt + 2 * SUBLANES, wl), F32),
            pltpu.VMEM((nb, tt, wl), F32), pltpu.VMEM((nb, tt, wl), F32),
            pltpu.VMEM((nb, tt, wl), F32), pltpu.VMEM((nb, tt, wl), F32),
            pltpu.VMEM((2, nb, wl), F32),
        ],
        compiler_params=_cparams(("parallel", "arbitrary")),
        name="lru_scan",
    )(gu, gu, gu, gu, gu, gu, conv_w, conv_b.reshape(1, d), wax, wax, b_a, b_x, lam)


def _rope_tables(seq, ctx_len):
    q = HEAD_DIM // 4
    inv_freq = ROPE_BASE ** (-jnp.arange(q, dtype=F32) / q)
    pos = jnp.arange(seq, dtype=jnp.int32)
    row = (pos // GRID_W).astype(F32)[:, None] * inv_freq
    col = (pos % GRID_W).astype(F32)[:, None] * inv_freq
    ang = jnp.concatenate([row, row, col, col], axis=-1)
    sign = jnp.concatenate([-jnp.ones((q,), F32), jnp.ones((q,), F32)] * 2)
    cos = jnp.concatenate([jnp.ones((ctx_len, HEAD_DIM), F32), jnp.cos(ang)], axis=0)
    sin = jnp.concatenate([jnp.zeros((ctx_len, HEAD_DIM), F32), jnp.sin(ang) * sign], axis=0)
    return jnp.tile(cos, (1, 2)), jnp.tile(sin, (1, 2))


def _qkv_kernel(x_ref, mod_ref, g_ref, w_ref, cos_ref, sin_ref, o_ref, *, tm, ctx_len, n_q, n_rope):
    j = pl.program_id(1)
    x = x_ref[...]
    is_ctx = _row_is_ctx(j, tm, ctx_len, x.shape)
    h = _adanorm(x, g_ref[...], _sel_mod(mod_ref, is_ctx, 0), _sel_mod(mod_ref, is_ctx, 1))
    acc = jnp.dot(h.astype(BF16), w_ref[...], preferred_element_type=F32)
    cos = cos_ref[...]
    sin = sin_ref[...]
    lane = lax.broadcasted_iota(jnp.int32, (tm, LANES), 1)
    first_half = (lane % (HEAD_DIM // 2)) < (HEAD_DIM // 4)
    q_scale = HEAD_DIM ** -0.5
    for c in range(acc.shape[1] // LANES):
        v = acc[:, c * LANES:(c + 1) * LANES]
        if c < n_rope:
            partner = jnp.where(first_half, pltpu.roll(v, LANES - HEAD_DIM // 4, 1), pltpu.roll(v, HEAD_DIM // 4, 1))
            v = v * cos + partner * sin
            if c < n_q:
                v = v * q_scale
        o_ref[:, c * LANES:(c + 1) * LANES] = v.astype(o_ref.dtype)


def _qkv_proj(x_all, mods_i, g, w, cos_t, sin_t, ctx_len):
    b, s, d = x_all.shape
    n = w.shape[1]
    tm = _pick_tile(s, (544, 384, 256, 128))
    n_q = N_HEADS * HEAD_DIM // LANES
    n_rope = (N_HEADS + N_KV_HEADS) * HEAD_DIM // LANES
    return pl.pallas_call(
        functools.partial(_qkv_kernel, tm=tm, ctx_len=ctx_len, n_q=n_q, n_rope=n_rope),
        grid=(b, s // tm),
        in_specs=[
            pl.BlockSpec((None, tm, d), lambda bi, j: (bi, j, 0)),
            pl.BlockSpec((None, 2, 8, d), lambda bi, j: (bi, 0, 0, 0)),
            pl.BlockSpec((1, d), lambda bi, j: (0, 0)),
            pl.BlockSpec((d, n), lambda bi, j: (0, 0)),
            pl.BlockSpec((tm, LANES), lambda bi, j: (j, 0)),
            pl.BlockSpec((tm, LANES), lambda bi, j: (j, 0)),
        ],
        out_specs=pl.BlockSpec((None, tm, n), lambda bi, j: (bi, j, 0)),
        out_shape=jax.ShapeDtypeStruct((b, s, n), BF16),
        compiler_params=_cparams(("parallel", "parallel")),
        name="qkv_proj",
    )(x_all, mods_i, g.reshape(1, d), w.astype(BF16), cos_t, sin_t)


def _attn_kernel(q_ref, kp_ref, km_ref, kn_ref, kc_ref, vp_ref, vm_ref, vn_ref, vc_ref, sink_ref, o_ref,
                 *, tq, nct, seq, ctx_len):
    j = pl.program_id(1)
    k_all = jnp.concatenate([kp_ref[...], km_ref[...], kn_ref[...], kc_ref[...]], axis=0)
    v_all = jnp.concatenate([vp_ref[...], vm_ref[...], vn_ref[...], vc_ref[...]], axis=0)
    n_loc = 2 * tq
    n_keys = n_loc + ctx_len
    row = lax.broadcasted_iota(jnp.int32, (tq, n_keys), 0)
    col = lax.broadcasted_iota(jnp.int32, (tq, n_keys), 1)
    p0 = (j - nct) * tq
    kpos = p0 - tq // 2 + col
    band = jnp.abs(row - (col - tq // 2)) <= WINDOW
    in_seq = jnp.logical_and(kpos >= 0, kpos < seq)
    valid_loc = jnp.logical_and(jnp.logical_and(band, in_seq), j >= nct)
    valid = jnp.logical_or(col >= n_loc, valid_loc)
    group = N_HEADS // N_KV_HEADS
    outs = []
    for h in range(N_HEADS):
        kvh = h // group
        qh = q_ref[:, h * HEAD_DIM:(h + 1) * HEAD_DIM]
        kk = k_all[:, kvh * HEAD_DIM:(kvh + 1) * HEAD_DIM]
        vv = v_all[:, kvh * HEAD_DIM:(kvh + 1) * HEAD_DIM]
        s = lax.dot_general(qh, kk, (((1,), (1,)), ((), ())), preferred_element_type=F32)
        s = jnp.where(valid, s, -1e30)
        sink = sink_ref[h:h + 1, 0:1]
        m = jnp.maximum(jnp.max(s, axis=1, keepdims=True), sink)
        p = jnp.exp(s - m)
        denom = jnp.sum(p, axis=1, keepdims=True) + jnp.exp(sink - m)
        o = jnp.dot(p.astype(BF16), vv, preferred_element_type=F32)
        outs.append((o / denom).astype(o_ref.dtype))
    o_ref[...] = jnp.concatenate(outs, axis=1)


def _attention(qkv, sinks, ctx_len):
    b, s, _ = qkv.shape
    seq = s - ctx_len
    tq = 2 * WINDOW
    half = tq // 2
    assert ctx_len % tq == 0 and seq % tq == 0
    nct = ctx_len // tq
    dq = N_HEADS * HEAD_DIM
    dkv = N_KV_HEADS * HEAD_DIM
    kcol, vcol = dq // dkv, dq // dkv + 1
    last_half = s // half - 1
    sink_b = jnp.broadcast_to(sinks.astype(F32)[:, None], (N_HEADS, LANES))

    def prev(col): return lambda bi, j: (bi, jnp.maximum(2 * j - 1, 0), col)
    def main(col): return lambda bi, j: (bi, j, col)
    def nxt(col): return lambda bi, j: (bi, jnp.minimum(2 * j + 2, last_half), col)
    def ctx(col): return lambda bi, j: (bi, 0, col)
    kv_specs = lambda col: [
        pl.BlockSpec((None, half, dkv), prev(col)), pl.BlockSpec((None, tq, dkv), main(col)),
        pl.BlockSpec((None, half, dkv), nxt(col)), pl.BlockSpec((None, ctx_len, dkv), ctx(col)),
    ]
    return pl.pallas_call(
        functools.partial(_attn_kernel, tq=tq, nct=nct, seq=seq, ctx_len=ctx_len),
        grid=(b, s // tq),
        in_specs=[pl.BlockSpec((None, tq, dq), lambda bi, j: (bi, j, 0))] + kv_specs(kcol) + kv_specs(vcol)
        + [pl.BlockSpec((N_HEADS, LANES), lambda bi, j: (0, 0))],
        out_specs=pl.BlockSpec((None, tq, dq), lambda bi, j: (bi, j, 0)),
        out_shape=jax.ShapeDtypeStruct((b, s, dq), BF16),
        compiler_params=_cparams(("parallel", "parallel")),
        name="swa_attention",
    )(qkv, *([qkv] * 8), sink_b)


def _dwconv3_kernel(um, up, un, cw_ref, cb_ref, g1_ref, g2_ref, v_ref, ext, *, tt, nt, nct, c):
    t = pl.program_id(1)
    prev_ok = jnp.logical_and(t != 0, t != nct)
    next_ok = jnp.logical_and(t != nct - 1, t != nt - 1)
    ext[SUBLANES:SUBLANES + tt, :] = um[...]
    ext[0:SUBLANES, :] = jnp.where(prev_ok, up[...], 0.0)
    ext[SUBLANES + tt:2 * SUBLANES + tt, :] = jnp.where(next_ok, un[...], 0.0)
    y = cb_ref[...]
    for k in range(3):
        y = y + cw_ref[k:k + 1, :] * ext[SUBLANES - 1 + k:SUBLANES - 1 + k + tt, :]
    g1_ref[...] = y[:, 0:c]
    g2_ref[...] = y[:, c:2 * c]
    v_ref[...] = y[:, 2 * c:3 * c]


def _dwconv3(u0, conv_w, conv_b, ctx_len):
    b, s, c3 = u0.shape
    c = c3 // 3
    tt = 256
    assert s % tt == 0 and ctx_len % tt == 0
    nt, nct = s // tt, ctx_len // tt
    r8 = tt // SUBLANES
    last8 = s // SUBLANES - 1
    out = jax.ShapeDtypeStruct((b, s, c), F32)
    ospec = pl.BlockSpec((None, tt, c), lambda bi, t: (bi, t, 0))
    return pl.pallas_call(
        functools.partial(_dwconv3_kernel, tt=tt, nt=nt, nct=nct, c=c),
        grid=(b, nt),
        in_specs=[
            pl.BlockSpec((None, tt, c3), lambda bi, t: (bi, t, 0)),
            pl.BlockSpec((None, SUBLANES, c3), lambda bi, t: (bi, jnp.maximum(t * r8 - 1, 0), 0)),
            pl.BlockSpec((None, SUBLANES, c3), lambda bi, t: (bi, jnp.minimum((t + 1) * r8, last8), 0)),
            pl.BlockSpec((3, c3), lambda bi, t: (0, 0)),
            pl.BlockSpec((1, c3), lambda bi, t: (0, 0)),
        ],
        out_specs=[ospec, ospec, ospec],
        out_shape=[out, out, out],
        scratch_shapes=[pltpu.VMEM((tt + 2 * SUBLANES, c3), F32)],
        compiler_params=_cparams(("parallel", "parallel")),
        name="hyena_dwconv",
    )(u0, u0, u0, conv_w, conv_b.reshape(1, c3))


def _filter_features(seq, ctx_len):
    def feats(length):
        t = jnp.linspace(0.0, 1.0, length, dtype=F32)[:, None]
        omega = (2.0 * math.pi / length) * jnp.arange(length, dtype=F32)[:, None]
        bands = jnp.linspace(1e-4, FILTER_BANDS - 1, FILTER_BANDS, dtype=F32)[None, :]
        return jnp.concatenate([t, jnp.cos(bands * omega), -jnp.sin(bands * omega)], axis=-1)
    z = jnp.concatenate([feats(ctx_len), feats(seq)], axis=0)
    return jnp.pad(z, ((0, 0), (0, 40 - z.shape[1])))


def _filter_kernel(z_ref, w1_ref, b1_ref, w2_ref, b2_ref, w3_ref, b3_ref, fr_ref, w4_ref, dl_ref, hw_ref, sum_ref,
                   *, tm, ctx_len, c):
    j = pl.program_id(0)

    @pl.when(j == 0)
    def _():
        sum_ref[...] = jnp.zeros_like(sum_ref)

    z = z_ref[...]
    fr = fr_ref[...]
    dot = functools.partial(jnp.dot, preferred_element_type=F32, precision=HIGHEST)
    hid = jnp.sin(fr * (dot(z, w1_ref[...]) + b1_ref[...]))
    hid = jnp.sin(fr * (dot(hid, w2_ref[...]) + b2_ref[...]))
    hid = jnp.sin(fr * (dot(hid, w3_ref[...]) + b3_ref[...]))
    h = dot(hid, w4_ref[...])
    window = jnp.exp(-z[:, 0:1] * jnp.abs(dl_ref[...]))
    rows = j * tm + lax.broadcasted_iota(jnp.int32, (tm, c), 0)
    first = jnp.logical_or(rows == 0, rows == ctx_len)
    is_ctx = j * tm < ctx_len
    srow = jnp.where(is_ctx, 0, 1)
    for q in range(4):
        hq = h[:, q * c:(q + 1) * c] * window
        if q % 2 == 1:
            hq = jnp.where(first, 0.0, hq)
        hw_ref[:, q * c:(q + 1) * c] = hq
        part = jnp.sum(jnp.abs(hq), axis=0, keepdims=True)
        for r in range(2):
            sum_ref[r:r + 1, q * c:(q + 1) * c] += jnp.where(srow == r, part, 0.0)


def _hyena_filters(seq, ctx_len, w1, b1, w2, b2, w3, b3, freq, w4, c):
    s = seq + ctx_len
    z = _filter_features(seq, ctx_len)
    tm = 256
    assert ctx_len % tm == 0 and seq % tm == 0
    fw = w1.shape[1]
    w1p = jnp.pad(w1.astype(F32), ((0, 40 - w1.shape[0]), (0, 0)))
    deltas = jnp.linspace(math.log(DECAY_TARGET) / SLOW_DECAY_PCT, math.log(DECAY_TARGET) / FAST_DECAY_PCT, c,
                          dtype=F32)[None, :]
    full = lambda shape: pl.BlockSpec(shape, lambda j: tuple(0 for _ in shape))
    return pl.pallas_call(
        functools.partial(_filter_kernel, tm=tm, ctx_len=ctx_len, c=c),
        grid=(s // tm,),
        in_specs=[
            pl.BlockSpec((tm, 40), lambda j: (j, 0)),
            full((40, fw)), full((1, fw)), full((fw, fw)), full((1, fw)), full((fw, fw)), full((1, fw)),
            full((1, fw)), full((fw, 4 * c)), full((1, c)),
        ],
        out_specs=[pl.BlockSpec((tm, 4 * c), lambda j: (j, 0)), pl.BlockSpec((SUBLANES, 4 * c), lambda j: (0, 0))],
        out_shape=[jax.ShapeDtypeStruct((s, 4 * c), F32), jax.ShapeDtypeStruct((SUBLANES, 4 * c), F32)],
        compiler_params=_cparams(("arbitrary",)),
        name="hyena_filter_mlp",
    )(z, w1p, b1.reshape(1, fw), w2, b2.reshape(1, fw), w3, b3.reshape(1, fw), freq.reshape(1, fw), w4, deltas)


def _fft_tables(seq, ctx_len):
    n1 = FFT_N1
    rc, rl = ctx_len // n1, seq // n1
    n2c, n2l = 2 * rc, 2 * rl
    k2n = n2l + n2c
    r = rc + rl

    def cis(num, den):
        ang = (-2.0 * math.pi / den) * (num % den).astype(F32)
        return jnp.cos(ang), jnp.sin(ang)

    ar = jnp.arange
    fl_r, fl_i = cis(ar(n2l)[:, None] * ar(rl)[None, :], n2l)
    fc_r, fc_i = cis(ar(n2c)[:, None] * ar(rc)[None, :], n2c)
    zl, zc = jnp.zeros((n2l, rc), F32), jnp.zeros((n2c, rl), F32)
    f_r = jnp.concatenate([jnp.concatenate([zl, fl_r], 1), jnp.concatenate([fc_r, zc], 1)], 0)
    f_i = jnp.concatenate([jnp.concatenate([zl, fl_i], 1), jnp.concatenate([fc_i, zc], 1)], 0)
    scale = jnp.concatenate([jnp.full((n2l,), 1.0 / (n1 * n2l), F32), jnp.full((n2c,), 1.0 / (n1 * n2c), F32)])
    g_r, g_i = (f_r * scale[:, None]).T, (f_i * scale[:, None]).T
    tl_r, tl_i = cis(ar(n2l)[:, None] * ar(n1)[None, :], n1 * n2l)
    tc_r, tc_i = cis(ar(n2c)[:, None] * ar(n1)[None, :], n1 * n2c)
    tw_r, tw_i = jnp.concatenate([tl_r, tc_r], 0), jnp.concatenate([tl_i, tc_i], 0)
    bl = lambda a: jnp.broadcast_to(a[:, :, None], a.shape + (LANES,))
    f1_r, f1_i = cis(ar(n1)[:, None] * ar(n1)[None, :], n1)
    return dict(f_r=f_r, f_i=f_i, g_r=g_r, g_i=g_i, f1_r=f1_r, f1_i=f1_i,
                ta_r=bl(tw_r.T), ta_i=bl(tw_i.T),
                tb_r=bl(tw_r), tb_i=bl(tw_i),
                k2n=k2n, n2l=n2l, r=r)


def _hdot(a, b):
    return jnp.dot(a, b, preferred_element_type=F32, precision=HIGHEST)


FFT_NB1 = 8
FFT_CC = 1024


def _rows(ref, lead, n):
    blk = ref[lead + (slice(None), slice(n, n + 1), slice(None))]
    return blk.reshape(blk.shape[0], blk.shape[2])


def _fft_a_kernel(x_ref, fr_ref, fi_ref, tr_ref, ti_ref, o_ref, *, packed, cc):
    fr, fi = fr_ref[...], fi_ref[...]
    k2n = fr.shape[0]
    for n in range(FFT_NB1):
        if packed:
            xr, xi = _rows(x_ref, (0,), n), _rows(x_ref, (1,), n)
            a_r = _hdot(fr, xr) - _hdot(fi, xi)
            a_i = _hdot(fi, xr) + _hdot(fr, xi)
        else:
            x = _rows(x_ref, (), n)
            a_r, a_i = _hdot(fr, x), _hdot(fi, x)
        tr, ti = tr_ref[n], ti_ref[n]
        o_r, o_i = [], []
        for l in range(cc // LANES):
            cr, ci = a_r[:, l * LANES:(l + 1) * LANES], a_i[:, l * LANES:(l + 1) * LANES]
            o_r.append(cr * tr - ci * ti)
            o_i.append(cr * ti + ci * tr)
        o_ref[0, :, n:n + 1, :] = jnp.concatenate(o_r, axis=1).reshape(k2n, 1, cc)
        o_ref[1, :, n:n + 1, :] = jnp.concatenate(o_i, axis=1).reshape(k2n, 1, cc)


def _fft_stage_a(xv, tb, packed, c):
    k2n, r = tb["k2n"], tb["r"]
    cc = min(c, FFT_CC)
    nnb, ncc = FFT_N1 // FFT_NB1, c // cc
    mat = pl.BlockSpec((k2n, r), lambda *a: (0, 0))
    if packed:
        p = xv.shape[1]
        grid = (p, nnb, ncc)
        x_spec = pl.BlockSpec((2, None, r, FFT_NB1, cc), lambda pi, nb, cb: (0, pi, 0, nb, cb))
        t_spec = pl.BlockSpec((FFT_NB1, k2n, LANES), lambda pi, nb, cb: (nb, 0, 0))
        o_spec = pl.BlockSpec((None, 2, k2n, FFT_NB1, cc), lambda pi, nb, cb: (pi, 0, 0, nb, cb))
        o_shape = jax.ShapeDtypeStruct((p, 2, k2n, FFT_N1, c), F32)
        sem = ("parallel", "parallel", "parallel")
    else:
        grid = (nnb, ncc)
        x_spec = pl.BlockSpec((r, FFT_NB1, cc), lambda nb, cb: (0, nb, cb))
        t_spec = pl.BlockSpec((FFT_NB1, k2n, LANES), lambda nb, cb: (nb, 0, 0))
        o_spec = pl.BlockSpec((2, k2n, FFT_NB1, cc), lambda nb, cb: (0, 0, nb, cb))
        o_shape = jax.ShapeDtypeStruct((2, k2n, FFT_N1, c), F32)
        sem = ("parallel", "parallel")
    return pl.pallas_call(
        functools.partial(_fft_a_kernel, packed=packed, cc=cc),
        grid=grid,
        in_specs=[x_spec, mat, mat, t_spec, t_spec],
        out_specs=o_spec,
        out_shape=o_shape,
        compiler_params=_cparams(sem),
        name="hyena_fft_a",
    )(xv, tb["f_r"], tb["f_i"], tb["ta_r"], tb["ta_i"])


def _fft_bf_kernel(a_ref, f1r_ref, f1i_ref, o_ref):
    ar_, ai_ = a_ref[0], a_ref[1]
    f1r, f1i = f1r_ref[...], f1i_ref[...]
    o_ref[0] = _hdot(f1r, ar_) - _hdot(f1i, ai_)
    o_ref[1] = _hdot(f1i, ar_) + _hdot(f1r, ai_)


def _fft_stage_b_filter(af, tb, c4):
    k2n = tb["k2n"]
    cc = min(c4, FFT_CC)
    mat = pl.BlockSpec((FFT_N1, FFT_N1), lambda k, cb: (0, 0))
    blk = pl.BlockSpec((2, None, FFT_N1, cc), lambda k, cb: (0, k, 0, cb))
    return pl.pallas_call(
        _fft_bf_kernel,
        grid=(k2n, c4 // cc),
        in_specs=[blk, mat, mat],
        out_specs=blk,
        out_shape=jax.ShapeDtypeStruct(af.shape, F32),
        compiler_params=_cparams(("parallel", "parallel")),
        name="hyena_fft_b_filter",
    )(af, tb["f1_r"], tb["f1_i"])


def _fft_mid_kernel(a_ref, h0_ref, h1_ref, s0_ref, s1_ref, f1r_ref, f1i_ref, tr_ref, ti_ref, o_ref, *, n2l, cc):
    k2 = pl.program_id(1)
    f1r, f1i = f1r_ref[...], f1i_ref[...]
    ar_, ai_ = a_ref[0], a_ref[1]
    xr = _hdot(f1r, ar_) - _hdot(f1i, ai_)
    xi = _hdot(f1i, ar_) + _hdot(f1r, ai_)
    norm = s0_ref[...] + s1_ref[...]
    inv = 1.0 / jnp.where(k2 >= n2l, norm[0:1, :], norm[1:2, :])
    kr = (h0_ref[0] + h1_ref[0]) * inv
    ki = (h0_ref[1] - h1_ref[1]) * inv
    yr = xr * kr - xi * ki
    yi = xr * ki + xi * kr
    br = _hdot(f1r, yr) + _hdot(f1i, yi)
    bi = _hdot(f1r, yi) - _hdot(f1i, yr)
    tr, ti = tr_ref[...], ti_ref[...]
    for l in range(cc // LANES):
        sl = slice(l * LANES, (l + 1) * LANES)
        cr, ci = br[:, sl], bi[:, sl]
        o_ref[0, :, sl] = cr * tr + ci * ti
        o_ref[1, :, sl] = ci * tr - cr * ti


def _fft_mid(a, hspec, sums, tb, order, c):
    p = a.shape[0]
    k2n, n2l = tb["k2n"], tb["n2l"]
    cc = min(c, FFT_CC)
    ncc = c // cc
    mat = pl.BlockSpec((FFT_N1, FFT_N1), lambda pi, k, cb: (0, 0))
    blk = pl.BlockSpec((None, 2, None, FFT_N1, cc), lambda pi, k, cb: (pi, 0, k, 0, cb))
    hs = lambda side: pl.BlockSpec((2, None, FFT_N1, cc), lambda pi, k, cb: (0, k, 0, (2 * order + side) * ncc + cb))
    ss = lambda side: pl.BlockSpec((SUBLANES, cc), lambda pi, k, cb: (0, (2 * order + side) * ncc + cb))
    tw = pl.BlockSpec((None, FFT_N1, LANES), lambda pi, k, cb: (k, 0, 0))
    return pl.pallas_call(
        functools.partial(_fft_mid_kernel, n2l=n2l, cc=cc),
        grid=(p, k2n, ncc),
        in_specs=[blk, hs(0), hs(1), ss(0), ss(1), mat, mat, tw, tw],
        out_specs=blk,
        out_shape=jax.ShapeDtypeStruct(a.shape, F32),
        compiler_params=_cparams(("parallel", "parallel", "parallel")),
        name="hyena_fft_mid",
    )(a, hspec, hspec, sums, sums, tb["f1_r"], tb["f1_i"], tb["tb_r"], tb["tb_i"])


def _fft_inv_kernel(b_ref, gr_ref, gi_ref, g_ref, v_ref, skip_ref, o_ref, *, cc):
    gr, gi = gr_ref[...], gi_ref[...]
    r = gr.shape[0]
    skip = skip_ref[...]
    for n in range(FFT_NB1):
        br, bi = _rows(b_ref, (0,), n), _rows(b_ref, (1,), n)
        yr = _hdot(gr, br) + _hdot(gi, bi)
        yi = _hdot(gr, bi) - _hdot(gi, br)
        for half, y in ((0, yr), (1, yi)):
            z = _rows(g_ref, (half,), n) * (y + skip * _rows(v_ref, (half,), n))
            o_ref[half, :, n:n + 1, :] = z.reshape(r, 1, cc).astype(o_ref.dtype)


def _fft_inverse_gate(bv, gv, vv, skip, tb, c):
    p = bv.shape[0]
    k2n, r = tb["k2n"], tb["r"]
    cc = min(c, FFT_CC)
    mat = pl.BlockSpec((r, k2n), lambda pi, nb, cb: (0, 0))
    xs = pl.BlockSpec((2, None, r, FFT_NB1, cc), lambda pi, nb, cb: (0, pi, 0, nb, cb))
    return pl.pallas_call(
        functools.partial(_fft_inv_kernel, cc=cc),
        grid=(p, FFT_N1 // FFT_NB1, c // cc),
        in_specs=[pl.BlockSpec((None, 2, k2n, FFT_NB1, cc), lambda pi, nb, cb: (pi, 0, 0, nb, cb)), mat, mat, xs, xs,
                  pl.BlockSpec((1, cc), lambda pi, nb, cb: (0, cb))],
        out_specs=xs,
        out_shape=jax.ShapeDtypeStruct(gv.shape, F32),
        compiler_params=_cparams(("parallel", "parallel", "parallel")),
        name="hyena_fft_inv",
    )(bv, tb["g_r"], tb["g_i"], gv, vv, skip.reshape(1, c))


def _hyena_mix(u0, conv_w, conv_b, fw1, fb1, fw2, fb2, fw3, fb3, ffreq, fw4, skip, ctx_len):
    b, s, c3 = u0.shape
    c = c3 // 3
    seq = s - ctx_len
    assert b % 2 == 0 and seq % FFT_N1 == 0 and ctx_len % FFT_N1 == 0
    p = b // 2
    r = s // FFT_N1
    g1, g2, v = _dwconv3(u0, conv_w, conv_b, ctx_len)
    hw, sums = _hyena_filters(seq, ctx_len, fw1, fb1, fw2, fb2, fw3, fb3, ffreq, fw4, c)
    tb = _fft_tables(seq, ctx_len)
    af = _fft_stage_a(hw.reshape(r, FFT_N1, 4 * c), tb, False, 4 * c)
    hspec = _fft_stage_b_filter(af, tb, 4 * c)
    view = lambda t: t.reshape(2, p, r, FFT_N1, c)
    z = v
    for order, gate in ((0, g1), (1, g2)):
        a = _fft_stage_a(view(z), tb, True, c)
        bm = _fft_mid(a, hspec, sums, tb, order, c)
        z = _fft_inverse_gate(bm, view(gate), view(z), skip[order], tb, c).reshape(b, s, c)
    return z


def _router_kernel(h_ref, r_ref, g_ref, hb_ref, *, ne):
    hb_ref[...] = h_ref[...].astype(BF16)
    logits = jnp.dot(h_ref[...], r_ref[...], preferred_element_type=F32, precision=HIGHEST)
    lane = lax.broadcasted_iota(jnp.int32, logits.shape, 1)
    neg = -jnp.inf
    logits = jnp.where(lane < ne, logits, neg)
    m1 = jnp.max(logits, axis=1, keepdims=True)
    i1 = jnp.min(jnp.where(logits == m1, lane, LANES), axis=1, keepdims=True)
    rest = jnp.where(lane == i1, neg, logits)
    m2 = jnp.max(rest, axis=1, keepdims=True)
    i2 = jnp.min(jnp.where(rest == m2, lane, LANES), axis=1, keepdims=True)
    e2 = jnp.exp(m2 - m1)
    w1 = 1.0 / (1.0 + e2)
    g_ref[...] = jnp.where(lane == i1, w1, 0.0) + jnp.where(lane == i2, e2 * w1, 0.0)


def _router(h2, router):
    b, s, d = h2.shape
    ne = router.shape[1]
    tm = _pick_tile(s, (544, 384, 256, 128))
    rp = jnp.pad(router.astype(F32), ((0, 0), (0, LANES - ne)))
    return pl.pallas_call(
        functools.partial(_router_kernel, ne=ne),
        grid=(b, s // tm),
        in_specs=[pl.BlockSpec((None, tm, d), lambda bi, j: (bi, j, 0)), pl.BlockSpec((d, LANES), lambda bi, j: (0, 0))],
        out_specs=[pl.BlockSpec((None, tm, LANES), lambda bi, j: (bi, j, 0)),
                   pl.BlockSpec((None, tm, d), lambda bi, j: (bi, j, 0))],
        out_shape=[jax.ShapeDtypeStruct((b, s, LANES), F32), jax.ShapeDtypeStruct((b, s, d), BF16)],
        compiler_params=_cparams(("parallel", "parallel")),
        name="moe_router",
    )(h2, rp)


def _moe_sparse_kernel(h_ref, x_ref, mod_ref, gate_ref, wg_ref, wu_ref, wd_ref, o_ref, dest, cnt, xe, ye,
                       *, tb, max_tiles, rb, ctx_len, ne, nf):
    j, e, f = pl.program_id(1), pl.program_id(2), pl.program_id(3)
    half = tb // 2

    @pl.when(jnp.logical_and(e == 0, f == 0))
    def _():
        tri = (lax.broadcasted_iota(jnp.int32, (rb, rb), 0) > lax.broadcasted_iota(jnp.int32, (rb, rb), 1))
        tri = jnp.where(tri, 1.0, 0.0).astype(BF16)
        carry = jnp.zeros((1, LANES), F32)
        for k in range(tb // rb):
            routed = gate_ref[k * rb:(k + 1) * rb, :] > 0.0
            r01 = jnp.where(routed, 1.0, 0.0)
            before = jnp.dot(tri, r01.astype(BF16), preferred_element_type=F32) + carry
            dest[k * rb:(k + 1) * rb, :] = jnp.where(routed, before, -1.0)
            carry = carry + jnp.sum(r01, axis=0, keepdims=True)
        cnt[...] = jnp.broadcast_to(carry, cnt.shape)
        o_ref[...] = jnp.zeros_like(o_ref)

    lane1 = lax.broadcasted_iota(jnp.int32, (1, LANES), 1)
    n_e = jnp.sum(jnp.where(lane1 == e, cnt[0:1, :], 0.0))
    n_tiles = jnp.ceil(n_e * (1.0 / LANES)).astype(jnp.int32)

    lane_h = lax.broadcasted_iota(jnp.int32, (half, LANES), 1)

    def expert_col(ref, rows):
        return jnp.sum(jnp.where(lane_h == e, ref[rows, :], 0.0), axis=1, keepdims=True)

    def one_hot(rows, width):
        dcol = expert_col(dest, rows)
        lanef = lane_h.astype(F32)
        pieces = [jnp.where(dcol == lanef + float(k * LANES), 1.0, 0.0).astype(BF16) for k in range(width // LANES)]
        return jnp.concatenate(pieces, axis=1)

    for tiles in range(1, max_tiles + 1):
        m = tiles * LANES

        @pl.when(jnp.logical_and(n_tiles == tiles, f == 0))
        def _():
            acc = None
            for hb in range(2):
                rows = slice(hb * half, (hb + 1) * half)
                part = lax.dot_general(one_hot(rows, m), h_ref[rows, :], (((0,), (0,)), ((), ())),
                                       preferred_element_type=F32)
                acc = part if acc is None else acc + part
            xe[0:m, :] = acc.astype(BF16)
            ye[0:m, :] = jnp.zeros((m, ye.shape[1]), F32)

        @pl.when(n_tiles == tiles)
        def _():
            x = xe[0:m, :]
            a = jnp.dot(x, wg_ref[...], preferred_element_type=F32)
            u = jnp.dot(x, wu_ref[...], preferred_element_type=F32)
            act = (a * jax.nn.sigmoid(a) * u).astype(BF16)
            ye[0:m, :] += jnp.dot(act, wd_ref[...], preferred_element_type=F32)

        @pl.when(jnp.logical_and(n_tiles == tiles, f == nf - 1))
        def _():
            y = ye[0:m, :].astype(BF16)
            for hb in range(2):
                rows = slice(hb * half, (hb + 1) * half)
                back = jnp.dot(one_hot(rows, m), y, preferred_element_type=F32)
                o_ref[rows, :] += expert_col(gate_ref, rows) * back

    @pl.when(jnp.logical_and(e == ne - 1, f == nf - 1))
    def _():
        x = x_ref[...]
        is_ctx = _row_is_ctx(j, tb, ctx_len, x.shape)
        o_ref[...] = x + _sel_mod(mod_ref, is_ctx, 5) * o_ref[...]


def _moe_sparse(hb16, x_all, mods_i, gate, wg, wu, wd, ctx_len):
    b, s, d = x_all.shape
    ne, _, f = wg.shape
    tb = _pick_tile(s, (1088, 768, 512, 256))
    max_tiles = -(-tb // LANES)
    ch = max_tiles * LANES
    rb = max(r for r in range(SUBLANES, 257, SUBLANES) if tb % r == 0)
    fc = _pick_tile(f, (512, 256, 128))
    nf = f // fc
    row = lambda bi, j, e, k: (bi, j, 0)
    return pl.pallas_call(
        functools.partial(_moe_sparse_kernel, tb=tb, max_tiles=max_tiles, rb=rb, ctx_len=ctx_len, ne=ne, nf=nf),
        grid=(b, s // tb, ne, nf),
        in_specs=[
            pl.BlockSpec((None, tb, d), row),
            pl.BlockSpec((None, tb, d), row, pipeline_mode=pl.Buffered(1)),
            pl.BlockSpec((None, 2, 8, d), lambda bi, j, e, k: (bi, 0, 0, 0)),
            pl.BlockSpec((None, tb, LANES), row),
            pl.BlockSpec((None, d, fc), lambda bi, j, e, k: (e, 0, k)),
            pl.BlockSpec((None, d, fc), lambda bi, j, e, k: (e, 0, k)),
            pl.BlockSpec((None, fc, d), lambda bi, j, e, k: (e, k, 0)),
        ],
        out_specs=pl.BlockSpec((None, tb, d), row),
        out_shape=jax.ShapeDtypeStruct((b, s, d), F32),
        scratch_shapes=[pltpu.VMEM((tb, LANES), F32), pltpu.VMEM((SUBLANES, LANES), F32),
                        pltpu.VMEM((ch, d), BF16), pltpu.VMEM((ch, d), F32)],
        compiler_params=_cparams(("parallel", "parallel", "arbitrary", "arbitrary")),
        name="moe_experts",
    )(hb16, x_all, mods_i, gate, wg.astype(BF16), wu.astype(BF16), wd.astype(BF16))


def _final_norm_kernel(x_ref, g_ref, o_ref):
    x = x_ref[...]
    ms = jnp.mean(x * x, axis=-1, keepdims=True)
    o_ref[...] = (x * lax.rsqrt(ms + EPS)) * g_ref[...]


def _final_norm(x_all, g, ctx_len):
    b, s, d = x_all.shape
    seq = s - ctx_len
    tm = _pick_tile(math.gcd(seq, ctx_len), (512, 256, 128))
    off = ctx_len // tm
    return pl.pallas_call(
        _final_norm_kernel,
        grid=(b, seq // tm),
        in_specs=[pl.BlockSpec((None, tm, d), lambda bi, j: (bi, j + off, 0)), pl.BlockSpec((1, d), lambda bi, j: (0, 0))],
        out_specs=pl.BlockSpec((None, tm, d), lambda bi, j: (bi, j, 0)),
        out_shape=jax.ShapeDtypeStruct((b, seq, d), F32),
        compiler_params=_cparams(("parallel", "parallel")),
        name="final_norm",
    )(x_all, g.reshape(1, d))


def kernel(x, c, ctx, c_ctx, ada_w, ada_b, norm_mix, norm_ffn, norm_final,
           lru_w_in, lru_conv_w, lru_conv_b, lru_w_a, lru_b_a, lru_w_x, lru_b_x, lru_lambda, lru_w_out,
           attn_w_qkv, attn_sinks, attn_w_o,
           hy_w_in, hy_b_in, hy_conv_w, hy_conv_b, hy_f_w1, hy_f_b1, hy_f_w2, hy_f_b2, hy_f_w3, hy_f_b3,
           hy_f_freq, hy_f_w4, hy_skip, hy_w_out, hy_b_out,
           ffn_w_gate, ffn_w_up, ffn_w_down,
           moe_router, moe_w_gate, moe_w_up, moe_w_down):
    depth = ada_w.shape[0]
    ctx_len, seq, d = ctx.shape[1], x.shape[1], x.shape[2]
    x_all = jnp.concatenate([ctx, x], axis=1)
    mods = _mods(c, c_ctx, ada_w, ada_b)
    zero_d = jnp.zeros((d,), F32)
    for i in range(depth):
        j = i // N_MIXERS
        moe = i % 2 == 1
        h_dtype = F32 if moe else BF16
        if i % N_MIXERS == 0:
            gu = _norm_mm(x_all, mods[i], norm_mix[i], lru_w_in[j], jnp.zeros((lru_w_in.shape[2],), F32), ctx_len, F32)
            hs_f, hs_b = _lru_scan(gu, lru_conv_w[j], lru_conv_b[j], lru_w_a[j], lru_b_a[j], lru_w_x[j], lru_b_x[j],
                                   lru_lambda[j], ctx_len)
            x_all, h2 = _lru_mm_res(hs_f, hs_b, gu, x_all, mods[i], norm_ffn[i], lru_w_out[j], ctx_len, h_dtype)
        elif i % N_MIXERS == 1:
            cos_t, sin_t = _rope_tables(seq, ctx_len)
            qkv = _qkv_proj(x_all, mods[i], norm_mix[i], attn_w_qkv[j], cos_t, sin_t, ctx_len)
            o = _attention(qkv, attn_sinks[j], ctx_len)
            x_all, h2 = _mm_res(o, x_all, mods[i], norm_ffn[i], attn_w_o[j], zero_d, ctx_len, h_dtype)
        else:
            u0 = _norm_mm(x_all, mods[i], norm_mix[i], hy_w_in[j], hy_b_in[j], ctx_len, F32)
            z = _hyena_mix(u0, hy_conv_w[j], hy_conv_b[j], hy_f_w1[j], hy_f_b1[j], hy_f_w2[j], hy_f_b2[j],
                           hy_f_w3[j], hy_f_b3[j], hy_f_freq[j], hy_f_w4[j], hy_skip[j], ctx_len)
            x_all, h2 = _mm_res(z, x_all, mods[i], norm_ffn[i], hy_w_out[j], hy_b_out[j], ctx_len, h_dtype)
        if moe:
            gate, hb16 = _router(h2, moe_router[i // 2])
            x_all = _moe_sparse(hb16, x_all, mods[i], gate, moe_w_gate[i // 2], moe_w_up[i // 2], moe_w_down[i // 2],
                                ctx_len)
        else:
            x_all = _ffn_dense(h2, x_all, mods[i], ffn_w_gate[i // 2], ffn_w_up[i // 2], ffn_w_down[i // 2], ctx_len)
    return _final_norm(x_all, norm_final, ctx_len)
```

```python
import functools
import math

import jax
import jax.numpy as jnp
import numpy as np
from jax import lax
from jax.experimental import pallas as pl
from jax.experimental.pallas import tpu as pltpu

F32 = jnp.float32
BF16 = jnp.bfloat16
HIGHEST = lax.Precision.HIGHEST

N_MIXERS = 3
EPS = 1e-6
GRID_W = 64
RNN_BLOCKS = 8
LRU_C = 8.0
N_HEADS = 16
N_KV_HEADS = 4
HEAD_DIM = 64
WINDOW = 128
ROPE_BASE = 10000.0
FILTER_BANDS = 16
DECAY_TARGET = 1e-2
FAST_DECAY_PCT = 0.3
SLOW_DECAY_PCT = 1.5
N_EXPERTS = 8
TOP_K = 2

LANES = 128
SUBLANES = 8
VMEM_LIMIT = 56 * 1024 * 1024
FFT_N1 = 128


def _cparams(sem, vmem=VMEM_LIMIT):
    return pltpu.CompilerParams(dimension_semantics=sem, vmem_limit_bytes=vmem)


def _pick_tile(n, cands):
    for c in cands:
        if n % c == 0:
            return c
    raise ValueError(f"no tile for {n} in {cands}")


def _row_is_ctx(j, tm, ctx_len, shape):
    rows = j * tm + lax.broadcasted_iota(jnp.int32, shape, 0)
    return rows < ctx_len


def _sel_mod(mod_ref, is_ctx, idx):
    return jnp.where(is_ctx, mod_ref[0, idx:idx + 1, :], mod_ref[1, idx:idx + 1, :])


def _adanorm(x, g, shift, scale):
    ms = jnp.mean(x * x, axis=-1, keepdims=True)
    y = x * lax.rsqrt(ms + EPS)
    return (y * g) * (1.0 + scale) + shift


def _mods(c, c_ctx, ada_w, ada_b):
    depth, d, d6 = ada_w.shape
    b = c.shape[0]
    rows = ((b + 1 + SUBLANES - 1) // SUBLANES) * SUBLANES
    cond = jnp.concatenate([c, c_ctx[None, :], jnp.zeros((rows - b - 1, d), F32)], axis=0)
    tn = _pick_tile(d6, (1024, 512, 256, 128))

    def kern(c_ref, w_ref, b_ref, o_ref):
        x = c_ref[...]
        s = x * jax.nn.sigmoid(x)
        o_ref[...] = jnp.dot(s, w_ref[...], preferred_element_type=F32, precision=HIGHEST) + b_ref[...]

    out = pl.pallas_call(
        kern,
        grid=(depth, d6 // tn),
        in_specs=[
            pl.BlockSpec((rows, d), lambda i, n: (0, 0)),
            pl.BlockSpec((None, d, tn), lambda i, n: (i, 0, n)),
            pl.BlockSpec((None, 1, tn), lambda i, n: (i, 0, n)),
        ],
        out_specs=pl.BlockSpec((None, rows, tn), lambda i, n: (i, 0, n)),
        out_shape=jax.ShapeDtypeStruct((depth, rows, d6), F32),
        compiler_params=_cparams(("parallel", "parallel")),
        name="mods",
    )(cond, ada_w, ada_b.reshape(depth, 1, d6))
    m = out.reshape(depth, rows, 6, d)
    lat = m[:, :b]
    ctx = jnp.broadcast_to(m[:, b:b + 1], lat.shape)
    both = jnp.stack([ctx, lat], axis=2)
    return jnp.pad(both, ((0, 0), (0, 0), (0, 0), (0, 2), (0, 0)))


def _norm_mm_kernel(x_ref, mod_ref, g_ref, w_ref, b_ref, o_ref, *, tm, ctx_len):
    j = pl.program_id(1)
    x = x_ref[...]
    is_ctx = _row_is_ctx(j, tm, ctx_len, x.shape)
    h = _adanorm(x, g_ref[...], _sel_mod(mod_ref, is_ctx, 0), _sel_mod(mod_ref, is_ctx, 1))
    acc = jnp.dot(h.astype(BF16), w_ref[...], preferred_element_type=F32) + b_ref[...]
    o_ref[...] = acc.astype(o_ref.dtype)


def _norm_mm(x_all, mods_i, g, w, bias, ctx_len, out_dtype):
    b, s, d = x_all.shape
    n = w.shape[1]
    tm = _pick_tile(s, (544, 384, 256, 128))
    return pl.pallas_call(
        functools.partial(_norm_mm_kernel, tm=tm, ctx_len=ctx_len),
        grid=(b, s // tm),
        in_specs=[
            pl.BlockSpec((None, tm, d), lambda bi, j: (bi, j, 0)),
            pl.BlockSpec((None, 2, 8, d), lambda bi, j: (bi, 0, 0, 0)),
            pl.BlockSpec((1, d), lambda bi, j: (0, 0)),
            pl.BlockSpec((d, n), lambda bi, j: (0, 0)),
            pl.BlockSpec((1, n), lambda bi, j: (0, 0)),
        ],
        out_specs=pl.BlockSpec((None, tm, n), lambda bi, j: (bi, j, 0)),
        out_shape=jax.ShapeDtypeStruct((b, s, n), out_dtype),
        compiler_params=_cparams(("parallel", "parallel")),
        name="norm_mm",
    )(x_all, mods_i, g.reshape(1, d), w.astype(BF16), bias.reshape(1, n).astype(F32))


def _mm_res_body(y_bf16, x_ref, mod_ref, g_ref, w_ref, b_ref, xo_ref, ho_ref, tm, ctx_len):
    j = pl.program_id(1)
    x = x_ref[...]
    is_ctx = _row_is_ctx(j, tm, ctx_len, x.shape)
    acc = jnp.dot(y_bf16, w_ref[...], preferred_element_type=F32) + b_ref[...]
    x_new = x + _sel_mod(mod_ref, is_ctx, 2) * acc
    xo_ref[...] = x_new
    h2 = _adanorm(x_new, g_ref[...], _sel_mod(mod_ref, is_ctx, 3), _sel_mod(mod_ref, is_ctx, 4))
    ho_ref[...] = h2.astype(ho_ref.dtype)


def _mm_res_kernel(y_ref, x_ref, mod_ref, g_ref, w_ref, b_ref, xo_ref, ho_ref, *, tm, ctx_len):
    _mm_res_body(y_ref[...].astype(BF16), x_ref, mod_ref, g_ref, w_ref, b_ref, xo_ref, ho_ref, tm, ctx_len)


def _lru_mm_res_kernel(hf_ref, hb_ref, gate_ref, x_ref, mod_ref, g_ref, w_ref, b_ref, xo_ref, ho_ref, *, tm, ctx_len):
    y = (hf_ref[...] + hb_ref[...]) * jax.nn.gelu(gate_ref[...])
    _mm_res_body(y.astype(BF16), x_ref, mod_ref, g_ref, w_ref, b_ref, xo_ref, ho_ref, tm, ctx_len)


def _mm_res_call(kern, ys, y_specs, x_all, mods_i, g_ffn, w, bias, ctx_len, tm, h_dtype):
    b, s, d = x_all.shape
    k = w.shape[0]
    return pl.pallas_call(
        functools.partial(kern, tm=tm, ctx_len=ctx_len),
        grid=(b, s // tm),
        in_specs=y_specs + [
            pl.BlockSpec((None, tm, d), lambda bi, j: (bi, j, 0)),
            pl.BlockSpec((None, 2, 8, d), lambda bi, j: (bi, 0, 0, 0)),
            pl.BlockSpec((1, d), lambda bi, j: (0, 0)),
            pl.BlockSpec((k, d), lambda bi, j: (0, 0)),
            pl.BlockSpec((1, d), lambda bi, j: (0, 0)),
        ],
        out_specs=[
            pl.BlockSpec((None, tm, d), lambda bi, j: (bi, j, 0)),
            pl.BlockSpec((None, tm, d), lambda bi, j: (bi, j, 0)),
        ],
        out_shape=[jax.ShapeDtypeStruct((b, s, d), F32), jax.ShapeDtypeStruct((b, s, d), h_dtype)],
        compiler_params=_cparams(("parallel", "parallel")),
        name="mm_res",
    )(*ys, x_all, mods_i, g_ffn.reshape(1, d), w.astype(BF16), bias.reshape(1, d).astype(F32))


def _mm_res(y, x_all, mods_i, g_ffn, w, bias, ctx_len, h_dtype=BF16):
    b, s, d = x_all.shape
    k = w.shape[0]
    tm = _pick_tile(s, (544, 384, 256, 128))
    spec = [pl.BlockSpec((None, tm, k), lambda bi, j: (bi, j, 0))]
    return _mm_res_call(_mm_res_kernel, [y], spec, x_all, mods_i, g_ffn, w, bias, ctx_len, tm, h_dtype)


def _lru_mm_res(hs_f, hs_b, gu, x_all, mods_i, g_ffn, w, ctx_len, h_dtype=BF16):
    b, s, d = x_all.shape
    k = w.shape[0]
    tm = _pick_tile(s, (544, 384, 256, 128))
    specs = [
        pl.BlockSpec((None, tm, k), lambda bi, j: (bi, j, 0)),
        pl.BlockSpec((None, tm, k), lambda bi, j: (bi, j, 0)),
        pl.BlockSpec((None, tm, k), lambda bi, j: (bi, j, 0)),
    ]
    return _mm_res_call(_lru_mm_res_kernel, [hs_f, hs_b, gu], specs, x_all, mods_i, g_ffn, w,
                        jnp.zeros((d,), F32), ctx_len, tm, h_dtype)


def _ffn_kernel(h_ref, x_ref, mod_ref, wg_ref, wu_ref, wd_ref, o_ref, *, tm, ctx_len):
    j = pl.program_id(1)
    h = h_ref[...]
    a = jnp.dot(h, wg_ref[...], preferred_element_type=F32)
    u = jnp.dot(h, wu_ref[...], preferred_element_type=F32)
    act = (a * jax.nn.sigmoid(a) * u).astype(BF16)
    y = jnp.dot(act, wd_ref[...], preferred_element_type=F32)
    x = x_ref[...]
    is_ctx = _row_is_ctx(j, tm, ctx_len, x.shape)
    o_ref[...] = x + _sel_mod(mod_ref, is_ctx, 5) * y


def _ffn_dense(h2, x_all, mods_i, wg, wu, wd, ctx_len):
    b, s, d = x_all.shape
    f = wg.shape[1]
    tm = _pick_tile(s, (544, 384, 256, 128))
    resident = dict(pipeline_mode=pl.Buffered(1))
    return pl.pallas_call(
        functools.partial(_ffn_kernel, tm=tm, ctx_len=ctx_len),
        grid=(b, s // tm),
        in_specs=[
            pl.BlockSpec((None, tm, d), lambda bi, j: (bi, j, 0)),
            pl.BlockSpec((None, tm, d), lambda bi, j: (bi, j, 0)),
            pl.BlockSpec((None, 2, 8, d), lambda bi, j: (bi, 0, 0, 0)),
            pl.BlockSpec((d, f), lambda bi, j: (0, 0), **resident),
            pl.BlockSpec((d, f), lambda bi, j: (0, 0), **resident),
            pl.BlockSpec((f, d), lambda bi, j: (0, 0), **resident),
        ],
        out_specs=pl.BlockSpec((None, tm, d), lambda bi, j: (bi, j, 0)),
        out_shape=jax.ShapeDtypeStruct((b, s, d), F32),
        compiler_params=_cparams(("parallel", "parallel")),
        name="ffn_dense",
    )(h2, x_all, mods_i, wg.astype(BF16), wu.astype(BF16), wd.astype(BF16))


def _lru_bwd_tile(j, nt, nct):
    return jnp.where(j < nct, nct - 1 - j, nt - 1 - (j - nct))


def _lru_scan_kernel(um_f, up_f, un_f, um_b, up_b, un_b, cw_ref, cb_ref, wax_f, wax_b, ba_ref, bx_ref, lam_ref,
                     hsf_ref, hsb_ref, ext, af_s, bf_s, ab_s, bb_s, carry, *, tt, nt, nct, wl, nb):
    j = pl.program_id(1)

    @pl.when(j == 0)
    def _():
        carry[...] = jnp.zeros_like(carry)

    def coeffs(um, up, un, t, d, wax_ref, a_s, b_s):
        prev_ok = jnp.logical_and(t != 0, t != nct)
        next_ok = jnp.logical_and(t != nct - 1, t != nt - 1)
        ext[:, SUBLANES:SUBLANES + tt, :] = um[...]
        ext[:, 0:SUBLANES, :] = jnp.where(prev_ok, up[...], 0.0)
        ext[:, SUBLANES + tt:2 * SUBLANES + tt, :] = jnp.where(next_ok, un[...], 0.0)
        uc = cb_ref[...].reshape(1, 1, wl)
        for k in range(4):
            uc = uc + cw_ref[k:k + 1, :].reshape(1, 1, wl) * ext[:, SUBLANES - 2 + k:SUBLANES - 2 + k + tt, :]
        sp = jax.nn.softplus(-lam_ref[d:d + 1, :])
        for hb in range(wl // LANES):
            sl = slice(hb * LANES, (hb + 1) * LANES)
            ub = uc[:, :, sl].reshape(nb * tt, LANES)
            pre = jnp.dot(ub.astype(BF16), wax_ref[hb], preferred_element_type=F32)
            r = jax.nn.sigmoid(pre[:, :LANES] + ba_ref[d:d + 1, sl])
            i = jax.nn.sigmoid(pre[:, LANES:] + bx_ref[d:d + 1, sl])
            log_a = (-LRU_C) * r * sp[:, sl]
            a = jnp.exp(log_a)
            b = jnp.sqrt(-jnp.tanh(log_a) * (a * a + 1.0)) * (i * ub)
            a_s[:, :, sl] = a.reshape(nb, tt, LANES)
            b_s[:, :, sl] = b.reshape(nb, tt, LANES)

    tb = _lru_bwd_tile(j, nt, nct)
    coeffs(um_f, up_f, un_f, j, 0, wax_f, af_s, bf_s)
    coeffs(um_b, up_b, un_b, tb, 1, wax_b, ab_s, bb_s)

    def step(t, c):
        hf, hb = c
        hf = af_s[:, pl.ds(t, 1), :].reshape(nb, wl) * hf + bf_s[:, pl.ds(t, 1), :].reshape(nb, wl)
        hsf_ref[:, pl.ds(t, 1), :] = hf.reshape(nb, 1, wl)
        r = tt - 1 - t
        hb = ab_s[:, pl.ds(r, 1), :].reshape(nb, wl) * hb + bb_s[:, pl.ds(r, 1), :].reshape(nb, wl)
        hsb_ref[:, pl.ds(r, 1), :] = hb.reshape(nb, 1, wl)
        return hf, hb

    hf, hb = lax.fori_loop(0, tt, step, (carry[0], carry[1]), unroll=8)
    carry[0] = hf
    carry[1] = hb


def _lru_scan(gu, conv_w, conv_b, w_a, b_a, w_x, b_x, lam, ctx_len):
    nb, s, d2 = gu.shape
    d = d2 // 2
    tt = 256
    assert s % tt == 0 and ctx_len % tt == 0 and d % (RNN_BLOCKS * LANES) == 0 and d // RNN_BLOCKS == LANES
    nt, nct = s // tt, ctx_len // tt
    wl = 2 * LANES
    ncg = d // wl
    coff = d // wl
    r8 = tt // SUBLANES
    last8 = s // SUBLANES - 1
    wax = jnp.concatenate([w_a, w_x], axis=-1).astype(BF16)

    def main_f(cg, j): return (0, j, coff + cg)
    def prev_f(cg, j): return (0, jnp.maximum(j * r8 - 1, 0), coff + cg)
    def next_f(cg, j): return (0, jnp.minimum((j + 1) * r8, last8), coff + cg)
    def tb_(j): return _lru_bwd_tile(j, nt, nct)
    def main_b(cg, j): return (0, tb_(j), coff + cg)
    def prev_b(cg, j): return (0, jnp.maximum(tb_(j) * r8 - 1, 0), coff + cg)
    def next_b(cg, j): return (0, jnp.minimum((tb_(j) + 1) * r8, last8), coff + cg)

    big = lambda im: pl.BlockSpec((nb, tt, wl), im)
    halo = lambda im: pl.BlockSpec((nb, SUBLANES, wl), im)
    vec2 = pl.BlockSpec((2, wl), lambda cg, j: (0, cg))
    return pl.pallas_call(
        functools.partial(_lru_scan_kernel, tt=tt, nt=nt, nct=nct, wl=wl, nb=nb),
        grid=(ncg, nt),
        in_specs=[
            big(main_f), halo(prev_f), halo(next_f), big(main_b), halo(prev_b), halo(next_b),
            pl.BlockSpec((4, wl), lambda cg, j: (0, cg)),
            pl.BlockSpec((1, wl), lambda cg, j: (0, cg)),
            pl.BlockSpec((None, wl // LANES, LANES, 2 * LANES), lambda cg, j: (0, cg, 0, 0)),
            pl.BlockSpec((None, wl // LANES, LANES, 2 * LANES), lambda cg, j: (1, cg, 0, 0)),
            vec2, vec2, vec2,
        ],
        out_specs=[
            pl.BlockSpec((nb, tt, wl), lambda cg, j: (0, j, cg)),
            pl.BlockSpec((nb, tt, wl), lambda cg, j: (0, tb_(j), cg)),
        ],
        out_shape=[jax.ShapeDtypeStruct((nb, s, d), F32), jax.ShapeDtypeStruct((nb, s, d), F32)],
        scratch_shapes=[
            pltpu.VMEM((nb, tt + 2 * SUBLANES, wl), F32),
            pltpu.VMEM((nb, tt, wl), F32), pltpu.VMEM((nb, tt, wl), F32),
            pltpu.VMEM((nb, tt, wl), F32), pltpu.VMEM((nb, tt, wl), F32),
            pltpu.VMEM((2, nb, wl), F32),
        ],
        compiler_params=_cparams(("parallel", "arbitrary")),
        name="lru_scan",
    )(gu, gu, gu, gu, gu, gu, conv_w, conv_b.reshape(1, d), wax, wax, b_a, b_x, lam)


def _rope_tables(seq, ctx_len):
    q = HEAD_DIM // 4
    inv_freq = ROPE_BASE ** (-jnp.arange(q, dtype=F32) / q)
    pos = jnp.arange(seq, dtype=jnp.int32)
    row = (pos // GRID_W).astype(F32)[:, None] * inv_freq
    col = (pos % GRID_W).astype(F32)[:, None] * inv_freq
    ang = jnp.concatenate([row, row, col, col], axis=-1)
    sign = jnp.concatenate([-jnp.ones((q,), F32), jnp.ones((q,), F32)] * 2)
    cos = jnp.concatenate([jnp.ones((ctx_len, HEAD_DIM), F32), jnp.cos(ang)], axis=0)
    sin = jnp.concatenate([jnp.zeros((ctx_len, HEAD_DIM), F32), jnp.sin(ang) * sign], axis=0)
    return jnp.tile(cos, (1, 2)), jnp.tile(sin, (1, 2))


def _qkv_kernel(x_ref, mod_ref, g_ref, w_ref, cos_ref, sin_ref, o_ref, *, tm, ctx_len, n_q, n_rope):
    j = pl.program_id(1)
    x = x_ref[...]
    is_ctx = _row_is_ctx(j, tm, ctx_len, x.shape)
    h = _adanorm(x, g_ref[...], _sel_mod(mod_ref, is_ctx, 0), _sel_mod(mod_ref, is_ctx, 1))
    acc = jnp.dot(h.astype(BF16), w_ref[...], preferred_element_type=F32)
    cos = cos_ref[...]
    sin = sin_ref[...]
    lane = lax.broadcasted_iota(jnp.int32, (tm, LANES), 1)
    first_half = (lane % (HEAD_DIM // 2)) < (HEAD_DIM // 4)
    q_scale = HEAD_DIM ** -0.5
    for c in range(acc.shape[1] // LANES):
        v = acc[:, c * LANES:(c + 1) * LANES]
        if c < n_rope:
            partner = jnp.where(first_half, pltpu.roll(v, LANES - HEAD_DIM // 4, 1), pltpu.roll(v, HEAD_DIM // 4, 1))
            v = v * cos + partner * sin
            if c < n_q:
                v = v * q_scale
        o_ref[:, c * LANES:(c + 1) * LANES] = v.astype(o_ref.dtype)


def _qkv_proj(x_all, mods_i, g, w, cos_t, sin_t, ctx_len):
    b, s, d = x_all.shape
    n = w.shape[1]
    tm = _pick_tile(s, (544, 384, 256, 128))
    n_q = N_HEADS * HEAD_DIM // LANES
    n_rope = (N_HEADS + N_KV_HEADS) * HEAD_DIM // LANES
    return pl.pallas_call(
        functools.partial(_qkv_kernel, tm=tm, ctx_len=ctx_len, n_q=n_q, n_rope=n_rope),
        grid=(b, s // tm),
        in_specs=[
            pl.BlockSpec((None, tm, d), lambda bi, j: (bi, j, 0)),
            pl.BlockSpec((None, 2, 8, d), lambda bi, j: (bi, 0, 0, 0)),
            pl.BlockSpec((1, d), lambda bi, j: (0, 0)),
            pl.BlockSpec((d, n), lambda bi, j: (0, 0)),
            pl.BlockSpec((tm, LANES), lambda bi, j: (j, 0)),
            pl.BlockSpec((tm, LANES), lambda bi, j: (j, 0)),
        ],
        out_specs=pl.BlockSpec((None, tm, n), lambda bi, j: (bi, j, 0)),
        out_shape=jax.ShapeDtypeStruct((b, s, n), BF16),
        compiler_params=_cparams(("parallel", "parallel")),
        name="qkv_proj",
    )(x_all, mods_i, g.reshape(1, d), w.astype(BF16), cos_t, sin_t)


def _attn_kernel(q_ref, kp_ref, km_ref, kn_ref, kc_ref, vp_ref, vm_ref, vn_ref, vc_ref, sink_ref, o_ref,
                 *, tq, nct, seq, ctx_len):
    j = pl.program_id(1)
    k_all = jnp.concatenate([kp_ref[...], km_ref[...], kn_ref[...], kc_ref[...]], axis=0)
    v_all = jnp.concatenate([vp_ref[...], vm_ref[...], vn_ref[...], vc_ref[...]], axis=0)
    n_loc = 2 * tq
    n_keys = n_loc + ctx_len
    row = lax.broadcasted_iota(jnp.int32, (tq, n_keys), 0)
    col = lax.broadcasted_iota(jnp.int32, (tq, n_keys), 1)
    p0 = (j - nct) * tq
    kpos = p0 - tq // 2 + col
    band = jnp.abs(row - (col - tq // 2)) <= WINDOW
    in_seq = jnp.logical_and(kpos >= 0, kpos < seq)
    valid_loc = jnp.logical_and(jnp.logical_and(band, in_seq), j >= nct)
    valid = jnp.logical_or(col >= n_loc, valid_loc)
    group = N_HEADS // N_KV_HEADS
    outs = []
    for h in range(N_HEADS):
        kvh = h // group
        qh = q_ref[:, h * HEAD_DIM:(h + 1) * HEAD_DIM]
        kk = k_all[:, kvh * HEAD_DIM:(kvh + 1) * HEAD_DIM]
        vv = v_all[:, kvh * HEAD_DIM:(kvh + 1) * HEAD_DIM]
        s = lax.dot_general(qh, kk, (((1,), (1,)), ((), ())), preferred_element_type=F32)
        s = jnp.where(valid, s, -1e30)
        sink = sink_ref[h:h + 1, 0:1]
        m = jnp.maximum(jnp.max(s, axis=1, keepdims=True), sink)
        p = jnp.exp(s - m)
        denom = jnp.sum(p, axis=1, keepdims=True) + jnp.exp(sink - m)
        o = jnp.dot(p.astype(BF16), vv, preferred_element_type=F32)
        outs.append((o / denom).astype(o_ref.dtype))
    o_ref[...] = jnp.concatenate(outs, axis=1)


def _attention(qkv, sinks, ctx_len):
    b, s, _ = qkv.shape
    seq = s - ctx_len
    tq = 2 * WINDOW
    half = tq // 2
    assert ctx_len % tq == 0 and seq % tq == 0
    nct = ctx_len // tq
    dq = N_HEADS * HEAD_DIM
    dkv = N_KV_HEADS * HEAD_DIM
    kcol, vcol = dq // dkv, dq // dkv + 1
    last_half = s // half - 1
    sink_b = jnp.broadcast_to(sinks.astype(F32)[:, None], (N_HEADS, LANES))

    def prev(col): return lambda bi, j: (bi, jnp.maximum(2 * j - 1, 0), col)
    def main(col): return lambda bi, j: (bi, j, col)
    def nxt(col): return lambda bi, j: (bi, jnp.minimum(2 * j + 2, last_half), col)
    def ctx(col): return lambda bi, j: (bi, 0, col)
    kv_specs = lambda col: [
        pl.BlockSpec((None, half, dkv), prev(col)), pl.BlockSpec((None, tq, dkv), main(col)),
        pl.BlockSpec((None, half, dkv), nxt(col)), pl.BlockSpec((None, ctx_len, dkv), ctx(col)),
    ]
    return pl.pallas_call(
        functools.partial(_attn_kernel, tq=tq, nct=nct, seq=seq, ctx_len=ctx_len),
        grid=(b, s // tq),
        in_specs=[pl.BlockSpec((None, tq, dq), lambda bi, j: (bi, j, 0))] + kv_specs(kcol) + kv_specs(vcol)
        + [pl.BlockSpec((N_HEADS, LANES), lambda bi, j: (0, 0))],
        out_specs=pl.BlockSpec((None, tq, dq), lambda bi, j: (bi, j, 0)),
        out_shape=jax.ShapeDtypeStruct((b, s, dq), BF16),
        compiler_params=_cparams(("parallel", "parallel")),
        name="swa_attention",
    )(qkv, *([qkv] * 8), sink_b)


def _dwconv3_kernel(um, up, un, cw_ref, cb_ref, g1_ref, g2_ref, v_ref, ext, *, tt, nt, nct, c):
    t = pl.program_id(1)
    prev_ok = jnp.logical_and(t != 0, t != nct)
    next_ok = jnp.logical_and(t != nct - 1, t != nt - 1)
    ext[SUBLANES:SUBLANES + tt, :] = um[...]
    ext[0:SUBLANES, :] = jnp.where(prev_ok, up[...], 0.0)
    ext[SUBLANES + tt:2 * SUBLANES + tt, :] = jnp.where(next_ok, un[...], 0.0)
    y = cb_ref[...]
    for k in range(3):
        y = y + cw_ref[k:k + 1, :] * ext[SUBLANES - 1 + k:SUBLANES - 1 + k + tt, :]
    g1_ref[...] = y[:, 0:c]
    g2_ref[...] = y[:, c:2 * c]
    v_ref[...] = y[:, 2 * c:3 * c]


def _dwconv3(u0, conv_w, conv_b, ctx_len):
    b, s, c3 = u0.shape
    c = c3 // 3
    tt = 256
    assert s % tt == 0 and ctx_len % tt == 0
    nt, nct = s // tt, ctx_len // tt
    r8 = tt // SUBLANES
    last8 = s // SUBLANES - 1
    out = jax.ShapeDtypeStruct((b, s, c), F32)
    ospec = pl.BlockSpec((None, tt, c), lambda bi, t: (bi, t, 0))
    return pl.pallas_call(
        functools.partial(_dwconv3_kernel, tt=tt, nt=nt, nct=nct, c=c),
        grid=(b, nt),
        in_specs=[
            pl.BlockSpec((None, tt, c3), lambda bi, t: (bi, t, 0)),
            pl.BlockSpec((None, SUBLANES, c3), lambda bi, t: (bi, jnp.maximum(t * r8 - 1, 0), 0)),
            pl.BlockSpec((None, SUBLANES, c3), lambda bi, t: (bi, jnp.minimum((t + 1) * r8, last8), 0)),
            pl.BlockSpec((3, c3), lambda bi, t: (0, 0)),
            pl.BlockSpec((1, c3), lambda bi, t: (0, 0)),
        ],
        out_specs=[ospec, ospec, ospec],
        out_shape=[out, out, out],
        scratch_shapes=[pltpu.VMEM((tt + 2 * SUBLANES, c3), F32)],
        compiler_params=_cparams(("parallel", "parallel")),
        name="hyena_dwconv",
    )(u0, u0, u0, conv_w, conv_b.reshape(1, c3))


def _filter_features(seq, ctx_len):
    def feats(length):
        t = jnp.linspace(0.0, 1.0, length, dtype=F32)[:, None]
        omega = (2.0 * math.pi / length) * jnp.arange(length, dtype=F32)[:, None]
        bands = jnp.linspace(1e-4, FILTER_BANDS - 1, FILTER_BANDS, dtype=F32)[None, :]
        return jnp.concatenate([t, jnp.cos(bands * omega), -jnp.sin(bands * omega)], axis=-1)
    z = jnp.concatenate([feats(ctx_len), feats(seq)], axis=0)
    return jnp.pad(z, ((0, 0), (0, 40 - z.shape[1])))


def _filter_kernel(z_ref, w1_ref, b1_ref, w2_ref, b2_ref, w3_ref, b3_ref, fr_ref, w4_ref, dl_ref, hw_ref, sum_ref,
                   *, tm, ctx_len, c):
    j = pl.program_id(0)

    @pl.when(j == 0)
    def _():
        sum_ref[...] = jnp.zeros_like(sum_ref)

    z = z_ref[...]
    fr = fr_ref[...]
    dot = functools.partial(jnp.dot, preferred_element_type=F32, precision=HIGHEST)
    hid = jnp.sin(fr * (dot(z, w1_ref[...]) + b1_ref[...]))
    hid = jnp.sin(fr * (dot(hid, w2_ref[...]) + b2_ref[...]))
    hid = jnp.sin(fr * (dot(hid, w3_ref[...]) + b3_ref[...]))
    h = dot(hid, w4_ref[...])
    window = jnp.exp(-z[:, 0:1] * jnp.abs(dl_ref[...]))
    rows = j * tm + lax.broadcasted_iota(jnp.int32, (tm, c), 0)
    first = jnp.logical_or(rows == 0, rows == ctx_len)
    is_ctx = j * tm < ctx_len
    srow = jnp.where(is_ctx, 0, 1)
    for q in range(4):
        hq = h[:, q * c:(q + 1) * c] * window
        if q % 2 == 1:
            hq = jnp.where(first, 0.0, hq)
        hw_ref[:, q * c:(q + 1) * c] = hq
        part = jnp.sum(jnp.abs(hq), axis=0, keepdims=True)
        for r in range(2):
            sum_ref[r:r + 1, q * c:(q + 1) * c] += jnp.where(srow == r, part, 0.0)


def _hyena_filters(seq, ctx_len, w1, b1, w2, b2, w3, b3, freq, w4, c):
    s = seq + ctx_len
    z = _filter_features(seq, ctx_len)
    tm = 256
    assert ctx_len % tm == 0 and seq % tm == 0
    fw = w1.shape[1]
    w1p = jnp.pad(w1.astype(F32), ((0, 40 - w1.shape[0]), (0, 0)))
    deltas = jnp.linspace(math.log(DECAY_TARGET) / SLOW_DECAY_PCT, math.log(DECAY_TARGET) / FAST_DECAY_PCT, c,
                          dtype=F32)[None, :]
    full = lambda shape: pl.BlockSpec(shape, lambda j: tuple(0 for _ in shape))
    return pl.pallas_call(
        functools.partial(_filter_kernel, tm=tm, ctx_len=ctx_len, c=c),
        grid=(s // tm,),
        in_specs=[
            pl.BlockSpec((tm, 40), lambda j: (j, 0)),
            full((40, fw)), full((1, fw)), full((fw, fw)), full((1, fw)), full((fw, fw)), full((1, fw)),
            full((1, fw)), full((fw, 4 * c)), full((1, c)),
        ],
        out_specs=[pl.BlockSpec((tm, 4 * c), lambda j: (j, 0)), pl.BlockSpec((SUBLANES, 4 * c), lambda j: (0, 0))],
        out_shape=[jax.ShapeDtypeStruct((s, 4 * c), F32), jax.ShapeDtypeStruct((SUBLANES, 4 * c), F32)],
        compiler_params=_cparams(("arbitrary",)),
        name="hyena_filter_mlp",
    )(z, w1p, b1.reshape(1, fw), w2, b2.reshape(1, fw), w3, b3.reshape(1, fw), freq.reshape(1, fw), w4, deltas)


def _fft_tables(seq, ctx_len):
    n1 = FFT_N1
    rc, rl = ctx_len // n1, seq // n1
    n2c, n2l = 2 * rc, 2 * rl
    k2n = n2l + n2c
    r = rc + rl

    def cis(num, den):
        ang = (-2.0 * math.pi / den) * (num % den).astype(F32)
        return jnp.cos(ang), jnp.sin(ang)

    ar = jnp.arange
    fl_r, fl_i = cis(ar(n2l)[:, None] * ar(rl)[None, :], n2l)
    fc_r, fc_i = cis(ar(n2c)[:, None] * ar(rc)[None, :], n2c)
    zl, zc = jnp.zeros((n2l, rc), F32), jnp.zeros((n2c, rl), F32)
    f_r = jnp.concatenate([jnp.concatenate([zl, fl_r], 1), jnp.concatenate([fc_r, zc], 1)], 0)
    f_i = jnp.concatenate([jnp.concatenate([zl, fl_i], 1), jnp.concatenate([fc_i, zc], 1)], 0)
    scale = jnp.concatenate([jnp.full((n2l,), 1.0 / (n1 * n2l), F32), jnp.full((n2c,), 1.0 / (n1 * n2c), F32)])
    g_r, g_i = (f_r * scale[:, None]).T, (f_i * scale[:, None]).T
    tl_r, tl_i = cis(ar(n2l)[:, None] * ar(n1)[None, :], n1 * n2l)
    tc_r, tc_i = cis(ar(n2c)[:, None] * ar(n1)[None, :], n1 * n2c)
    tw_r, tw_i = jnp.concatenate([tl_r, tc_r], 0), jnp.concatenate([tl_i, tc_i], 0)
    bl = lambda a: jnp.broadcast_to(a[:, :, None], a.shape + (LANES,))
    f1_r, f1_i = cis(ar(n1)[:, None] * ar(n1)[None, :], n1)
    hi = lambda a: a.astype(BF16)
    lo = lambda a: (a - a.astype(BF16).astype(F32)).astype(BF16)
    f1_hl = jnp.concatenate([hi(f1_r), hi(f1_i), lo(f1_r), lo(f1_i)], axis=0)
    return dict(f_r=f_r, f_i=f_i, g_r=g_r, g_i=g_i, f1_hl=f1_hl, f1_h=f1_hl[:2 * n1],
                ta_r=bl(tw_r.T), ta_i=bl(tw_i.T),
                tb_r=bl(tw_r), tb_i=bl(tw_i),
                k2n=k2n, n2l=n2l, r=r)


def _hdot(a, b):
    return jnp.dot(a, b, preferred_element_type=F32, precision=HIGHEST)


FFT_NB1 = 8
FFT_CC = 1024


def _rows(ref, lead, n):
    blk = ref[lead + (slice(None), slice(n, n + 1), slice(None))]
    return blk.reshape(blk.shape[0], blk.shape[2])


def _fft_a_kernel(x_ref, fr_ref, fi_ref, tr_ref, ti_ref, o_ref, *, packed, cc):
    fr, fi = fr_ref[...], fi_ref[...]
    k2n = fr.shape[0]
    for n in range(FFT_NB1):
        if packed:
            xr, xi = _rows(x_ref, (0,), n), _rows(x_ref, (1,), n)
            a_r = _hdot(fr, xr) - _hdot(fi, xi)
            a_i = _hdot(fi, xr) + _hdot(fr, xi)
        else:
            x = _rows(x_ref, (), n)
            a_r, a_i = _hdot(fr, x), _hdot(fi, x)
        tr, ti = tr_ref[n], ti_ref[n]
        o_r, o_i = [], []
        for l in range(cc // LANES):
            cr, ci = a_r[:, l * LANES:(l + 1) * LANES], a_i[:, l * LANES:(l + 1) * LANES]
            o_r.append(cr * tr - ci * ti)
            o_i.append(cr * ti + ci * tr)
        o_ref[0, :, n:n + 1, :] = jnp.concatenate(o_r, axis=1).reshape(k2n, 1, cc)
        o_ref[1, :, n:n + 1, :] = jnp.concatenate(o_i, axis=1).reshape(k2n, 1, cc)


def _fft_stage_a(xv, tb, packed, c):
    k2n, r = tb["k2n"], tb["r"]
    cc = min(c, FFT_CC)
    nnb, ncc = FFT_N1 // FFT_NB1, c // cc
    mat = pl.BlockSpec((k2n, r), lambda *a: (0, 0))
    if packed:
        p = xv.shape[1]
        grid = (p, nnb, ncc)
        x_spec = pl.BlockSpec((2, None, r, FFT_NB1, cc), lambda pi, nb, cb: (0, pi, 0, nb, cb))
        t_spec = pl.BlockSpec((FFT_NB1, k2n, LANES), lambda pi, nb, cb: (nb, 0, 0))
        o_spec = pl.BlockSpec((None, 2, k2n, FFT_NB1, cc), lambda pi, nb, cb: (pi, 0, 0, nb, cb))
        o_shape = jax.ShapeDtypeStruct((p, 2, k2n, FFT_N1, c), F32)
        sem = ("parallel", "parallel", "parallel")
    else:
        grid = (nnb, ncc)
        x_spec = pl.BlockSpec((r, FFT_NB1, cc), lambda nb, cb: (0, nb, cb))
        t_spec = pl.BlockSpec((FFT_NB1, k2n, LANES), lambda nb, cb: (nb, 0, 0))
        o_spec = pl.BlockSpec((2, k2n, FFT_NB1, cc), lambda nb, cb: (0, 0, nb, cb))
        o_shape = jax.ShapeDtypeStruct((2, k2n, FFT_N1, c), F32)
        sem = ("parallel", "parallel")
    return pl.pallas_call(
        functools.partial(_fft_a_kernel, packed=packed, cc=cc),
        grid=grid,
        in_specs=[x_spec, mat, mat, t_spec, t_spec],
        out_specs=o_spec,
        out_shape=o_shape,
        compiler_params=_cparams(sem),
        name="hyena_fft_a",
    )(xv, tb["f_r"], tb["f_i"], tb["ta_r"], tb["ta_i"])


def _cdot3(fhl_ref, fh_ref, x):
    x_hi = x.astype(BF16)
    x_lo = (x - x_hi.astype(F32)).astype(BF16)
    s = jnp.dot(fhl_ref[...], x_hi, preferred_element_type=F32)
    t = jnp.dot(fh_ref[...], x_lo, preferred_element_type=F32)
    n = FFT_N1
    return s[0:n] + s[2 * n:3 * n] + t[0:n], s[n:2 * n] + s[3 * n:4 * n] + t[n:2 * n]


def _fft_bf_kernel(a_ref, fhl_ref, fh_ref, o_ref):
    rr, ir = _cdot3(fhl_ref, fh_ref, a_ref[0])
    ri, ii = _cdot3(fhl_ref, fh_ref, a_ref[1])
    o_ref[0] = rr - ii
    o_ref[1] = ir + ri


def _fft_stage_b_filter(af, tb, c4):
    k2n = tb["k2n"]
    cc = min(c4, FFT_CC)
    mhl = pl.BlockSpec((4 * FFT_N1, FFT_N1), lambda k, cb: (0, 0))
    mh = pl.BlockSpec((2 * FFT_N1, FFT_N1), lambda k, cb: (0, 0))
    blk = pl.BlockSpec((2, None, FFT_N1, cc), lambda k, cb: (0, k, 0, cb))
    return pl.pallas_call(
        _fft_bf_kernel,
        grid=(k2n, c4 // cc),
        in_specs=[blk, mhl, mh],
        out_specs=blk,
        out_shape=jax.ShapeDtypeStruct(af.shape, F32),
        compiler_params=_cparams(("parallel", "parallel")),
        name="hyena_fft_b_filter",
    )(af, tb["f1_hl"], tb["f1_h"])


def _fft_mid_kernel(a_ref, h0_ref, h1_ref, s0_ref, s1_ref, fhl_ref, fh_ref, tr_ref, ti_ref, o_ref, *, n2l, cc):
    k2 = pl.program_id(1)
    rr, ir = _cdot3(fhl_ref, fh_ref, a_ref[0])
    ri, ii = _cdot3(fhl_ref, fh_ref, a_ref[1])
    xr = rr - ii
    xi = ir + ri
    norm = s0_ref[...] + s1_ref[...]
    inv = 1.0 / jnp.where(k2 >= n2l, norm[0:1, :], norm[1:2, :])
    kr = (h0_ref[0] + h1_ref[0]) * inv
    ki = (h0_ref[1] - h1_ref[1]) * inv
    yr = xr * kr - xi * ki
    yi = xr * ki + xi * kr
    r_yr, i_yr = _cdot3(fhl_ref, fh_ref, yr)
    r_yi, i_yi = _cdot3(fhl_ref, fh_ref, yi)
    br = r_yr + i_yi
    bi = r_yi - i_yr
    tr, ti = tr_ref[...], ti_ref[...]
    for l in range(cc // LANES):
        sl = slice(l * LANES, (l + 1) * LANES)
        cr, ci = br[:, sl], bi[:, sl]
        o_ref[0, :, sl] = cr * tr + ci * ti
        o_ref[1, :, sl] = ci * tr - cr * ti


def _fft_mid(a, hspec, sums, tb, order, c):
    p = a.shape[0]
    k2n, n2l = tb["k2n"], tb["n2l"]
    cc = min(c, FFT_CC)
    ncc = c // cc
    mhl = pl.BlockSpec((4 * FFT_N1, FFT_N1), lambda pi, k, cb: (0, 0))
    mh = pl.BlockSpec((2 * FFT_N1, FFT_N1), lambda pi, k, cb: (0, 0))
    blk = pl.BlockSpec((None, 2, None, FFT_N1, cc), lambda pi, k, cb: (pi, 0, k, 0, cb))
    hs = lambda side: pl.BlockSpec((2, None, FFT_N1, cc), lambda pi, k, cb: (0, k, 0, (2 * order + side) * ncc + cb))
    ss = lambda side: pl.BlockSpec((SUBLANES, cc), lambda pi, k, cb: (0, (2 * order + side) * ncc + cb))
    tw = pl.BlockSpec((None, FFT_N1, LANES), lambda pi, k, cb: (k, 0, 0))
    return pl.pallas_call(
        functools.partial(_fft_mid_kernel, n2l=n2l, cc=cc),
        grid=(p, k2n, ncc),
        in_specs=[blk, hs(0), hs(1), ss(0), ss(1), mhl, mh, tw, tw],
        out_specs=blk,
        out_shape=jax.ShapeDtypeStruct(a.shape, F32),
        compiler_params=_cparams(("parallel", "parallel", "parallel")),
        name="hyena_fft_mid",
    )(a, hspec, hspec, sums, sums, tb["f1_hl"], tb["f1_h"], tb["tb_r"], tb["tb_i"])


def _fft_inv_kernel(b_ref, gr_ref, gi_ref, g_ref, v_ref, skip_ref, o_ref, *, cc):
    gr, gi = gr_ref[...], gi_ref[...]
    r = gr.shape[0]
    skip = skip_ref[...]
    for n in range(FFT_NB1):
        br, bi = _rows(b_ref, (0,), n), _rows(b_ref, (1,), n)
        yr = _hdot(gr, br) + _hdot(gi, bi)
        yi = _hdot(gr, bi) - _hdot(gi, br)
        for half, y in ((0, yr), (1, yi)):
            z = _rows(g_ref, (half,), n) * (y + skip * _rows(v_ref, (half,), n))
            o_ref[half, :, n:n + 1, :] = z.reshape(r, 1, cc).astype(o_ref.dtype)


def _fft_inverse_gate(bv, gv, vv, skip, tb, c):
    p = bv.shape[0]
    k2n, r = tb["k2n"], tb["r"]
    cc = min(c, FFT_CC)
    mat = pl.BlockSpec((r, k2n), lambda pi, nb, cb: (0, 0))
    xs = pl.BlockSpec((2, None, r, FFT_NB1, cc), lambda pi, nb, cb: (0, pi, 0, nb, cb))
    return pl.pallas_call(
        functools.partial(_fft_inv_kernel, cc=cc),
        grid=(p, FFT_N1 // FFT_NB1, c // cc),
        in_specs=[pl.BlockSpec((None, 2, k2n, FFT_NB1, cc), lambda pi, nb, cb: (pi, 0, 0, nb, cb)), mat, mat, xs, xs,
                  pl.BlockSpec((1, cc), lambda pi, nb, cb: (0, cb))],
        out_specs=xs,
        out_shape=jax.ShapeDtypeStruct(gv.shape, F32),
        compiler_params=_cparams(("parallel", "parallel", "parallel")),
        name="hyena_fft_inv",
    )(bv, tb["g_r"], tb["g_i"], gv, vv, skip.reshape(1, c))


def _hyena_mix(u0, conv_w, conv_b, fw1, fb1, fw2, fb2, fw3, fb3, ffreq, fw4, skip, ctx_len):
    b, s, c3 = u0.shape
    c = c3 // 3
    seq = s - ctx_len
    assert b % 2 == 0 and seq % FFT_N1 == 0 and ctx_len % FFT_N1 == 0
    p = b // 2
    r = s // FFT_N1
    g1, g2, v = _dwconv3(u0, conv_w, conv_b, ctx_len)
    hw, sums = _hyena_filters(seq, ctx_len, fw1, fb1, fw2, fb2, fw3, fb3, ffreq, fw4, c)
    tb = _fft_tables(seq, ctx_len)
    af = _fft_stage_a(hw.reshape(r, FFT_N1, 4 * c), tb, False, 4 * c)
    hspec = _fft_stage_b_filter(af, tb, 4 * c)
    view = lambda t: t.reshape(2, p, r, FFT_N1, c)
    z = v
    for order, gate in ((0, g1), (1, g2)):
        a = _fft_stage_a(view(z), tb, True, c)
        bm = _fft_mid(a, hspec, sums, tb, order, c)
        z = _fft_inverse_gate(bm, view(gate), view(z), skip[order], tb, c).reshape(b, s, c)
    return z


def _router_kernel(h_ref, r_ref, g_ref, hb_ref, *, ne):
    hb_ref[...] = h_ref[...].astype(BF16)
    logits = jnp.dot(h_ref[...], r_ref[...], preferred_element_type=F32, precision=HIGHEST)
    lane = lax.broadcasted_iota(jnp.int32, logits.shape, 1)
    neg = -jnp.inf
    logits = jnp.where(lane < ne, logits, neg)
    m1 = jnp.max(logits, axis=1, keepdims=True)
    i1 = jnp.min(jnp.where(logits == m1, lane, LANES), axis=1, keepdims=True)
    rest = jnp.where(lane == i1, neg, logits)
    m2 = jnp.max(rest, axis=1, keepdims=True)
    i2 = jnp.min(jnp.where(rest == m2, lane, LANES), axis=1, keepdims=True)
    e2 = jnp.exp(m2 - m1)
    w1 = 1.0 / (1.0 + e2)
    g_ref[...] = jnp.where(lane == i1, w1, 0.0) + jnp.where(lane == i2, e2 * w1, 0.0)


def _router(h2, router):
    b, s, d = h2.shape
    ne = router.shape[1]
    tm = _pick_tile(s, (544, 384, 256, 128))
    rp = jnp.pad(router.astype(F32), ((0, 0), (0, LANES - ne)))
    return pl.pallas_call(
        functools.partial(_router_kernel, ne=ne),
        grid=(b, s // tm),
        in_specs=[pl.BlockSpec((None, tm, d), lambda bi, j: (bi, j, 0)), pl.BlockSpec((d, LANES), lambda bi, j: (0, 0))],
        out_specs=[pl.BlockSpec((None, tm, LANES), lambda bi, j: (bi, j, 0)),
                   pl.BlockSpec((None, tm, d), lambda bi, j: (bi, j, 0))],
        out_shape=[jax.ShapeDtypeStruct((b, s, LANES), F32), jax.ShapeDtypeStruct((b, s, d), BF16)],
        compiler_params=_cparams(("parallel", "parallel")),
        name="moe_router",
    )(h2, rp)


def _moe_sparse_kernel(h_ref, x_ref, mod_ref, gate_ref, wg_ref, wu_ref, wd_ref, o_ref, dest, cnt, xe, ye,
                       *, tb, min_tiles, main_tiles, rb, ctx_len, ne, nf):
    j, e, f = pl.program_id(1), pl.program_id(2), pl.program_id(3)
    half = tb // 2

    @pl.when(jnp.logical_and(e == 0, f == 0))
    def _():
        tri = (lax.broadcasted_iota(jnp.int32, (rb, rb), 0) > lax.broadcasted_iota(jnp.int32, (rb, rb), 1))
        tri = jnp.where(tri, 1.0, 0.0).astype(BF16)
        carry = jnp.zeros((1, LANES), F32)
        for k in range(tb // rb):
            routed = gate_ref[k * rb:(k + 1) * rb, :] > 0.0
            r01 = jnp.where(routed, 1.0, 0.0)
            before = jnp.dot(tri, r01.astype(BF16), preferred_element_type=F32) + carry
            dest[k * rb:(k + 1) * rb, :] = jnp.where(routed, before, -1.0)
            carry = carry + jnp.sum(r01, axis=0, keepdims=True)
        cnt[...] = jnp.broadcast_to(carry, cnt.shape)
        o_ref[...] = jnp.zeros_like(o_ref)

    lane1 = lax.broadcasted_iota(jnp.int32, (1, LANES), 1)
    n_e = jnp.sum(jnp.where(lane1 == e, cnt[0:1, :], 0.0))
    n_tiles = jnp.ceil(n_e * (1.0 / LANES)).astype(jnp.int32)

    lane_h = lax.broadcasted_iota(jnp.int32, (half, LANES), 1)

    def expert_col(ref, rows):
        return jnp.sum(jnp.where(lane_h == e, ref[rows, :], 0.0), axis=1, keepdims=True)

    def one_hot(rows, first, width):
        dcol = expert_col(dest, rows) - first
        lanef = lane_h.astype(F32)
        pieces = [jnp.where(dcol == lanef + float(k * LANES), 1.0, 0.0).astype(BF16) for k in range(width // LANES)]
        return jnp.concatenate(pieces, axis=1)

    def gather(first, width, dst):
        acc = None
        for hb in range(2):
            rows = slice(hb * half, (hb + 1) * half)
            part = lax.dot_general(one_hot(rows, first, width), h_ref[rows, :], (((0,), (0,)), ((), ())),
                                   preferred_element_type=F32)
            acc = part if acc is None else acc + part
        xe[dst, :] = acc.astype(BF16)
        ye[dst, :] = jnp.zeros((width, ye.shape[1]), F32)

    def expert_ffn(dst):
        x = xe[dst, :]
        a = jnp.dot(x, wg_ref[...], preferred_element_type=F32)
        u = jnp.dot(x, wu_ref[...], preferred_element_type=F32)
        act = (a * jax.nn.sigmoid(a) * u).astype(BF16)
        ye[dst, :] += jnp.dot(act, wd_ref[...], preferred_element_type=F32)

    def scatter(first, width, dst):
        y = ye[dst, :].astype(BF16)
        for hb in range(2):
            rows = slice(hb * half, (hb + 1) * half)
            back = jnp.dot(one_hot(rows, first, width), y, preferred_element_type=F32)
            o_ref[rows, :] += expert_col(gate_ref, rows) * back

    main_rows = main_tiles * LANES
    active = n_tiles > 0
    n_extra = jnp.maximum(n_tiles - main_tiles, 0)

    def extra(fn):
        def body(c, carry):
            off = pl.multiple_of(main_rows + c * LANES, LANES)
            fn(off.astype(F32), LANES, pl.ds(off, LANES))
            return carry
        lax.fori_loop(0, n_extra, body, 0)

    @pl.when(jnp.logical_and(active, f == 0))
    def _():
        gather(0.0, main_rows, slice(0, main_rows))
        extra(gather)

    for tiles in range(min_tiles, main_tiles + 1):
        sel = n_tiles >= tiles if tiles == main_tiles else n_tiles == tiles
        if tiles == min_tiles:
            sel = jnp.logical_and(active, n_tiles <= tiles)

        @pl.when(sel)
        def _():
            expert_ffn(slice(0, tiles * LANES))

    @pl.when(active)
    def _():
        extra(lambda first, width, dst: expert_ffn(dst))

    @pl.when(jnp.logical_and(active, f == nf - 1))
    def _():
        scatter(0.0, main_rows, slice(0, main_rows))
        extra(scatter)

    @pl.when(jnp.logical_and(e == ne - 1, f == nf - 1))
    def _():
        x = x_ref[...]
        is_ctx = _row_is_ctx(j, tb, ctx_len, x.shape)
        o_ref[...] = x + _sel_mod(mod_ref, is_ctx, 5) * o_ref[...]


def _moe_sparse(hb16, x_all, mods_i, gate, wg, wu, wd, ctx_len):
    b, s, d = x_all.shape
    ne, _, f = wg.shape
    tb = _pick_tile(s, (1088, 768, 512, 256))
    max_tiles = -(-tb // LANES)
    ch = max_tiles * LANES
    mean_tiles = -(-tb * TOP_K // (ne * LANES))
    min_tiles, main_tiles = max(mean_tiles - 1, 1), min(mean_tiles + 1, max_tiles)
    rb = max(r for r in range(SUBLANES, 257, SUBLANES) if tb % r == 0)
    fc = _pick_tile(f, (512, 256, 128))
    nf = f // fc
    row = lambda bi, j, e, k: (bi, j, 0)
    return pl.pallas_call(
        functools.partial(_moe_sparse_kernel, tb=tb, min_tiles=min_tiles, main_tiles=main_tiles, rb=rb,
                          ctx_len=ctx_len, ne=ne, nf=nf),
        grid=(b, s // tb, ne, nf),
        in_specs=[
            pl.BlockSpec((None, tb, d), row),
            pl.BlockSpec((None, tb, d), row, pipeline_mode=pl.Buffered(1)),
            pl.BlockSpec((None, 2, 8, d), lambda bi, j, e, k: (bi, 0, 0, 0)),
            pl.BlockSpec((None, tb, LANES), row),
            pl.BlockSpec((None, d, fc), lambda bi, j, e, k: (e, 0, k)),
            pl.BlockSpec((None, d, fc), lambda bi, j, e, k: (e, 0, k)),
            pl.BlockSpec((None, fc, d), lambda bi, j, e, k: (e, k, 0)),
        ],
        out_specs=pl.BlockSpec((None, tb, d), row),
        out_shape=jax.ShapeDtypeStruct((b, s, d), F32),
        scratch_shapes=[pltpu.VMEM((tb, LANES), F32), pltpu.VMEM((SUBLANES, LANES), F32),
                        pltpu.VMEM((ch, d), BF16), pltpu.VMEM((ch, d), F32)],
        compiler_params=_cparams(("parallel", "parallel", "arbitrary", "arbitrary")),
        name="moe_experts",
    )(hb16, x_all, mods_i, gate, wg.astype(BF16), wu.astype(BF16), wd.astype(BF16))


def _final_norm_kernel(x_ref, g_ref, o_ref):
    x = x_ref[...]
    ms = jnp.mean(x * x, axis=-1, keepdims=True)
    o_ref[...] = (x * lax.rsqrt(ms + EPS)) * g_ref[...]


def _final_norm(x_all, g, ctx_len):
    b, s, d = x_all.shape
    seq = s - ctx_len
    tm = _pick_tile(math.gcd(seq, ctx_len), (512, 256, 128))
    off = ctx_len // tm
    return pl.pallas_call(
        _final_norm_kernel,
        grid=(b, seq // tm),
        in_specs=[pl.BlockSpec((None, tm, d), lambda bi, j: (bi, j + off, 0)), pl.BlockSpec((1, d), lambda bi, j: (0, 0))],
        out_specs=pl.BlockSpec((None, tm, d), lambda bi, j: (bi, j, 0)),
        out_shape=jax.ShapeDtypeStruct((b, seq, d), F32),
        compiler_params=_cparams(("parallel", "parallel")),
        name="final_norm",
    )(x_all, g.reshape(1, d))


def kernel(x, c, ctx, c_ctx, ada_w, ada_b, norm_mix, norm_ffn, norm_final,
           lru_w_in, lru_conv_w, lru_conv_b, lru_w_a, lru_b_a, lru_w_x, lru_b_x, lru_lambda, lru_w_out,
           attn_w_qkv, attn_sinks, attn_w_o,
           hy_w_in, hy_b_in, hy_conv_w, hy_conv_b, hy_f_w1, hy_f_b1, hy_f_w2, hy_f_b2, hy_f_w3, hy_f_b3,
           hy_f_freq, hy_f_w4, hy_skip, hy_w_out, hy_b_out,
           ffn_w_gate, ffn_w_up, ffn_w_down,
           moe_router, moe_w_gate, moe_w_up, moe_w_down):
    depth = ada_w.shape[0]
    ctx_len, seq, d = ctx.shape[1], x.shape[1], x.shape[2]
    x_all = jnp.concatenate([ctx, x], axis=1)
    mods = _mods(c, c_ctx, ada_w, ada_b)
    zero_d = jnp.zeros((d,), F32)
    for i in range(depth):
        j = i // N_MIXERS
        moe = i % 2 == 1
        h_dtype = F32 if moe else BF16
        if i % N_MIXERS == 0:
            gu = _norm_mm(x_all, mods[i], norm_mix[i], lru_w_in[j], jnp.zeros((lru_w_in.shape[2],), F32), ctx_len, F32)
            hs_f, hs_b = _lru_scan(gu, lru_conv_w[j], lru_conv_b[j], lru_w_a[j], lru_b_a[j], lru_w_x[j], lru_b_x[j],
                                   lru_lambda[j], ctx_len)
            x_all, h2 = _lru_mm_res(hs_f, hs_b, gu, x_all, mods[i], norm_ffn[i], lru_w_out[j], ctx_len, h_dtype)
        elif i % N_MIXERS == 1:
            cos_t, sin_t = _rope_tables(seq, ctx_len)
            qkv = _qkv_proj(x_all, mods[i], norm_mix[i], attn_w_qkv[j], cos_t, sin_t, ctx_len)
            o = _attention(qkv, attn_sinks[j], ctx_len)
            x_all, h2 = _mm_res(o, x_all, mods[i], norm_ffn[i], attn_w_o[j], zero_d, ctx_len, h_dtype)
        else:
            u0 = _norm_mm(x_all, mods[i], norm_mix[i], hy_w_in[j], hy_b_in[j], ctx_len, F32)
            z = _hyena_mix(u0, hy_conv_w[j], hy_conv_b[j], hy_f_w1[j], hy_f_b1[j], hy_f_w2[j], hy_f_b2[j],
                           hy_f_w3[j], hy_f_b3[j], hy_f_freq[j], hy_f_w4[j], hy_skip[j], ctx_len)
            x_all, h2 = _mm_res(z, x_all, mods[i], norm_ffn[i], hy_w_out[j], hy_b_out[j], ctx_len, h_dtype)
        if moe:
            gate, hb16 = _router(h2, moe_router[i // 2])
            x_all = _moe_sparse(hb16, x_all, mods[i], gate, moe_w_gate[i // 2], moe_w_up[i // 2], moe_w_down[i // 2],
                                ctx_len)
        else:
            x_all = _ffn_dense(h2, x_all, mods[i], ffn_w_gate[i // 2], ffn_w_up[i // 2], ffn_w_down[i // 2], ctx_len)
    return _final_norm(x_all, norm_final, ctx_len)
```

```python
import functools
import math

import jax
import jax.numpy as jnp
import numpy as np
from jax import lax
from jax.experimental import pallas as pl
from jax.experimental.pallas import tpu as pltpu

F32 = jnp.float32
BF16 = jnp.bfloat16
HIGHEST = lax.Precision.HIGHEST

N_MIXERS = 3
EPS = 1e-6
GRID_W = 64
RNN_BLOCKS = 8
LRU_C = 8.0
N_HEADS = 16
N_KV_HEADS = 4
HEAD_DIM = 64
WINDOW = 128
ROPE_BASE = 10000.0
FILTER_BANDS = 16
DECAY_TARGET = 1e-2
FAST_DECAY_PCT = 0.3
SLOW_DECAY_PCT = 1.5
N_EXPERTS = 8
TOP_K = 2

LANES = 128
SUBLANES = 8
VMEM_LIMIT = 56 * 1024 * 1024
FFT_N1 = 128


def _cparams(sem, vmem=VMEM_LIMIT):
    return pltpu.CompilerParams(dimension_semantics=sem, vmem_limit_bytes=vmem)


def _pick_tile(n, cands):
    for c in cands:
        if n % c == 0:
            return c
    raise ValueError(f"no tile for {n} in {cands}")


def _row_is_ctx(j, tm, ctx_len, shape):
    rows = j * tm + lax.broadcasted_iota(jnp.int32, shape, 0)
    return rows < ctx_len


def _sel_mod(mod_ref, is_ctx, idx):
    return jnp.where(is_ctx, mod_ref[0, idx:idx + 1, :], mod_ref[1, idx:idx + 1, :])


def _adanorm(x, g, shift, scale):
    ms = jnp.mean(x * x, axis=-1, keepdims=True)
    y = x * lax.rsqrt(ms + EPS)
    return (y * g) * (1.0 + scale) + shift


def _mods(c, c_ctx, ada_w, ada_b):
    depth, d, d6 = ada_w.shape
    b = c.shape[0]
    rows = ((b + 1 + SUBLANES - 1) // SUBLANES) * SUBLANES
    cond = jnp.concatenate([c, c_ctx[None, :], jnp.zeros((rows - b - 1, d), F32)], axis=0)
    tn = _pick_tile(d6, (1024, 512, 256, 128))

    def kern(c_ref, w_ref, b_ref, o_ref):
        x = c_ref[...]
        s = x * jax.nn.sigmoid(x)
        o_ref[...] = jnp.dot(s, w_ref[...], preferred_element_type=F32, precision=HIGHEST) + b_ref[...]

    out = pl.pallas_call(
        kern,
        grid=(depth, d6 // tn),
        in_specs=[
            pl.BlockSpec((rows, d), lambda i, n: (0, 0)),
            pl.BlockSpec((None, d, tn), lambda i, n: (i, 0, n)),
            pl.BlockSpec((None, 1, tn), lambda i, n: (i, 0, n)),
        ],
        out_specs=pl.BlockSpec((None, rows, tn), lambda i, n: (i, 0, n)),
        out_shape=jax.ShapeDtypeStruct((depth, rows, d6), F32),
        compiler_params=_cparams(("parallel", "parallel")),
        name="mods",
    )(cond, ada_w, ada_b.reshape(depth, 1, d6))
    m = out.reshape(depth, rows, 6, d)
    lat = m[:, :b]
    ctx = jnp.broadcast_to(m[:, b:b + 1], lat.shape)
    both = jnp.stack([ctx, lat], axis=2)
    return jnp.pad(both, ((0, 0), (0, 0), (0, 0), (0, 2), (0, 0)))


def _norm_mm_kernel(x_ref, mod_ref, g_ref, w_ref, b_ref, o_ref, *, tm, ctx_len):
    j = pl.program_id(1)
    x = x_ref[...]
    is_ctx = _row_is_ctx(j, tm, ctx_len, x.shape)
    h = _adanorm(x, g_ref[...], _sel_mod(mod_ref, is_ctx, 0), _sel_mod(mod_ref, is_ctx, 1))
    acc = jnp.dot(h.astype(BF16), w_ref[...], preferred_element_type=F32) + b_ref[...]
    o_ref[...] = acc.astype(o_ref.dtype)


def _norm_mm(x_all, mods_i, g, w, bias, ctx_len, out_dtype):
    b, s, d = x_all.shape
    n = w.shape[1]
    tm = _pick_tile(s, (544, 384, 256, 128))
    return pl.pallas_call(
        functools.partial(_norm_mm_kernel, tm=tm, ctx_len=ctx_len),
        grid=(b, s // tm),
        in_specs=[
            pl.BlockSpec((None, tm, d), lambda bi, j: (bi, j, 0)),
            pl.BlockSpec((None, 2, 8, d), lambda bi, j: (bi, 0, 0, 0)),
            pl.BlockSpec((1, d), lambda bi, j: (0, 0)),
            pl.BlockSpec((d, n), lambda bi, j: (0, 0)),
            pl.BlockSpec((1, n), lambda bi, j: (0, 0)),
        ],
        out_specs=pl.BlockSpec((None, tm, n), lambda bi, j: (bi, j, 0)),
        out_shape=jax.ShapeDtypeStruct((b, s, n), out_dtype),
        compiler_params=_cparams(("parallel", "parallel")),
        name="norm_mm",
    )(x_all, mods_i, g.reshape(1, d), w.astype(BF16), bias.reshape(1, n).astype(F32))


def _mm_res_body(y_bf16, x_ref, mod_ref, g_ref, w_ref, b_ref, xo_ref, ho_ref, tm, ctx_len):
    j = pl.program_id(1)
    x = x_ref[...]
    is_ctx = _row_is_ctx(j, tm, ctx_len, x.shape)
    acc = jnp.dot(y_bf16, w_ref[...], preferred_element_type=F32) + b_ref[...]
    x_new = x + _sel_mod(mod_ref, is_ctx, 2) * acc
    xo_ref[...] = x_new
    h2 = _adanorm(x_new, g_ref[...], _sel_mod(mod_ref, is_ctx, 3), _sel_mod(mod_ref, is_ctx, 4))
    ho_ref[...] = h2.astype(ho_ref.dtype)


def _mm_res_kernel(y_ref, x_ref, mod_ref, g_ref, w_ref, b_ref, xo_ref, ho_ref, *, tm, ctx_len):
    _mm_res_body(y_ref[...].astype(BF16), x_ref, mod_ref, g_ref, w_ref, b_ref, xo_ref, ho_ref, tm, ctx_len)


def _lru_mm_res_kernel(hf_ref, hb_ref, gate_ref, x_ref, mod_ref, g_ref, w_ref, b_ref, xo_ref, ho_ref, *, tm, ctx_len):
    y = (hf_ref[...] + hb_ref[...]) * jax.nn.gelu(gate_ref[...])
    _mm_res_body(y.astype(BF16), x_ref, mod_ref, g_ref, w_ref, b_ref, xo_ref, ho_ref, tm, ctx_len)


def _mm_res_call(kern, ys, y_specs, x_all, mods_i, g_ffn, w, bias, ctx_len, tm, h_dtype):
    b, s, d = x_all.shape
    k = w.shape[0]
    return pl.pallas_call(
        functools.partial(kern, tm=tm, ctx_len=ctx_len),
        grid=(b, s // tm),
        in_specs=y_specs + [
            pl.BlockSpec((None, tm, d), lambda bi, j: (bi, j, 0)),
            pl.BlockSpec((None, 2, 8, d), lambda bi, j: (bi, 0, 0, 0)),
            pl.BlockSpec((1, d), lambda bi, j: (0, 0)),
            pl.BlockSpec((k, d), lambda bi, j: (0, 0)),
            pl.BlockSpec((1, d), lambda bi, j: (0, 0)),
        ],
        out_specs=[
            pl.BlockSpec((None, tm, d), lambda bi, j: (bi, j, 0)),
            pl.BlockSpec((None, tm, d), lambda bi, j: (bi, j, 0)),
        ],
        out_shape=[jax.ShapeDtypeStruct((b, s, d), F32), jax.ShapeDtypeStruct((b, s, d), h_dtype)],
        compiler_params=_cparams(("parallel", "parallel")),
        name="mm_res",
    )(*ys, x_all, mods_i, g_ffn.reshape(1, d), w.astype(BF16), bias.reshape(1, d).astype(F32))


def _mm_res(y, x_all, mods_i, g_ffn, w, bias, ctx_len, h_dtype=BF16):
    b, s, d = x_all.shape
    k = w.shape[0]
    tm = _pick_tile(s, (544, 384, 256, 128))
    spec = [pl.BlockSpec((None, tm, k), lambda bi, j: (bi, j, 0))]
    return _mm_res_call(_mm_res_kernel, [y], spec, x_all, mods_i, g_ffn, w, bias, ctx_len, tm, h_dtype)


def _lru_mm_res(hs_f, hs_b, gu, x_all, mods_i, g_ffn, w, ctx_len, h_dtype=BF16):
    b, s, d = x_all.shape
    k = w.shape[0]
    tm = _pick_tile(s, (544, 384, 256, 128))
    specs = [
        pl.BlockSpec((None, tm, k), lambda bi, j: (bi, j, 0)),
        pl.BlockSpec((None, tm, k), lambda bi, j: (bi, j, 0)),
        pl.BlockSpec((None, tm, k), lambda bi, j: (bi, j, 0)),
    ]
    return _mm_res_call(_lru_mm_res_kernel, [hs_f, hs_b, gu], specs, x_all, mods_i, g_ffn, w,
                        jnp.zeros((d,), F32), ctx_len, tm, h_dtype)


def _ffn_kernel(h_ref, x_ref, mod_ref, wg_ref, wu_ref, wd_ref, o_ref, *, tm, ctx_len):
    j = pl.program_id(1)
    h = h_ref[...]
    a = jnp.dot(h, wg_ref[...], preferred_element_type=F32)
    u = jnp.dot(h, wu_ref[...], preferred_element_type=F32)
    act = (a * jax.nn.sigmoid(a) * u).astype(BF16)
    y = jnp.dot(act, wd_ref[...], preferred_element_type=F32)
    x = x_ref[...]
    is_ctx = _row_is_ctx(j, tm, ctx_len, x.shape)
    o_ref[...] = x + _sel_mod(mod_ref, is_ctx, 5) * y


def _ffn_dense(h2, x_all, mods_i, wg, wu, wd, ctx_len):
    b, s, d = x_all.shape
    f = wg.shape[1]
    tm = _pick_tile(s, (544, 384, 256, 128))
    resident = dict(pipeline_mode=pl.Buffered(1))
    return pl.pallas_call(
        functools.partial(_ffn_kernel, tm=tm, ctx_len=ctx_len),
        grid=(b, s // tm),
        in_specs=[
            pl.BlockSpec((None, tm, d), lambda bi, j: (bi, j, 0)),
            pl.BlockSpec((None, tm, d), lambda bi, j: (bi, j, 0)),
            pl.BlockSpec((None, 2, 8, d), lambda bi, j: (bi, 0, 0, 0)),
            pl.BlockSpec((d, f), lambda bi, j: (0, 0), **resident),
            pl.BlockSpec((d, f), lambda bi, j: (0, 0), **resident),
            pl.BlockSpec((f, d), lambda bi, j: (0, 0), **resident),
        ],
        out_specs=pl.BlockSpec((None, tm, d), lambda bi, j: (bi, j, 0)),
        out_shape=jax.ShapeDtypeStruct((b, s, d), F32),
        compiler_params=_cparams(("parallel", "parallel")),
        name="ffn_dense",
    )(h2, x_all, mods_i, wg.astype(BF16), wu.astype(BF16), wd.astype(BF16))


def _lru_bwd_tile(j, nt, nct):
    return jnp.where(j < nct, nct - 1 - j, nt - 1 - (j - nct))


def _lru_scan_kernel(um_f, up_f, un_f, um_b, up_b, un_b, cw_ref, cb_ref, wax_f, wax_b, ba_ref, bx_ref, lam_ref,
                     hsf_ref, hsb_ref, ext, af_s, bf_s, ab_s, bb_s, carry, *, tt, nt, nct, wl, nb):
    j = pl.program_id(1)

    @pl.when(j == 0)
    def _():
        carry[...] = jnp.zeros_like(carry)

    def coeffs(um, up, un, t, d, wax_ref, a_s, b_s):
        prev_ok = jnp.logical_and(t != 0, t != nct)
        next_ok = jnp.logical_and(t != nct - 1, t != nt - 1)
        ext[:, SUBLANES:SUBLANES + tt, :] = um[...]
        ext[:, 0:SUBLANES, :] = jnp.where(prev_ok, up[...], 0.0)
        ext[:, SUBLANES + tt:2 * SUBLANES + tt, :] = jnp.where(next_ok, un[...], 0.0)
        uc = cb_ref[...].reshape(1, 1, wl)
        for k in range(4):
            uc = uc + cw_ref[k:k + 1, :].reshape(1, 1, wl) * ext[:, SUBLANES - 2 + k:SUBLANES - 2 + k + tt, :]
        sp = jax.nn.softplus(-lam_ref[d:d + 1, :])
        for hb in range(wl // LANES):
            sl = slice(hb * LANES, (hb + 1) * LANES)
            ub = uc[:, :, sl].reshape(nb * tt, LANES)
            pre = jnp.dot(ub.astype(BF16), wax_ref[hb], preferred_element_type=F32)
            r = jax.nn.sigmoid(pre[:, :LANES] + ba_ref[d:d + 1, sl])
            i = jax.nn.sigmoid(pre[:, LANES:] + bx_ref[d:d + 1, sl])
            log_a = (-LRU_C) * r * sp[:, sl]
            a = jnp.exp(log_a)
            b = jnp.sqrt(-jnp.tanh(log_a) * (a * a + 1.0)) * (i * ub)
            a_s[:, :, sl] = a.reshape(nb, tt, LANES)
            b_s[:, :, sl] = b.reshape(nb, tt, LANES)

    tb = _lru_bwd_tile(j, nt, nct)
    coeffs(um_f, up_f, un_f, j, 0, wax_f, af_s, bf_s)
    coeffs(um_b, up_b, un_b, tb, 1, wax_b, ab_s, bb_s)

    def step(t, c):
        hf, hb = c
        hf = af_s[:, pl.ds(t, 1), :].reshape(nb, wl) * hf + bf_s[:, pl.ds(t, 1), :].reshape(nb, wl)
        hsf_ref[:, pl.ds(t, 1), :] = hf.reshape(nb, 1, wl)
        r = tt - 1 - t
        hb = ab_s[:, pl.ds(r, 1), :].reshape(nb, wl) * hb + bb_s[:, pl.ds(r, 1), :].reshape(nb, wl)
        hsb_ref[:, pl.ds(r, 1), :] = hb.reshape(nb, 1, wl)
        return hf, hb

    hf, hb = lax.fori_loop(0, tt, step, (carry[0], carry[1]), unroll=8)
    carry[0] = hf
    carry[1] = hb


def _lru_scan(gu, conv_w, conv_b, w_a, b_a, w_x, b_x, lam, ctx_len):
    nb, s, d2 = gu.shape
    d = d2 // 2
    tt = 128
    assert s % tt == 0 and ctx_len % tt == 0 and d % (RNN_BLOCKS * LANES) == 0 and d // RNN_BLOCKS == LANES
    nt, nct = s // tt, ctx_len // tt
    wl = 4 * LANES
    ncg = d // wl
    coff = d // wl
    r8 = tt // SUBLANES
    last8 = s // SUBLANES - 1
    wax = jnp.concatenate([w_a, w_x], axis=-1).astype(BF16)

    def main_f(cg, j): return (0, j, coff + cg)
    def prev_f(cg, j): return (0, jnp.maximum(j * r8 - 1, 0), coff + cg)
    def next_f(cg, j): return (0, jnp.minimum((j + 1) * r8, last8), coff + cg)
    def tb_(j): return _lru_bwd_tile(j, nt, nct)
    def main_b(cg, j): return (0, tb_(j), coff + cg)
    def prev_b(cg, j): return (0, jnp.maximum(tb_(j) * r8 - 1, 0), coff + cg)
    def next_b(cg, j): return (0, jnp.minimum((tb_(j) + 1) * r8, last8), coff + cg)

    big = lambda im: pl.BlockSpec((nb, tt, wl), im)
    halo = lambda im: pl.BlockSpec((nb, SUBLANES, wl), im)
    vec2 = pl.BlockSpec((2, wl), lambda cg, j: (0, cg))
    return pl.pallas_call(
        functools.partial(_lru_scan_kernel, tt=tt, nt=nt, nct=nct, wl=wl, nb=nb),
        grid=(ncg, nt),
        in_specs=[
            big(main_f), halo(prev_f), halo(next_f), big(main_b), halo(prev_b), halo(next_b),
            pl.BlockSpec((4, wl), lambda cg, j: (0, cg)),
            pl.BlockSpec((1, wl), lambda cg, j: (0, cg)),
            pl.BlockSpec((None, wl // LANES, LANES, 2 * LANES), lambda cg, j: (0, cg, 0, 0)),
            pl.BlockSpec((None, wl // LANES, LANES, 2 * LANES), lambda cg, j: (1, cg, 0, 0)),
            vec2, vec2, vec2,
        ],
        out_specs=[
            pl.BlockSpec((nb, tt, wl), lambda cg, j: (0, j, cg)),
            pl.BlockSpec((nb, tt, wl), lambda cg, j: (0, tb_(j), cg)),
        ],
        out_shape=[jax.ShapeDtypeStruct((nb, s, d), F32), jax.ShapeDtypeStruct((nb, s, d), F32)],
        scratch_shapes=[
            pltpu.VMEM((nb, tt + 2 * SUBLANES, wl), F32),
            pltpu.VMEM((nb, tt, wl), F32), pltpu.VMEM((nb, tt, wl), F32),
            pltpu.VMEM((nb, tt, wl), F32), pltpu.VMEM((nb, tt, wl), F32),
            pltpu.VMEM((2, nb, wl), F32),
        ],
        compiler_params=_cparams(("parallel", "arbitrary")),
        name="lru_scan",
    )(gu, gu, gu, gu, gu, gu, conv_w, conv_b.reshape(1, d), wax, wax, b_a, b_x, lam)


def _rope_tables(seq, ctx_len):
    q = HEAD_DIM // 4
    inv_freq = ROPE_BASE ** (-jnp.arange(q, dtype=F32) / q)
    pos = jnp.arange(seq, dtype=jnp.int32)
    row = (pos // GRID_W).astype(F32)[:, None] * inv_freq
    col = (pos % GRID_W).astype(F32)[:, None] * inv_freq
    ang = jnp.concatenate([row, row, col, col], axis=-1)
    sign = jnp.concatenate([-jnp.ones((q,), F32), jnp.ones((q,), F32)] * 2)
    cos = jnp.concatenate([jnp.ones((ctx_len, HEAD_DIM), F32), jnp.cos(ang)], axis=0)
    sin = jnp.concatenate([jnp.zeros((ctx_len, HEAD_DIM), F32), jnp.sin(ang) * sign], axis=0)
    return jnp.tile(cos, (1, 2)), jnp.tile(sin, (1, 2))


def _qkv_kernel(x_ref, mod_ref, g_ref, w_ref, cos_ref, sin_ref, o_ref, *, tm, ctx_len, n_q, n_rope):
    j = pl.program_id(1)
    x = x_ref[...]
    is_ctx = _row_is_ctx(j, tm, ctx_len, x.shape)
    h = _adanorm(x, g_ref[...], _sel_mod(mod_ref, is_ctx, 0), _sel_mod(mod_ref, is_ctx, 1))
    acc = jnp.dot(h.astype(BF16), w_ref[...], preferred_element_type=F32)
    cos = cos_ref[...]
    sin = sin_ref[...]
    lane = lax.broadcasted_iota(jnp.int32, (tm, LANES), 1)
    first_half = (lane % (HEAD_DIM // 2)) < (HEAD_DIM // 4)
    q_scale = HEAD_DIM ** -0.5
    for c in range(acc.shape[1] // LANES):
        v = acc[:, c * LANES:(c + 1) * LANES]
        if c < n_rope:
            partner = jnp.where(first_half, pltpu.roll(v, LANES - HEAD_DIM // 4, 1), pltpu.roll(v, HEAD_DIM // 4, 1))
            v = v * cos + partner * sin
            if c < n_q:
                v = v * q_scale
        o_ref[:, c * LANES:(c + 1) * LANES] = v.astype(o_ref.dtype)


def _qkv_proj(x_all, mods_i, g, w, cos_t, sin_t, ctx_len):
    b, s, d = x_all.shape
    n = w.shape[1]
    tm = _pick_tile(s, (544, 384, 256, 128))
    n_q = N_HEADS * HEAD_DIM // LANES
    n_rope = (N_HEADS + N_KV_HEADS) * HEAD_DIM // LANES
    return pl.pallas_call(
        functools.partial(_qkv_kernel, tm=tm, ctx_len=ctx_len, n_q=n_q, n_rope=n_rope),
        grid=(b, s // tm),
        in_specs=[
            pl.BlockSpec((None, tm, d), lambda bi, j: (bi, j, 0)),
            pl.BlockSpec((None, 2, 8, d), lambda bi, j: (bi, 0, 0, 0)),
            pl.BlockSpec((1, d), lambda bi, j: (0, 0)),
            pl.BlockSpec((d, n), lambda bi, j: (0, 0)),
            pl.BlockSpec((tm, LANES), lambda bi, j: (j, 0)),
            pl.BlockSpec((tm, LANES), lambda bi, j: (j, 0)),
        ],
        out_specs=pl.BlockSpec((None, tm, n), lambda bi, j: (bi, j, 0)),
        out_shape=jax.ShapeDtypeStruct((b, s, n), BF16),
        compiler_params=_cparams(("parallel", "parallel")),
        name="qkv_proj",
    )(x_all, mods_i, g.reshape(1, d), w.astype(BF16), cos_t, sin_t)


def _attn_kernel(q_ref, kp_ref, km_ref, kn_ref, kc_ref, vp_ref, vm_ref, vn_ref, vc_ref, sink_ref, o_ref,
                 *, tq, nct, seq, ctx_len):
    j = pl.program_id(1)
    k_all = jnp.concatenate([kp_ref[...], km_ref[...], kn_ref[...], kc_ref[...]], axis=0)
    v_all = jnp.concatenate([vp_ref[...], vm_ref[...], vn_ref[...], vc_ref[...]], axis=0)
    n_loc = 2 * tq
    n_keys = n_loc + ctx_len
    row = lax.broadcasted_iota(jnp.int32, (tq, n_keys), 0)
    col = lax.broadcasted_iota(jnp.int32, (tq, n_keys), 1)
    p0 = (j - nct) * tq
    kpos = p0 - tq // 2 + col
    band = jnp.abs(row - (col - tq // 2)) <= WINDOW
    in_seq = jnp.logical_and(kpos >= 0, kpos < seq)
    valid_loc = jnp.logical_and(jnp.logical_and(band, in_seq), j >= nct)
    valid = jnp.logical_or(col >= n_loc, valid_loc)
    group = N_HEADS // N_KV_HEADS
    outs = []
    for h in range(N_HEADS):
        kvh = h // group
        qh = q_ref[:, h * HEAD_DIM:(h + 1) * HEAD_DIM]
        kk = k_all[:, kvh * HEAD_DIM:(kvh + 1) * HEAD_DIM]
        vv = v_all[:, kvh * HEAD_DIM:(kvh + 1) * HEAD_DIM]
        s = lax.dot_general(qh, kk, (((1,), (1,)), ((), ())), preferred_element_type=F32)
        s = jnp.where(valid, s, -1e30)
        sink = sink_ref[h:h + 1, 0:1]
        m = jnp.maximum(jnp.max(s, axis=1, keepdims=True), sink)
        p = jnp.exp(s - m)
        denom = jnp.sum(p, axis=1, keepdims=True) + jnp.exp(sink - m)
        o = jnp.dot(p.astype(BF16), vv, preferred_element_type=F32)
        outs.append((o / denom).astype(o_ref.dtype))
    o_ref[...] = jnp.concatenate(outs, axis=1)


def _attention(qkv, sinks, ctx_len):
    b, s, _ = qkv.shape
    seq = s - ctx_len
    tq = 2 * WINDOW
    half = tq // 2
    assert ctx_len % tq == 0 and seq % tq == 0
    nct = ctx_len // tq
    dq = N_HEADS * HEAD_DIM
    dkv = N_KV_HEADS * HEAD_DIM
    kcol, vcol = dq // dkv, dq // dkv + 1
    last_half = s // half - 1
    sink_b = jnp.broadcast_to(sinks.astype(F32)[:, None], (N_HEADS, LANES))

    def prev(col): return lambda bi, j: (bi, jnp.maximum(2 * j - 1, 0), col)
    def main(col): return lambda bi, j: (bi, j, col)
    def nxt(col): return lambda bi, j: (bi, jnp.minimum(2 * j + 2, last_half), col)
    def ctx(col): return lambda bi, j: (bi, 0, col)
    kv_specs = lambda col: [
        pl.BlockSpec((None, half, dkv), prev(col)), pl.BlockSpec((None, tq, dkv), main(col)),
        pl.BlockSpec((None, half, dkv), nxt(col)), pl.BlockSpec((None, ctx_len, dkv), ctx(col)),
    ]
    return pl.pallas_call(
        functools.partial(_attn_kernel, tq=tq, nct=nct, seq=seq, ctx_len=ctx_len),
        grid=(b, s // tq),
        in_specs=[pl.BlockSpec((None, tq, dq), lambda bi, j: (bi, j, 0))] + kv_specs(kcol) + kv_specs(vcol)
        + [pl.BlockSpec((N_HEADS, LANES), lambda bi, j: (0, 0))],
        out_specs=pl.BlockSpec((None, tq, dq), lambda bi, j: (bi, j, 0)),
        out_shape=jax.ShapeDtypeStruct((b, s, dq), BF16),
        compiler_params=_cparams(("parallel", "parallel")),
        name="swa_attention",
    )(qkv, *([qkv] * 8), sink_b)


def _dwconv3_kernel(um, up, un, cw_ref, cb_ref, g1_ref, g2_ref, v_ref, ext, *, tt, nt, nct, c):
    t = pl.program_id(1)
    prev_ok = jnp.logical_and(t != 0, t != nct)
    next_ok = jnp.logical_and(t != nct - 1, t != nt - 1)
    ext[SUBLANES:SUBLANES + tt, :] = um[...]
    ext[0:SUBLANES, :] = jnp.where(prev_ok, up[...], 0.0)
    ext[SUBLANES + tt:2 * SUBLANES + tt, :] = jnp.where(next_ok, un[...], 0.0)
    y = cb_ref[...]
    for k in range(3):
        y = y + cw_ref[k:k + 1, :] * ext[SUBLANES - 1 + k:SUBLANES - 1 + k + tt, :]
    g1_ref[...] = y[:, 0:c]
    g2_ref[...] = y[:, c:2 * c]
    v_ref[...] = y[:, 2 * c:3 * c]


def _dwconv3(u0, conv_w, conv_b, ctx_len):
    b, s, c3 = u0.shape
    c = c3 // 3
    tt = 256
    assert s % tt == 0 and ctx_len % tt == 0
    nt, nct = s // tt, ctx_len // tt
    r8 = tt // SUBLANES
    last8 = s // SUBLANES - 1
    out = jax.ShapeDtypeStruct((b, s, c), F32)
    ospec = pl.BlockSpec((None, tt, c), lambda bi, t: (bi, t, 0))
    return pl.pallas_call(
        functools.partial(_dwconv3_kernel, tt=tt, nt=nt, nct=nct, c=c),
        grid=(b, nt),
        in_specs=[
            pl.BlockSpec((None, tt, c3), lambda bi, t: (bi, t, 0)),
            pl.BlockSpec((None, SUBLANES, c3), lambda bi, t: (bi, jnp.maximum(t * r8 - 1, 0), 0)),
            pl.BlockSpec((None, SUBLANES, c3), lambda bi, t: (bi, jnp.minimum((t + 1) * r8, last8), 0)),
            pl.BlockSpec((3, c3), lambda bi, t: (0, 0)),
            pl.BlockSpec((1, c3), lambda bi, t: (0, 0)),
        ],
        out_specs=[ospec, ospec, ospec],
        out_shape=[out, out, out],
        scratch_shapes=[pltpu.VMEM((tt + 2 * SUBLANES, c3), F32)],
        compiler_params=_cparams(("parallel", "parallel")),
        name="hyena_dwconv",
    )(u0, u0, u0, conv_w, conv_b.reshape(1, c3))


def _filter_features(seq, ctx_len):
    def feats(length):
        t = jnp.linspace(0.0, 1.0, length, dtype=F32)[:, None]
        omega = (2.0 * math.pi / length) * jnp.arange(length, dtype=F32)[:, None]
        bands = jnp.linspace(1e-4, FILTER_BANDS - 1, FILTER_BANDS, dtype=F32)[None, :]
        return jnp.concatenate([t, jnp.cos(bands * omega), -jnp.sin(bands * omega)], axis=-1)
    z = jnp.concatenate([feats(ctx_len), feats(seq)], axis=0)
    return jnp.pad(z, ((0, 0), (0, 40 - z.shape[1])))


def _filter_kernel(z_ref, w1_ref, b1_ref, w2_ref, b2_ref, w3_ref, b3_ref, fr_ref, w4_ref, dl_ref, hw_ref, sum_ref,
                   *, tm, ctx_len, c):
    j = pl.program_id(0)

    @pl.when(j == 0)
    def _():
        sum_ref[...] = jnp.zeros_like(sum_ref)

    z = z_ref[...]
    fr = fr_ref[...]
    dot = functools.partial(jnp.dot, preferred_element_type=F32, precision=HIGHEST)
    hid = jnp.sin(fr * (dot(z, w1_ref[...]) + b1_ref[...]))
    hid = jnp.sin(fr * (dot(hid, w2_ref[...]) + b2_ref[...]))
    hid = jnp.sin(fr * (dot(hid, w3_ref[...]) + b3_ref[...]))
    h = dot(hid, w4_ref[...])
    window = jnp.exp(-z[:, 0:1] * jnp.abs(dl_ref[...]))
    rows = j * tm + lax.broadcasted_iota(jnp.int32, (tm, c), 0)
    first = jnp.logical_or(rows == 0, rows == ctx_len)
    is_ctx = j * tm < ctx_len
    srow = jnp.where(is_ctx, 0, 1)
    for q in range(4):
        hq = h[:, q * c:(q + 1) * c] * window
        if q % 2 == 1:
            hq = jnp.where(first, 0.0, hq)
        hw_ref[:, q * c:(q + 1) * c] = hq
        part = jnp.sum(jnp.abs(hq), axis=0, keepdims=True)
        for r in range(2):
            sum_ref[r:r + 1, q * c:(q + 1) * c] += jnp.where(srow == r, part, 0.0)


def _hyena_filters(seq, ctx_len, w1, b1, w2, b2, w3, b3, freq, w4, c):
    s = seq + ctx_len
    z = _filter_features(seq, ctx_len)
    tm = 256
    assert ctx_len % tm == 0 and seq % tm == 0
    fw = w1.shape[1]
    w1p = jnp.pad(w1.astype(F32), ((0, 40 - w1.shape[0]), (0, 0)))
    deltas = jnp.linspace(math.log(DECAY_TARGET) / SLOW_DECAY_PCT, math.log(DECAY_TARGET) / FAST_DECAY_PCT, c,
                          dtype=F32)[None, :]
    full = lambda shape: pl.BlockSpec(shape, lambda j: tuple(0 for _ in shape))
    return pl.pallas_call(
        functools.partial(_filter_kernel, tm=tm, ctx_len=ctx_len, c=c),
        grid=(s // tm,),
        in_specs=[
            pl.BlockSpec((tm, 40), lambda j: (j, 0)),
            full((40, fw)), full((1, fw)), full((fw, fw)), full((1, fw)), full((fw, fw)), full((1, fw)),
            full((1, fw)), full((fw, 4 * c)), full((1, c)),
        ],
        out_specs=[pl.BlockSpec((tm, 4 * c), lambda j: (j, 0)), pl.BlockSpec((SUBLANES, 4 * c), lambda j: (0, 0))],
        out_shape=[jax.ShapeDtypeStruct((s, 4 * c), F32), jax.ShapeDtypeStruct((SUBLANES, 4 * c), F32)],
        compiler_params=_cparams(("arbitrary",)),
        name="hyena_filter_mlp",
    )(z, w1p, b1.reshape(1, fw), w2, b2.reshape(1, fw), w3, b3.reshape(1, fw), freq.reshape(1, fw), w4, deltas)


def _stack_hl(m_r, m_i):
    rows = m_r.shape[0]
    g = -(-rows // SUBLANES) * SUBLANES
    pad = lambda a: jnp.pad(a, ((0, g - rows), (0, 0)))
    hi = lambda a: pad(a.astype(BF16))
    lo = lambda a: pad((a - a.astype(BF16).astype(F32)).astype(BF16))
    hl = jnp.concatenate([hi(m_r), hi(m_i), lo(m_r), lo(m_i)], axis=0)
    return hl, hl[:2 * g]


def _fft_tables(seq, ctx_len):
    n1 = FFT_N1
    rc, rl = ctx_len // n1, seq // n1
    n2c, n2l = 2 * rc, 2 * rl
    k2n = n2l + n2c
    r = rc + rl

    def cis(num, den):
        ang = (-2.0 * math.pi / den) * (num % den).astype(F32)
        return jnp.cos(ang), jnp.sin(ang)

    ar = jnp.arange
    fl_r, fl_i = cis(ar(n2l)[:, None] * ar(rl)[None, :], n2l)
    fc_r, fc_i = cis(ar(n2c)[:, None] * ar(rc)[None, :], n2c)
    zl, zc = jnp.zeros((n2l, rc), F32), jnp.zeros((n2c, rl), F32)
    f_r = jnp.concatenate([jnp.concatenate([zl, fl_r], 1), jnp.concatenate([fc_r, zc], 1)], 0)
    f_i = jnp.concatenate([jnp.concatenate([zl, fl_i], 1), jnp.concatenate([fc_i, zc], 1)], 0)
    scale = jnp.concatenate([jnp.full((n2l,), 1.0 / (n1 * n2l), F32), jnp.full((n2c,), 1.0 / (n1 * n2c), F32)])
    g_r, g_i = (f_r * scale[:, None]).T, (f_i * scale[:, None]).T
    tl_r, tl_i = cis(ar(n2l)[:, None] * ar(n1)[None, :], n1 * n2l)
    tc_r, tc_i = cis(ar(n2c)[:, None] * ar(n1)[None, :], n1 * n2c)
    tw_r, tw_i = jnp.concatenate([tl_r, tc_r], 0), jnp.concatenate([tl_i, tc_i], 0)
    bl = lambda a: jnp.broadcast_to(a[:, :, None], a.shape + (LANES,))
    f1_r, f1_i = cis(ar(n1)[:, None] * ar(n1)[None, :], n1)
    f1_hl, f1_h = _stack_hl(f1_r, f1_i)
    return dict(f_ri=_stack_ri(f_r, f_i), g_ri=_stack_ri(g_r, g_i), f1_hl=f1_hl, f1_h=f1_h,
                ta_r=bl(tw_r.T), ta_i=bl(tw_i.T),
                tb_r=bl(tw_r), tb_i=bl(tw_i),
                k2n=k2n, n2l=n2l, r=r)


def _stack_ri(m_r, m_i):
    rows = m_r.shape[0]
    g = -(-rows // SUBLANES) * SUBLANES
    pad = lambda a: jnp.pad(a, ((0, g - rows), (0, 0)))
    return jnp.concatenate([pad(m_r), pad(m_i)], axis=0)


def _cdot6(m_ref, x, rows):
    s = jnp.dot(m_ref[...], x, preferred_element_type=F32, precision=HIGHEST)
    g = m_ref.shape[0] // 2
    return s[0:rows], s[g:g + rows]


def _cdot3(fhl_ref, fh_ref, x, rows):
    x_hi = x.astype(BF16)
    x_lo = (x - x_hi.astype(F32)).astype(BF16)
    s = jnp.dot(fhl_ref[...], x_hi, preferred_element_type=F32)
    t = jnp.dot(fh_ref[...], x_lo, preferred_element_type=F32)
    g = fh_ref.shape[0] // 2
    return (s[0:rows] + s[2 * g:2 * g + rows] + t[0:rows],
            s[g:g + rows] + s[3 * g:3 * g + rows] + t[g:g + rows])


FFT_NB1 = 8
FFT_CC = 1024


def _rows(ref, lead, n):
    blk = ref[lead + (slice(None), slice(n, n + 1), slice(None))]
    return blk.reshape(blk.shape[0], blk.shape[2])


def _fft_a_kernel(x_ref, f_ref, tr_ref, ti_ref, o_ref, *, packed, cc, k2n):
    for n in range(FFT_NB1):
        if packed:
            rr, ir = _cdot6(f_ref, _rows(x_ref, (0,), n), k2n)
            ri, ii = _cdot6(f_ref, _rows(x_ref, (1,), n), k2n)
            a_r, a_i = rr - ii, ir + ri
        else:
            a_r, a_i = _cdot6(f_ref, _rows(x_ref, (), n), k2n)
        tr, ti = tr_ref[n], ti_ref[n]
        o_r, o_i = [], []
        for l in range(cc // LANES):
            cr, ci = a_r[:, l * LANES:(l + 1) * LANES], a_i[:, l * LANES:(l + 1) * LANES]
            o_r.append(cr * tr - ci * ti)
            o_i.append(cr * ti + ci * tr)
        o_ref[0, :, n:n + 1, :] = jnp.concatenate(o_r, axis=1).reshape(k2n, 1, cc)
        o_ref[1, :, n:n + 1, :] = jnp.concatenate(o_i, axis=1).reshape(k2n, 1, cc)


def _fft_stage_a(xv, tb, packed, c):
    k2n, r = tb["k2n"], tb["r"]
    cc = min(c, FFT_CC)
    nnb, ncc = FFT_N1 // FFT_NB1, c // cc
    mat = pl.BlockSpec(tb["f_ri"].shape, lambda *a: (0, 0))
    if packed:
        p = xv.shape[1]
        grid = (p, nnb, ncc)
        x_spec = pl.BlockSpec((2, None, r, FFT_NB1, cc), lambda pi, nb, cb: (0, pi, 0, nb, cb))
        t_spec = pl.BlockSpec((FFT_NB1, k2n, LANES), lambda pi, nb, cb: (nb, 0, 0))
        o_spec = pl.BlockSpec((None, 2, k2n, FFT_NB1, cc), lambda pi, nb, cb: (pi, 0, 0, nb, cb))
        o_shape = jax.ShapeDtypeStruct((p, 2, k2n, FFT_N1, c), F32)
        sem = ("parallel", "parallel", "parallel")
    else:
        grid = (nnb, ncc)
        x_spec = pl.BlockSpec((r, FFT_NB1, cc), lambda nb, cb: (0, nb, cb))
        t_spec = pl.BlockSpec((FFT_NB1, k2n, LANES), lambda nb, cb: (nb, 0, 0))
        o_spec = pl.BlockSpec((2, k2n, FFT_NB1, cc), lambda nb, cb: (0, 0, nb, cb))
        o_shape = jax.ShapeDtypeStruct((2, k2n, FFT_N1, c), F32)
        sem = ("parallel", "parallel")
    return pl.pallas_call(
        functools.partial(_fft_a_kernel, packed=packed, cc=cc, k2n=k2n),
        grid=grid,
        in_specs=[x_spec, mat, t_spec, t_spec],
        out_specs=o_spec,
        out_shape=o_shape,
        compiler_params=_cparams(sem),
        name="hyena_fft_a",
    )(xv, tb["f_ri"], tb["ta_r"], tb["ta_i"])


def _fft_bf_kernel(a_ref, fhl_ref, fh_ref, o_ref):
    rr, ir = _cdot3(fhl_ref, fh_ref, a_ref[0], FFT_N1)
    ri, ii = _cdot3(fhl_ref, fh_ref, a_ref[1], FFT_N1)
    o_ref[0] = rr - ii
    o_ref[1] = ir + ri


def _fft_stage_b_filter(af, tb, c4):
    k2n = tb["k2n"]
    cc = min(c4, FFT_CC)
    mhl = pl.BlockSpec((4 * FFT_N1, FFT_N1), lambda k, cb: (0, 0))
    mh = pl.BlockSpec((2 * FFT_N1, FFT_N1), lambda k, cb: (0, 0))
    blk = pl.BlockSpec((2, None, FFT_N1, cc), lambda k, cb: (0, k, 0, cb))
    return pl.pallas_call(
        _fft_bf_kernel,
        grid=(k2n, c4 // cc),
        in_specs=[blk, mhl, mh],
        out_specs=blk,
        out_shape=jax.ShapeDtypeStruct(af.shape, F32),
        compiler_params=_cparams(("parallel", "parallel")),
        name="hyena_fft_b_filter",
    )(af, tb["f1_hl"], tb["f1_h"])


def _fft_mid_kernel(a_ref, h0_ref, h1_ref, s0_ref, s1_ref, fhl_ref, fh_ref, tr_ref, ti_ref, o_ref, *, n2l, cc):
    k2 = pl.program_id(1)
    rr, ir = _cdot3(fhl_ref, fh_ref, a_ref[0], FFT_N1)
    ri, ii = _cdot3(fhl_ref, fh_ref, a_ref[1], FFT_N1)
    xr = rr - ii
    xi = ir + ri
    norm = s0_ref[...] + s1_ref[...]
    inv = 1.0 / jnp.where(k2 >= n2l, norm[0:1, :], norm[1:2, :])
    kr = (h0_ref[0] + h1_ref[0]) * inv
    ki = (h0_ref[1] - h1_ref[1]) * inv
    yr = xr * kr - xi * ki
    yi = xr * ki + xi * kr
    r_yr, i_yr = _cdot3(fhl_ref, fh_ref, yr, FFT_N1)
    r_yi, i_yi = _cdot3(fhl_ref, fh_ref, yi, FFT_N1)
    br = r_yr + i_yi
    bi = r_yi - i_yr
    tr, ti = tr_ref[...], ti_ref[...]
    for l in range(cc // LANES):
        sl = slice(l * LANES, (l + 1) * LANES)
        cr, ci = br[:, sl], bi[:, sl]
        o_ref[0, :, sl] = cr * tr + ci * ti
        o_ref[1, :, sl] = ci * tr - cr * ti


def _fft_mid(a, hspec, sums, tb, order, c):
    p = a.shape[0]
    k2n, n2l = tb["k2n"], tb["n2l"]
    cc = min(c, FFT_CC)
    ncc = c // cc
    mhl = pl.BlockSpec((4 * FFT_N1, FFT_N1), lambda pi, k, cb: (0, 0))
    mh = pl.BlockSpec((2 * FFT_N1, FFT_N1), lambda pi, k, cb: (0, 0))
    blk = pl.BlockSpec((None, 2, None, FFT_N1, cc), lambda pi, k, cb: (pi, 0, k, 0, cb))
    hs = lambda side: pl.BlockSpec((2, None, FFT_N1, cc), lambda pi, k, cb: (0, k, 0, (2 * order + side) * ncc + cb))
    ss = lambda side: pl.BlockSpec((SUBLANES, cc), lambda pi, k, cb: (0, (2 * order + side) * ncc + cb))
    tw = pl.BlockSpec((None, FFT_N1, LANES), lambda pi, k, cb: (k, 0, 0))
    return pl.pallas_call(
        functools.partial(_fft_mid_kernel, n2l=n2l, cc=cc),
        grid=(p, k2n, ncc),
        in_specs=[blk, hs(0), hs(1), ss(0), ss(1), mhl, mh, tw, tw],
        out_specs=blk,
        out_shape=jax.ShapeDtypeStruct(a.shape, F32),
        compiler_params=_cparams(("parallel", "parallel", "parallel")),
        name="hyena_fft_mid",
    )(a, hspec, hspec, sums, sums, tb["f1_hl"], tb["f1_h"], tb["tb_r"], tb["tb_i"])


def _fft_inv_kernel(b_ref, gm_ref, g_ref, v_ref, skip_ref, o_ref, *, cc, r):
    skip = skip_ref[...]
    for n in range(FFT_NB1):
        r_br, i_br = _cdot6(gm_ref, _rows(b_ref, (0,), n), r)
        r_bi, i_bi = _cdot6(gm_ref, _rows(b_ref, (1,), n), r)
        yr = r_br + i_bi
        yi = r_bi - i_br
        for half, y in ((0, yr), (1, yi)):
            z = _rows(g_ref, (half,), n) * (y + skip * _rows(v_ref, (half,), n))
            o_ref[half, :, n:n + 1, :] = z.reshape(r, 1, cc).astype(o_ref.dtype)


def _fft_inverse_gate(bv, gv, vv, skip, tb, c):
    p = bv.shape[0]
    k2n, r = tb["k2n"], tb["r"]
    cc = min(c, FFT_CC)
    mat = pl.BlockSpec(tb["g_ri"].shape, lambda pi, nb, cb: (0, 0))
    xs = pl.BlockSpec((2, None, r, FFT_NB1, cc), lambda pi, nb, cb: (0, pi, 0, nb, cb))
    return pl.pallas_call(
        functools.partial(_fft_inv_kernel, cc=cc, r=r),
        grid=(p, FFT_N1 // FFT_NB1, c // cc),
        in_specs=[pl.BlockSpec((None, 2, k2n, FFT_NB1, cc), lambda pi, nb, cb: (pi, 0, 0, nb, cb)), mat, xs, xs,
                  pl.BlockSpec((1, cc), lambda pi, nb, cb: (0, cb))],
        out_specs=xs,
        out_shape=jax.ShapeDtypeStruct(gv.shape, F32),
        compiler_params=_cparams(("parallel", "parallel", "parallel")),
        name="hyena_fft_inv",
    )(bv, tb["g_ri"], gv, vv, skip.reshape(1, c))


def _hyena_mix(u0, conv_w, conv_b, fw1, fb1, fw2, fb2, fw3, fb3, ffreq, fw4, skip, ctx_len):
    b, s, c3 = u0.shape
    c = c3 // 3
    seq = s - ctx_len
    assert b % 2 == 0 and seq % FFT_N1 == 0 and ctx_len % FFT_N1 == 0
    p = b // 2
    r = s // FFT_N1
    g1, g2, v = _dwconv3(u0, conv_w, conv_b, ctx_len)
    hw, sums = _hyena_filters(seq, ctx_len, fw1, fb1, fw2, fb2, fw3, fb3, ffreq, fw4, c)
    tb = _fft_tables(seq, ctx_len)
    af = _fft_stage_a(hw.reshape(r, FFT_N1, 4 * c), tb, False, 4 * c)
    hspec = _fft_stage_b_filter(af, tb, 4 * c)
    view = lambda t: t.reshape(2, p, r, FFT_N1, c)
    z = v
    for order, gate in ((0, g1), (1, g2)):
        a = _fft_stage_a(view(z), tb, True, c)
        bm = _fft_mid(a, hspec, sums, tb, order, c)
        z = _fft_inverse_gate(bm, view(gate), view(z), skip[order], tb, c).reshape(b, s, c)
    return z


def _router_kernel(h_ref, r_ref, g_ref, hb_ref, *, ne):
    hb_ref[...] = h_ref[...].astype(BF16)
    logits = jnp.dot(h_ref[...], r_ref[...], preferred_element_type=F32, precision=HIGHEST)
    lane = lax.broadcasted_iota(jnp.int32, logits.shape, 1)
    neg = -jnp.inf
    logits = jnp.where(lane < ne, logits, neg)
    m1 = jnp.max(logits, axis=1, keepdims=True)
    i1 = jnp.min(jnp.where(logits == m1, lane, LANES), axis=1, keepdims=True)
    rest = jnp.where(lane == i1, neg, logits)
    m2 = jnp.max(rest, axis=1, keepdims=True)
    i2 = jnp.min(jnp.where(rest == m2, lane, LANES), axis=1, keepdims=True)
    e2 = jnp.exp(m2 - m1)
    w1 = 1.0 / (1.0 + e2)
    g_ref[...] = jnp.where(lane == i1, w1, 0.0) + jnp.where(lane == i2, e2 * w1, 0.0)


def _router(h2, router):
    b, s, d = h2.shape
    ne = router.shape[1]
    tm = _pick_tile(s, (544, 384, 256, 128))
    rp = jnp.pad(router.astype(F32), ((0, 0), (0, LANES - ne)))
    return pl.pallas_call(
        functools.partial(_router_kernel, ne=ne),
        grid=(b, s // tm),
        in_specs=[pl.BlockSpec((None, tm, d), lambda bi, j: (bi, j, 0)), pl.BlockSpec((d, LANES), lambda bi, j: (0, 0))],
        out_specs=[pl.BlockSpec((None, tm, LANES), lambda bi, j: (bi, j, 0)),
                   pl.BlockSpec((None, tm, d), lambda bi, j: (bi, j, 0))],
        out_shape=[jax.ShapeDtypeStruct((b, s, LANES), F32), jax.ShapeDtypeStruct((b, s, d), BF16)],
        compiler_params=_cparams(("parallel", "parallel")),
        name="moe_router",
    )(h2, rp)


def _moe_sparse_kernel(h_ref, x_ref, mod_ref, gate_ref, wg_ref, wu_ref, wd_ref, o_ref, dest, cnt, xe, ye,
                       *, tb, min_tiles, main_tiles, rb, ctx_len, ne, nf):
    j, e, f = pl.program_id(1), pl.program_id(2), pl.program_id(3)
    half = tb // 2

    @pl.when(jnp.logical_and(e == 0, f == 0))
    def _():
        tri = (lax.broadcasted_iota(jnp.int32, (rb, rb), 0) > lax.broadcasted_iota(jnp.int32, (rb, rb), 1))
        tri = jnp.where(tri, 1.0, 0.0).astype(BF16)
        carry = jnp.zeros((1, LANES), F32)
        for k in range(tb // rb):
            routed = gate_ref[k * rb:(k + 1) * rb, :] > 0.0
            r01 = jnp.where(routed, 1.0, 0.0)
            before = jnp.dot(tri, r01.astype(BF16), preferred_element_type=F32) + carry
            dest[k * rb:(k + 1) * rb, :] = jnp.where(routed, before, -1.0)
            carry = carry + jnp.sum(r01, axis=0, keepdims=True)
        cnt[...] = jnp.broadcast_to(carry, cnt.shape)
        o_ref[...] = jnp.zeros_like(o_ref)

    lane1 = lax.broadcasted_iota(jnp.int32, (1, LANES), 1)
    n_e = jnp.sum(jnp.where(lane1 == e, cnt[0:1, :], 0.0))
    n_tiles = jnp.ceil(n_e * (1.0 / LANES)).astype(jnp.int32)

    lane_h = lax.broadcasted_iota(jnp.int32, (half, LANES), 1)

    def expert_col(ref, rows):
        return jnp.sum(jnp.where(lane_h == e, ref[rows, :], 0.0), axis=1, keepdims=True)

    def one_hot(rows, first, width):
        dcol = expert_col(dest, rows) - first
        lanef = lane_h.astype(F32)
        pieces = [jnp.where(dcol == lanef + float(k * LANES), 1.0, 0.0).astype(BF16) for k in range(width // LANES)]
        return jnp.concatenate(pieces, axis=1)

    def gather(first, width, dst):
        acc = None
        for hb in range(2):
            rows = slice(hb * half, (hb + 1) * half)
            part = lax.dot_general(one_hot(rows, first, width), h_ref[rows, :], (((0,), (0,)), ((), ())),
                                   preferred_element_type=F32)
            acc = part if acc is None else acc + part
        xe[dst, :] = acc.astype(BF16)
        ye[dst, :] = jnp.zeros((width, ye.shape[1]), F32)

    def expert_ffn(dst):
        x = xe[dst, :]
        a = jnp.dot(x, wg_ref[...], preferred_element_type=F32)
        u = jnp.dot(x, wu_ref[...], preferred_element_type=F32)
        act = (a * jax.nn.sigmoid(a) * u).astype(BF16)
        ye[dst, :] += jnp.dot(act, wd_ref[...], preferred_element_type=F32)

    def scatter(first, width, dst):
        y = ye[dst, :].astype(BF16)
        for hb in range(2):
            rows = slice(hb * half, (hb + 1) * half)
            back = jnp.dot(one_hot(rows, first, width), y, preferred_element_type=F32)
            o_ref[rows, :] += expert_col(gate_ref, rows) * back

    main_rows = main_tiles * LANES
    active = n_tiles > 0
    n_extra = jnp.maximum(n_tiles - main_tiles, 0)

    def extra(fn):
        def body(c, carry):
            off = pl.multiple_of(main_rows + c * LANES, LANES)
            fn(off.astype(F32), LANES, pl.ds(off, LANES))
            return carry
        lax.fori_loop(0, n_extra, body, 0)

    def main_ffn():
        unit = LANES // 2
        n_units = jnp.ceil(n_e * (1.0 / unit)).astype(jnp.int32)
        lo, hi = 2 * min_tiles - 1, 2 * main_tiles
        for units in range(lo, hi + 1):
            sel = n_units >= units if units == hi else n_units == units
            if units == lo:
                sel = n_units <= units
            pl.when(sel)(functools.partial(expert_ffn, slice(0, units * unit)))

    @pl.when(jnp.logical_and(active, f == 0))
    def _():
        gather(0.0, main_rows, slice(0, main_rows))
        extra(gather)

    @pl.when(active)
    def _():
        main_ffn()
        extra(lambda first, width, dst: expert_ffn(dst))

    @pl.when(jnp.logical_and(active, f == nf - 1))
    def _():
        scatter(0.0, main_rows, slice(0, main_rows))
        extra(scatter)

    @pl.when(jnp.logical_and(e == ne - 1, f == nf - 1))
    def _():
        x = x_ref[...]
        is_ctx = _row_is_ctx(j, tb, ctx_len, x.shape)
        o_ref[...] = x + _sel_mod(mod_ref, is_ctx, 5) * o_ref[...]


def _moe_sparse(hb16, x_all, mods_i, gate, wg, wu, wd, ctx_len):
    b, s, d = x_all.shape
    ne, _, f = wg.shape
    tb = _pick_tile(s, (1088, 768, 512, 256))
    max_tiles = -(-tb // LANES)
    ch = max_tiles * LANES
    mean_tiles = -(-tb * TOP_K // (ne * LANES))
    min_tiles, main_tiles = max(mean_tiles - 1, 1), min(mean_tiles + 1, max_tiles)
    rb = max(r for r in range(SUBLANES, 257, SUBLANES) if tb % r == 0)
    fc = _pick_tile(f, (512, 256, 128))
    nf = f // fc
    row = lambda bi, j, e, k: (bi, j, 0)
    return pl.pallas_call(
        functools.partial(_moe_sparse_kernel, tb=tb, min_tiles=min_tiles, main_tiles=main_tiles, rb=rb,
                          ctx_len=ctx_len, ne=ne, nf=nf),
        grid=(b, s // tb, ne, nf),
        in_specs=[
            pl.BlockSpec((None, tb, d), row),
            pl.BlockSpec((None, tb, d), row, pipeline_mode=pl.Buffered(1)),
            pl.BlockSpec((None, 2, 8, d), lambda bi, j, e, k: (bi, 0, 0, 0)),
            pl.BlockSpec((None, tb, LANES), row),
            pl.BlockSpec((None, d, fc), lambda bi, j, e, k: (e, 0, k)),
            pl.BlockSpec((None, d, fc), lambda bi, j, e, k: (e, 0, k)),
            pl.BlockSpec((None, fc, d), lambda bi, j, e, k: (e, k, 0)),
        ],
        out_specs=pl.BlockSpec((None, tb, d), row),
        out_shape=jax.ShapeDtypeStruct((b, s, d), F32),
        scratch_shapes=[pltpu.VMEM((tb, LANES), F32), pltpu.VMEM((SUBLANES, LANES), F32),
                        pltpu.VMEM((ch, d), BF16), pltpu.VMEM((ch, d), F32)],
        compiler_params=_cparams(("parallel", "parallel", "arbitrary", "arbitrary")),
        name="moe_experts",
    )(hb16, x_all, mods_i, gate, wg.astype(BF16), wu.astype(BF16), wd.astype(BF16))


def _final_norm_kernel(x_ref, g_ref, o_ref):
    x = x_ref[...]
    ms = jnp.mean(x * x, axis=-1, keepdims=True)
    o_ref[...] = (x * lax.rsqrt(ms + EPS)) * g_ref[...]


def _final_norm(x_all, g, ctx_len):
    b, s, d = x_all.shape
    seq = s - ctx_len
    tm = _pick_tile(math.gcd(seq, ctx_len), (512, 256, 128))
    off = ctx_len // tm
    return pl.pallas_call(
        _final_norm_kernel,
        grid=(b, seq // tm),
        in_specs=[pl.BlockSpec((None, tm, d), lambda bi, j: (bi, j + off, 0)), pl.BlockSpec((1, d), lambda bi, j: (0, 0))],
        out_specs=pl.BlockSpec((None, tm, d), lambda bi, j: (bi, j, 0)),
        out_shape=jax.ShapeDtypeStruct((b, seq, d), F32),
        compiler_params=_cparams(("parallel", "parallel")),
        name="final_norm",
    )(x_all, g.reshape(1, d))


def kernel(x, c, ctx, c_ctx, ada_w, ada_b, norm_mix, norm_ffn, norm_final,
           lru_w_in, lru_conv_w, lru_conv_b, lru_w_a, lru_b_a, lru_w_x, lru_b_x, lru_lambda, lru_w_out,
           attn_w_qkv, attn_sinks, attn_w_o,
           hy_w_in, hy_b_in, hy_conv_w, hy_conv_b, hy_f_w1, hy_f_b1, hy_f_w2, hy_f_b2, hy_f_w3, hy_f_b3,
           hy_f_freq, hy_f_w4, hy_skip, hy_w_out, hy_b_out,
           ffn_w_gate, ffn_w_up, ffn_w_down,
           moe_router, moe_w_gate, moe_w_up, moe_w_down):
    depth = ada_w.shape[0]
    ctx_len, seq, d = ctx.shape[1], x.shape[1], x.shape[2]
    x_all = jnp.concatenate([ctx, x], axis=1)
    mods = _mods(c, c_ctx, ada_w, ada_b)
    zero_d = jnp.zeros((d,), F32)
    for i in range(depth):
        j = i // N_MIXERS
        moe = i % 2 == 1
        h_dtype = F32 if moe else BF16
        if i % N_MIXERS == 0:
            gu = _norm_mm(x_all, mods[i], norm_mix[i], lru_w_in[j], jnp.zeros((lru_w_in.shape[2],), F32), ctx_len, F32)
            hs_f, hs_b = _lru_scan(gu, lru_conv_w[j], lru_conv_b[j], lru_w_a[j], lru_b_a[j], lru_w_x[j], lru_b_x[j],
                                   lru_lambda[j], ctx_len)
            x_all, h2 = _lru_mm_res(hs_f, hs_b, gu, x_all, mods[i], norm_ffn[i], lru_w_out[j], ctx_len, h_dtype)
        elif i % N_MIXERS == 1:
            cos_t, sin_t = _rope_tables(seq, ctx_len)
            qkv = _qkv_proj(x_all, mods[i], norm_mix[i], attn_w_qkv[j], cos_t, sin_t, ctx_len)
            o = _attention(qkv, attn_sinks[j], ctx_len)
            x_all, h2 = _mm_res(o, x_all, mods[i], norm_ffn[i], attn_w_o[j], zero_d, ctx_len, h_dtype)
        else:
            u0 = _norm_mm(x_all, mods[i], norm_mix[i], hy_w_in[j], hy_b_in[j], ctx_len, F32)
            z = _hyena_mix(u0, hy_conv_w[j], hy_conv_b[j], hy_f_w1[j], hy_f_b1[j], hy_f_w2[j], hy_f_b2[j],
                           hy_f_w3[j], hy_f_b3[j], hy_f_freq[j], hy_f_w4[j], hy_skip[j], ctx_len)
            x_all, h2 = _mm_res(z, x_all, mods[i], norm_ffn[i], hy_w_out[j], hy_b_out[j], ctx_len, h_dtype)
        if moe:
            gate, hb16 = _router(h2, moe_router[i // 2])
            x_all = _moe_sparse(hb16, x_all, mods[i], gate, moe_w_gate[i // 2], moe_w_up[i // 2], moe_w_down[i // 2],
                                ctx_len)
        else:
            x_all = _ffn_dense(h2, x_all, mods[i], ffn_w_gate[i // 2], ffn_w_up[i // 2], ffn_w_down[i // 2], ctx_len)
    return _final_norm(x_all, norm_final, ctx_len)
```

```python
import functools
import math

import jax
import jax.numpy as jnp
import numpy as np
from jax import lax
from jax.experimental import pallas as pl
from jax.experimental.pallas import tpu as pltpu

F32 = jnp.float32
BF16 = jnp.bfloat16
HIGHEST = lax.Precision.HIGHEST

N_MIXERS = 3
EPS = 1e-6
GRID_W = 64
RNN_BLOCKS = 8
LRU_C = 8.0
N_HEADS = 16
N_KV_HEADS = 4
HEAD_DIM = 64
WINDOW = 128
ROPE_BASE = 10000.0
FILTER_BANDS = 16
DECAY_TARGET = 1e-2
FAST_DECAY_PCT = 0.3
SLOW_DECAY_PCT = 1.5
N_EXPERTS = 8
TOP_K = 2

LANES = 128
SUBLANES = 8
VMEM_LIMIT = 56 * 1024 * 1024
FFT_N1 = 128


def _cparams(sem, vmem=VMEM_LIMIT):
    return pltpu.CompilerParams(dimension_semantics=sem, vmem_limit_bytes=vmem)


def _pick_tile(n, cands):
    for c in cands:
        if n % c == 0:
            return c
    raise ValueError(f"no tile for {n} in {cands}")


def _row_is_ctx(j, tm, ctx_len, shape):
    rows = j * tm + lax.broadcasted_iota(jnp.int32, shape, 0)
    return rows < ctx_len


def _sel_mod(mod_ref, is_ctx, idx):
    return jnp.where(is_ctx, mod_ref[0, idx:idx + 1, :], mod_ref[1, idx:idx + 1, :])


def _adanorm(x, g, shift, scale):
    ms = jnp.mean(x * x, axis=-1, keepdims=True)
    y = x * lax.rsqrt(ms + EPS)
    return (y * g) * (1.0 + scale) + shift


def _mods(c, c_ctx, ada_w, ada_b):
    depth, d, d6 = ada_w.shape
    b = c.shape[0]
    rows = ((b + 1 + SUBLANES - 1) // SUBLANES) * SUBLANES
    cond = jnp.concatenate([c, c_ctx[None, :], jnp.zeros((rows - b - 1, d), F32)], axis=0)
    tn = _pick_tile(d6, (1024, 512, 256, 128))

    def kern(c_ref, w_ref, b_ref, o_ref):
        x = c_ref[...]
        s = x * jax.nn.sigmoid(x)
        o_ref[...] = jnp.dot(s, w_ref[...], preferred_element_type=F32, precision=HIGHEST) + b_ref[...]

    out = pl.pallas_call(
        kern,
        grid=(depth, d6 // tn),
        in_specs=[
            pl.BlockSpec((rows, d), lambda i, n: (0, 0)),
            pl.BlockSpec((None, d, tn), lambda i, n: (i, 0, n)),
            pl.BlockSpec((None, 1, tn), lambda i, n: (i, 0, n)),
        ],
        out_specs=pl.BlockSpec((None, rows, tn), lambda i, n: (i, 0, n)),
        out_shape=jax.ShapeDtypeStruct((depth, rows, d6), F32),
        compiler_params=_cparams(("parallel", "parallel")),
        name="mods",
    )(cond, ada_w, ada_b.reshape(depth, 1, d6))
    m = out.reshape(depth, rows, 6, d)
    lat = m[:, :b]
    ctx = jnp.broadcast_to(m[:, b:b + 1], lat.shape)
    both = jnp.stack([ctx, lat], axis=2)
    return jnp.pad(both, ((0, 0), (0, 0), (0, 0), (0, 2), (0, 0)))


def _norm_mm_kernel(x_ref, mod_ref, g_ref, w_ref, b_ref, o_ref, *, tm, ctx_len):
    j = pl.program_id(1)
    x = x_ref[...]
    is_ctx = _row_is_ctx(j, tm, ctx_len, x.shape)
    h = _adanorm(x, g_ref[...], _sel_mod(mod_ref, is_ctx, 0), _sel_mod(mod_ref, is_ctx, 1))
    acc = jnp.dot(h.astype(BF16), w_ref[...], preferred_element_type=F32) + b_ref[...]
    o_ref[...] = acc.astype(o_ref.dtype)


def _norm_mm(x_all, mods_i, g, w, bias, ctx_len, out_dtype):
    b, s, d = x_all.shape
    n = w.shape[1]
    tm = _pick_tile(s, (544, 384, 256, 128))
    return pl.pallas_call(
        functools.partial(_norm_mm_kernel, tm=tm, ctx_len=ctx_len),
        grid=(b, s // tm),
        in_specs=[
            pl.BlockSpec((None, tm, d), lambda bi, j: (bi, j, 0)),
            pl.BlockSpec((None, 2, 8, d), lambda bi, j: (bi, 0, 0, 0)),
            pl.BlockSpec((1, d), lambda bi, j: (0, 0)),
            pl.BlockSpec((d, n), lambda bi, j: (0, 0)),
            pl.BlockSpec((1, n), lambda bi, j: (0, 0)),
        ],
        out_specs=pl.BlockSpec((None, tm, n), lambda bi, j: (bi, j, 0)),
        out_shape=jax.ShapeDtypeStruct((b, s, n), out_dtype),
        compiler_params=_cparams(("parallel", "parallel")),
        name="norm_mm",
    )(x_all, mods_i, g.reshape(1, d), w.astype(BF16), bias.reshape(1, n).astype(F32))


def _mm_res_body(y_bf16, x_ref, mod_ref, g_ref, w_ref, b_ref, xo_ref, ho_ref, tm, ctx_len):
    j = pl.program_id(1)
    x = x_ref[...]
    is_ctx = _row_is_ctx(j, tm, ctx_len, x.shape)
    acc = jnp.dot(y_bf16, w_ref[...], preferred_element_type=F32) + b_ref[...]
    x_new = x + _sel_mod(mod_ref, is_ctx, 2) * acc
    xo_ref[...] = x_new
    h2 = _adanorm(x_new, g_ref[...], _sel_mod(mod_ref, is_ctx, 3), _sel_mod(mod_ref, is_ctx, 4))
    ho_ref[...] = h2.astype(ho_ref.dtype)


def _mm_res_kernel(y_ref, x_ref, mod_ref, g_ref, w_ref, b_ref, xo_ref, ho_ref, *, tm, ctx_len):
    _mm_res_body(y_ref[...].astype(BF16), x_ref, mod_ref, g_ref, w_ref, b_ref, xo_ref, ho_ref, tm, ctx_len)


def _lru_mm_res_kernel(hf_ref, hb_ref, gate_ref, x_ref, mod_ref, g_ref, w_ref, b_ref, xo_ref, ho_ref, *, tm, ctx_len):
    y = (hf_ref[...] + hb_ref[...]) * jax.nn.gelu(gate_ref[...])
    _mm_res_body(y.astype(BF16), x_ref, mod_ref, g_ref, w_ref, b_ref, xo_ref, ho_ref, tm, ctx_len)


def _mm_res_call(kern, ys, y_specs, x_all, mods_i, g_ffn, w, bias, ctx_len, tm, h_dtype):
    b, s, d = x_all.shape
    k = w.shape[0]
    return pl.pallas_call(
        functools.partial(kern, tm=tm, ctx_len=ctx_len),
        grid=(b, s // tm),
        in_specs=y_specs + [
            pl.BlockSpec((None, tm, d), lambda bi, j: (bi, j, 0)),
            pl.BlockSpec((None, 2, 8, d), lambda bi, j: (bi, 0, 0, 0)),
            pl.BlockSpec((1, d), lambda bi, j: (0, 0)),
            pl.BlockSpec((k, d), lambda bi, j: (0, 0)),
            pl.BlockSpec((1, d), lambda bi, j: (0, 0)),
        ],
        out_specs=[
            pl.BlockSpec((None, tm, d), lambda bi, j: (bi, j, 0)),
            pl.BlockSpec((None, tm, d), lambda bi, j: (bi, j, 0)),
        ],
        out_shape=[jax.ShapeDtypeStruct((b, s, d), F32), jax.ShapeDtypeStruct((b, s, d), h_dtype)],
        compiler_params=_cparams(("parallel", "parallel")),
        name="mm_res",
    )(*ys, x_all, mods_i, g_ffn.reshape(1, d), w.astype(BF16), bias.reshape(1, d).astype(F32))


def _mm_res(y, x_all, mods_i, g_ffn, w, bias, ctx_len, h_dtype=BF16):
    b, s, d = x_all.shape
    k = w.shape[0]
    tm = _pick_tile(s, (544, 384, 256, 128))
    spec = [pl.BlockSpec((None, tm, k), lambda bi, j: (bi, j, 0))]
    return _mm_res_call(_mm_res_kernel, [y], spec, x_all, mods_i, g_ffn, w, bias, ctx_len, tm, h_dtype)


def _lru_mm_res(hs_f, hs_b, gu, x_all, mods_i, g_ffn, w, ctx_len, h_dtype=BF16):
    b, s, d = x_all.shape
    k = w.shape[0]
    tm = _pick_tile(s, (544, 384, 256, 128))
    specs = [
        pl.BlockSpec((None, tm, k), lambda bi, j: (bi, j, 0)),
        pl.BlockSpec((None, tm, k), lambda bi, j: (bi, j, 0)),
        pl.BlockSpec((None, tm, k), lambda bi, j: (bi, j, 0)),
    ]
    return _mm_res_call(_lru_mm_res_kernel, [hs_f, hs_b, gu], specs, x_all, mods_i, g_ffn, w,
                        jnp.zeros((d,), F32), ctx_len, tm, h_dtype)


def _ffn_kernel(h_ref, x_ref, mod_ref, wg_ref, wu_ref, wd_ref, o_ref, *, tm, ctx_len):
    j = pl.program_id(1)
    h = h_ref[...]
    a = jnp.dot(h, wg_ref[...], preferred_element_type=F32)
    u = jnp.dot(h, wu_ref[...], preferred_element_type=F32)
    act = (a * jax.nn.sigmoid(a) * u).astype(BF16)
    y = jnp.dot(act, wd_ref[...], preferred_element_type=F32)
    x = x_ref[...]
    is_ctx = _row_is_ctx(j, tm, ctx_len, x.shape)
    o_ref[...] = x + _sel_mod(mod_ref, is_ctx, 5) * y


def _ffn_dense(h2, x_all, mods_i, wg, wu, wd, ctx_len):
    b, s, d = x_all.shape
    f = wg.shape[1]
    tm = _pick_tile(s, (544, 384, 256, 128))
    resident = dict(pipeline_mode=pl.Buffered(1))
    return pl.pallas_call(
        functools.partial(_ffn_kernel, tm=tm, ctx_len=ctx_len),
        grid=(b, s // tm),
        in_specs=[
            pl.BlockSpec((None, tm, d), lambda bi, j: (bi, j, 0)),
            pl.BlockSpec((None, tm, d), lambda bi, j: (bi, j, 0)),
            pl.BlockSpec((None, 2, 8, d), lambda bi, j: (bi, 0, 0, 0)),
            pl.BlockSpec((d, f), lambda bi, j: (0, 0), **resident),
            pl.BlockSpec((d, f), lambda bi, j: (0, 0), **resident),
            pl.BlockSpec((f, d), lambda bi, j: (0, 0), **resident),
        ],
        out_specs=pl.BlockSpec((None, tm, d), lambda bi, j: (bi, j, 0)),
        out_shape=jax.ShapeDtypeStruct((b, s, d), F32),
        compiler_params=_cparams(("parallel", "parallel")),
        name="ffn_dense",
    )(h2, x_all, mods_i, wg.astype(BF16), wu.astype(BF16), wd.astype(BF16))


def _lru_bwd_tile(j, nt, nct):
    return jnp.where(j < nct, nct - 1 - j, nt - 1 - (j - nct))


def _lru_scan_kernel(um_f, up_f, un_f, um_b, up_b, un_b, cw_ref, cb_ref, wax_f, wax_b, ba_ref, bx_ref, lam_ref,
                     hsf_ref, hsb_ref, ext, af_s, bf_s, ab_s, bb_s, carry, *, tt, nt, nct, wl, nb):
    j = pl.program_id(1)

    @pl.when(j == 0)
    def _():
        carry[...] = jnp.zeros_like(carry)

    def coeffs(um, up, un, t, d, wax_ref, a_s, b_s):
        prev_ok = jnp.logical_and(t != 0, t != nct)
        next_ok = jnp.logical_and(t != nct - 1, t != nt - 1)
        ext[:, SUBLANES:SUBLANES + tt, :] = um[...]
        ext[:, 0:SUBLANES, :] = jnp.where(prev_ok, up[...], 0.0)
        ext[:, SUBLANES + tt:2 * SUBLANES + tt, :] = jnp.where(next_ok, un[...], 0.0)
        uc = cb_ref[...].reshape(1, 1, wl)
        for k in range(4):
            uc = uc + cw_ref[k:k + 1, :].reshape(1, 1, wl) * ext[:, SUBLANES - 2 + k:SUBLANES - 2 + k + tt, :]
        sp = jax.nn.softplus(-lam_ref[d:d + 1, :])
        for hb in range(wl // LANES):
            sl = slice(hb * LANES, (hb + 1) * LANES)
            ub = uc[:, :, sl].reshape(nb * tt, LANES)
            pre = jnp.dot(ub.astype(BF16), wax_ref[hb], preferred_element_type=F32)
            r = jax.nn.sigmoid(pre[:, :LANES] + ba_ref[d:d + 1, sl])
            i = jax.nn.sigmoid(pre[:, LANES:] + bx_ref[d:d + 1, sl])
            log_a = (-LRU_C) * r * sp[:, sl]
            a = jnp.exp(log_a)
            b = jnp.sqrt(-jnp.tanh(log_a) * (a * a + 1.0)) * (i * ub)
            a_s[:, :, sl] = a.reshape(nb, tt, LANES)
            b_s[:, :, sl] = b.reshape(nb, tt, LANES)

    tb = _lru_bwd_tile(j, nt, nct)
    coeffs(um_f, up_f, un_f, j, 0, wax_f, af_s, bf_s)
    coeffs(um_b, up_b, un_b, tb, 1, wax_b, ab_s, bb_s)

    def step(t, c):
        hf, hb = c
        hf = af_s[:, pl.ds(t, 1), :].reshape(nb, wl) * hf + bf_s[:, pl.ds(t, 1), :].reshape(nb, wl)
        hsf_ref[:, pl.ds(t, 1), :] = hf.reshape(nb, 1, wl)
        r = tt - 1 - t
        hb = ab_s[:, pl.ds(r, 1), :].reshape(nb, wl) * hb + bb_s[:, pl.ds(r, 1), :].reshape(nb, wl)
        hsb_ref[:, pl.ds(r, 1), :] = hb.reshape(nb, 1, wl)
        return hf, hb

    hf, hb = lax.fori_loop(0, tt, step, (carry[0], carry[1]), unroll=8)
    carry[0] = hf
    carry[1] = hb


def _lru_scan(gu, conv_w, conv_b, w_a, b_a, w_x, b_x, lam, ctx_len):
    nb, s, d2 = gu.shape
    d = d2 // 2
    tt = 128
    assert s % tt == 0 and ctx_len % tt == 0 and d % (RNN_BLOCKS * LANES) == 0 and d // RNN_BLOCKS == LANES
    nt, nct = s // tt, ctx_len // tt
    wl = 4 * LANES
    ncg = d // wl
    coff = d // wl
    r8 = tt // SUBLANES
    last8 = s // SUBLANES - 1
    wax = jnp.concatenate([w_a, w_x], axis=-1).astype(BF16)

    def main_f(cg, j): return (0, j, coff + cg)
    def prev_f(cg, j): return (0, jnp.maximum(j * r8 - 1, 0), coff + cg)
    def next_f(cg, j): return (0, jnp.minimum((j + 1) * r8, last8), coff + cg)
    def tb_(j): return _lru_bwd_tile(j, nt, nct)
    def main_b(cg, j): return (0, tb_(j), coff + cg)
    def prev_b(cg, j): return (0, jnp.maximum(tb_(j) * r8 - 1, 0), coff + cg)
    def next_b(cg, j): return (0, jnp.minimum((tb_(j) + 1) * r8, last8), coff + cg)

    big = lambda im: pl.BlockSpec((nb, tt, wl), im)
    halo = lambda im: pl.BlockSpec((nb, SUBLANES, wl), im)
    vec2 = pl.BlockSpec((2, wl), lambda cg, j: (0, cg))
    return pl.pallas_call(
        functools.partial(_lru_scan_kernel, tt=tt, nt=nt, nct=nct, wl=wl, nb=nb),
        grid=(ncg, nt),
        in_specs=[
            big(main_f), halo(prev_f), halo(next_f), big(main_b), halo(prev_b), halo(next_b),
            pl.BlockSpec((4, wl), lambda cg, j: (0, cg)),
            pl.BlockSpec((1, wl), lambda cg, j: (0, cg)),
            pl.BlockSpec((None, wl // LANES, LANES, 2 * LANES), lambda cg, j: (0, cg, 0, 0)),
            pl.BlockSpec((None, wl // LANES, LANES, 2 * LANES), lambda cg, j: (1, cg, 0, 0)),
            vec2, vec2, vec2,
        ],
        out_specs=[
            pl.BlockSpec((nb, tt, wl), lambda cg, j: (0, j, cg)),
            pl.BlockSpec((nb, tt, wl), lambda cg, j: (0, tb_(j), cg)),
        ],
        out_shape=[jax.ShapeDtypeStruct((nb, s, d), F32), jax.ShapeDtypeStruct((nb, s, d), F32)],
        scratch_shapes=[
            pltpu.VMEM((nb, tt + 2 * SUBLANES, wl), F32),
            pltpu.VMEM((nb, tt, wl), F32), pltpu.VMEM((nb, tt, wl), F32),
            pltpu.VMEM((nb, tt, wl), F32), pltpu.VMEM((nb, tt, wl), F32),
            pltpu.VMEM((2, nb, wl), F32),
        ],
        compiler_params=_cparams(("parallel", "arbitrary")),
        name="lru_scan",
    )(gu, gu, gu, gu, gu, gu, conv_w, conv_b.reshape(1, d), wax, wax, b_a, b_x, lam)


def _rope_tables(seq, ctx_len):
    q = HEAD_DIM // 4
    inv_freq = ROPE_BASE ** (-jnp.arange(q, dtype=F32) / q)
    pos = jnp.arange(seq, dtype=jnp.int32)
    row = (pos // GRID_W).astype(F32)[:, None] * inv_freq
    col = (pos % GRID_W).astype(F32)[:, None] * inv_freq
    ang = jnp.concatenate([row, row, col, col], axis=-1)
    sign = jnp.concatenate([-jnp.ones((q,), F32), jnp.ones((q,), F32)] * 2)
    cos = jnp.concatenate([jnp.ones((ctx_len, HEAD_DIM), F32), jnp.cos(ang)], axis=0)
    sin = jnp.concatenate([jnp.zeros((ctx_len, HEAD_DIM), F32), jnp.sin(ang) * sign], axis=0)
    return jnp.tile(cos, (1, 2)), jnp.tile(sin, (1, 2))


def _qkv_kernel(x_ref, mod_ref, g_ref, w_ref, cos_ref, sin_ref, o_ref, *, tm, ctx_len, n_q, n_rope):
    j = pl.program_id(1)
    x = x_ref[...]
    is_ctx = _row_is_ctx(j, tm, ctx_len, x.shape)
    h = _adanorm(x, g_ref[...], _sel_mod(mod_ref, is_ctx, 0), _sel_mod(mod_ref, is_ctx, 1))
    acc = jnp.dot(h.astype(BF16), w_ref[...], preferred_element_type=F32)
    cos = cos_ref[...]
    sin = sin_ref[...]
    lane = lax.broadcasted_iota(jnp.int32, (tm, LANES), 1)
    first_half = (lane % (HEAD_DIM // 2)) < (HEAD_DIM // 4)
    q_scale = HEAD_DIM ** -0.5
    for c in range(acc.shape[1] // LANES):
        v = acc[:, c * LANES:(c + 1) * LANES]
        if c < n_rope:
            partner = jnp.where(first_half, pltpu.roll(v, LANES - HEAD_DIM // 4, 1), pltpu.roll(v, HEAD_DIM // 4, 1))
            v = v * cos + partner * sin
            if c < n_q:
                v = v * q_scale
        o_ref[:, c * LANES:(c + 1) * LANES] = v.astype(o_ref.dtype)


def _qkv_proj(x_all, mods_i, g, w, cos_t, sin_t, ctx_len):
    b, s, d = x_all.shape
    n = w.shape[1]
    tm = _pick_tile(s, (544, 384, 256, 128))
    n_q = N_HEADS * HEAD_DIM // LANES
    n_rope = (N_HEADS + N_KV_HEADS) * HEAD_DIM // LANES
    return pl.pallas_call(
        functools.partial(_qkv_kernel, tm=tm, ctx_len=ctx_len, n_q=n_q, n_rope=n_rope),
        grid=(b, s // tm),
        in_specs=[
            pl.BlockSpec((None, tm, d), lambda bi, j: (bi, j, 0)),
            pl.BlockSpec((None, 2, 8, d), lambda bi, j: (bi, 0, 0, 0)),
            pl.BlockSpec((1, d), lambda bi, j: (0, 0)),
            pl.BlockSpec((d, n), lambda bi, j: (0, 0)),
            pl.BlockSpec((tm, LANES), lambda bi, j: (j, 0)),
            pl.BlockSpec((tm, LANES), lambda bi, j: (j, 0)),
        ],
        out_specs=pl.BlockSpec((None, tm, n), lambda bi, j: (bi, j, 0)),
        out_shape=jax.ShapeDtypeStruct((b, s, n), BF16),
        compiler_params=_cparams(("parallel", "parallel")),
        name="qkv_proj",
    )(x_all, mods_i, g.reshape(1, d), w.astype(BF16), cos_t, sin_t)


def _attn_kernel(q_ref, kp_ref, km_ref, kn_ref, kc_ref, vp_ref, vm_ref, vn_ref, vc_ref, sink_ref, o_ref,
                 *, tq, nct, seq, ctx_len):
    j = pl.program_id(1)
    k_all = jnp.concatenate([kp_ref[...], km_ref[...], kn_ref[...], kc_ref[...]], axis=0)
    v_all = jnp.concatenate([vp_ref[...], vm_ref[...], vn_ref[...], vc_ref[...]], axis=0)
    n_loc = 2 * tq
    n_keys = n_loc + ctx_len
    row = lax.broadcasted_iota(jnp.int32, (tq, n_keys), 0)
    col = lax.broadcasted_iota(jnp.int32, (tq, n_keys), 1)
    p0 = (j - nct) * tq
    kpos = p0 - tq // 2 + col
    band = jnp.abs(row - (col - tq // 2)) <= WINDOW
    in_seq = jnp.logical_and(kpos >= 0, kpos < seq)
    valid_loc = jnp.logical_and(jnp.logical_and(band, in_seq), j >= nct)
    valid = jnp.logical_or(col >= n_loc, valid_loc)
    group = N_HEADS // N_KV_HEADS
    outs = []
    for h in range(N_HEADS):
        kvh = h // group
        qh = q_ref[:, h * HEAD_DIM:(h + 1) * HEAD_DIM]
        kk = k_all[:, kvh * HEAD_DIM:(kvh + 1) * HEAD_DIM]
        vv = v_all[:, kvh * HEAD_DIM:(kvh + 1) * HEAD_DIM]
        s = lax.dot_general(qh, kk, (((1,), (1,)), ((), ())), preferred_element_type=F32)
        s = jnp.where(valid, s, -1e30)
        sink = sink_ref[h:h + 1, 0:1]
        m = jnp.maximum(jnp.max(s, axis=1, keepdims=True), sink)
        p = jnp.exp(s - m)
        denom = jnp.sum(p, axis=1, keepdims=True) + jnp.exp(sink - m)
        o = jnp.dot(p.astype(BF16), vv, preferred_element_type=F32)
        outs.append((o / denom).astype(o_ref.dtype))
    o_ref[...] = jnp.concatenate(outs, axis=1)


def _attention(qkv, sinks, ctx_len):
    b, s, _ = qkv.shape
    seq = s - ctx_len
    tq = 2 * WINDOW
    half = tq // 2
    assert ctx_len % tq == 0 and seq % tq == 0
    nct = ctx_len // tq
    dq = N_HEADS * HEAD_DIM
    dkv = N_KV_HEADS * HEAD_DIM
    kcol, vcol = dq // dkv, dq // dkv + 1
    last_half = s // half - 1
    sink_b = jnp.broadcast_to(sinks.astype(F32)[:, None], (N_HEADS, LANES))

    def prev(col): return lambda bi, j: (bi, jnp.maximum(2 * j - 1, 0), col)
    def main(col): return lambda bi, j: (bi, j, col)
    def nxt(col): return lambda bi, j: (bi, jnp.minimum(2 * j + 2, last_half), col)
    def ctx(col): return lambda bi, j: (bi, 0, col)
    kv_specs = lambda col: [
        pl.BlockSpec((None, half, dkv), prev(col)), pl.BlockSpec((None, tq, dkv), main(col)),
        pl.BlockSpec((None, half, dkv), nxt(col)), pl.BlockSpec((None, ctx_len, dkv), ctx(col)),
    ]
    return pl.pallas_call(
        functools.partial(_attn_kernel, tq=tq, nct=nct, seq=seq, ctx_len=ctx_len),
        grid=(b, s // tq),
        in_specs=[pl.BlockSpec((None, tq, dq), lambda bi, j: (bi, j, 0))] + kv_specs(kcol) + kv_specs(vcol)
        + [pl.BlockSpec((N_HEADS, LANES), lambda bi, j: (0, 0))],
        out_specs=pl.BlockSpec((None, tq, dq), lambda bi, j: (bi, j, 0)),
        out_shape=jax.ShapeDtypeStruct((b, s, dq), BF16),
        compiler_params=_cparams(("parallel", "parallel")),
        name="swa_attention",
    )(qkv, *([qkv] * 8), sink_b)


def _dwconv3_kernel(um, up, un, cw_ref, cb_ref, g1_ref, g2_ref, v_ref, ext, *, tt, nt, nct, c):
    t = pl.program_id(1)
    prev_ok = jnp.logical_and(t != 0, t != nct)
    next_ok = jnp.logical_and(t != nct - 1, t != nt - 1)
    ext[SUBLANES:SUBLANES + tt, :] = um[...]
    ext[0:SUBLANES, :] = jnp.where(prev_ok, up[...], 0.0)
    ext[SUBLANES + tt:2 * SUBLANES + tt, :] = jnp.where(next_ok, un[...], 0.0)
    y = cb_ref[...]
    for k in range(3):
        y = y + cw_ref[k:k + 1, :] * ext[SUBLANES - 1 + k:SUBLANES - 1 + k + tt, :]
    g1_ref[...] = y[:, 0:c]
    g2_ref[...] = y[:, c:2 * c]
    v_ref[...] = y[:, 2 * c:3 * c]


def _dwconv3(u0, conv_w, conv_b, ctx_len):
    b, s, c3 = u0.shape
    c = c3 // 3
    tt = 256
    assert s % tt == 0 and ctx_len % tt == 0
    nt, nct = s // tt, ctx_len // tt
    r8 = tt // SUBLANES
    last8 = s // SUBLANES - 1
    out = jax.ShapeDtypeStruct((b, s, c), F32)
    ospec = pl.BlockSpec((None, tt, c), lambda bi, t: (bi, t, 0))
    return pl.pallas_call(
        functools.partial(_dwconv3_kernel, tt=tt, nt=nt, nct=nct, c=c),
        grid=(b, nt),
        in_specs=[
            pl.BlockSpec((None, tt, c3), lambda bi, t: (bi, t, 0)),
            pl.BlockSpec((None, SUBLANES, c3), lambda bi, t: (bi, jnp.maximum(t * r8 - 1, 0), 0)),
            pl.BlockSpec((None, SUBLANES, c3), lambda bi, t: (bi, jnp.minimum((t + 1) * r8, last8), 0)),
            pl.BlockSpec((3, c3), lambda bi, t: (0, 0)),
            pl.BlockSpec((1, c3), lambda bi, t: (0, 0)),
        ],
        out_specs=[ospec, ospec, ospec],
        out_shape=[out, out, out],
        scratch_shapes=[pltpu.VMEM((tt + 2 * SUBLANES, c3), F32)],
        compiler_params=_cparams(("parallel", "parallel")),
        name="hyena_dwconv",
    )(u0, u0, u0, conv_w, conv_b.reshape(1, c3))


def _filter_features(seq, ctx_len):
    def feats(length):
        t = jnp.linspace(0.0, 1.0, length, dtype=F32)[:, None]
        omega = (2.0 * math.pi / length) * jnp.arange(length, dtype=F32)[:, None]
        bands = jnp.linspace(1e-4, FILTER_BANDS - 1, FILTER_BANDS, dtype=F32)[None, :]
        return jnp.concatenate([t, jnp.cos(bands * omega), -jnp.sin(bands * omega)], axis=-1)
    z = jnp.concatenate([feats(ctx_len), feats(seq)], axis=0)
    return jnp.pad(z, ((0, 0), (0, 40 - z.shape[1])))


def _filter_kernel(z_ref, w1_ref, b1_ref, w2_ref, b2_ref, w3_ref, b3_ref, fr_ref, w4_ref, dl_ref, hw_ref, sum_ref,
                   *, tm, ctx_len, c):
    j = pl.program_id(0)

    @pl.when(j == 0)
    def _():
        sum_ref[...] = jnp.zeros_like(sum_ref)

    z = z_ref[...]
    fr = fr_ref[...]
    dot = functools.partial(jnp.dot, preferred_element_type=F32, precision=HIGHEST)
    hid = jnp.sin(fr * (dot(z, w1_ref[...]) + b1_ref[...]))
    hid = jnp.sin(fr * (dot(hid, w2_ref[...]) + b2_ref[...]))
    hid = jnp.sin(fr * (dot(hid, w3_ref[...]) + b3_ref[...]))
    h = dot(hid, w4_ref[...])
    window = jnp.exp(-z[:, 0:1] * jnp.abs(dl_ref[...]))
    rows = j * tm + lax.broadcasted_iota(jnp.int32, (tm, c), 0)
    first = jnp.logical_or(rows == 0, rows == ctx_len)
    is_ctx = j * tm < ctx_len
    srow = jnp.where(is_ctx, 0, 1)
    for q in range(4):
        hq = h[:, q * c:(q + 1) * c] * window
        if q % 2 == 1:
            hq = jnp.where(first, 0.0, hq)
        hw_ref[:, q * c:(q + 1) * c] = hq
        part = jnp.sum(jnp.abs(hq), axis=0, keepdims=True)
        for r in range(2):
            sum_ref[r:r + 1, q * c:(q + 1) * c] += jnp.where(srow == r, part, 0.0)


def _hyena_filters(seq, ctx_len, w1, b1, w2, b2, w3, b3, freq, w4, c):
    s = seq + ctx_len
    z = _filter_features(seq, ctx_len)
    tm = 256
    assert ctx_len % tm == 0 and seq % tm == 0
    fw = w1.shape[1]
    w1p = jnp.pad(w1.astype(F32), ((0, 40 - w1.shape[0]), (0, 0)))
    deltas = jnp.linspace(math.log(DECAY_TARGET) / SLOW_DECAY_PCT, math.log(DECAY_TARGET) / FAST_DECAY_PCT, c,
                          dtype=F32)[None, :]
    full = lambda shape: pl.BlockSpec(shape, lambda j: tuple(0 for _ in shape))
    return pl.pallas_call(
        functools.partial(_filter_kernel, tm=tm, ctx_len=ctx_len, c=c),
        grid=(s // tm,),
        in_specs=[
            pl.BlockSpec((tm, 40), lambda j: (j, 0)),
            full((40, fw)), full((1, fw)), full((fw, fw)), full((1, fw)), full((fw, fw)), full((1, fw)),
            full((1, fw)), full((fw, 4 * c)), full((1, c)),
        ],
        out_specs=[pl.BlockSpec((tm, 4 * c), lambda j: (j, 0)), pl.BlockSpec((SUBLANES, 4 * c), lambda j: (0, 0))],
        out_shape=[jax.ShapeDtypeStruct((s, 4 * c), F32), jax.ShapeDtypeStruct((SUBLANES, 4 * c), F32)],
        compiler_params=_cparams(("arbitrary",)),
        name="hyena_filter_mlp",
    )(z, w1p, b1.reshape(1, fw), w2, b2.reshape(1, fw), w3, b3.reshape(1, fw), freq.reshape(1, fw), w4, deltas)


def _stack_hl(m_r, m_i):
    rows = m_r.shape[0]
    g = -(-rows // SUBLANES) * SUBLANES
    pad = lambda a: jnp.pad(a, ((0, g - rows), (0, 0)))
    hi = lambda a: pad(a.astype(BF16))
    lo = lambda a: pad((a - a.astype(BF16).astype(F32)).astype(BF16))
    hl = jnp.concatenate([hi(m_r), hi(m_i), lo(m_r), lo(m_i)], axis=0)
    return hl, hl[:2 * g]


def _fft_tables(seq, ctx_len):
    n1 = FFT_N1
    rc, rl = ctx_len // n1, seq // n1
    n2c, n2l = 2 * rc, 2 * rl
    k2n = n2l + n2c
    r = rc + rl

    def cis(num, den):
        ang = (-2.0 * math.pi / den) * (num % den).astype(F32)
        return jnp.cos(ang), jnp.sin(ang)

    ar = jnp.arange
    fl_r, fl_i = cis(ar(n2l)[:, None] * ar(rl)[None, :], n2l)
    fc_r, fc_i = cis(ar(n2c)[:, None] * ar(rc)[None, :], n2c)
    sl, sc = 1.0 / (n1 * n2l), 1.0 / (n1 * n2c)
    fl_hl, fl_h = (m.astype(F32) for m in _stack_hl(fl_r, fl_i))
    gl_hl, gl_h = (m.astype(F32) for m in _stack_hl(fl_r.T * sl, fl_i.T * sl))
    fc_ri, gc_ri = _stack_ri(fc_r, fc_i), _stack_ri(fc_r.T * sc, fc_i.T * sc)
    tl_r, tl_i = cis(ar(n2l)[:, None] * ar(n1)[None, :], n1 * n2l)
    tc_r, tc_i = cis(ar(n2c)[:, None] * ar(n1)[None, :], n1 * n2c)
    tw_r, tw_i = jnp.concatenate([tl_r, tc_r], 0), jnp.concatenate([tl_i, tc_i], 0)
    bl = lambda a: jnp.broadcast_to(a[:, :, None], a.shape + (LANES,))
    f1_r, f1_i = cis(ar(n1)[:, None] * ar(n1)[None, :], n1)
    f1_hl, f1_h = _stack_hl(f1_r, f1_i)
    return dict(fl_hl=fl_hl, fl_h=fl_h, fc_ri=fc_ri, gl_hl=gl_hl, gl_h=gl_h, gc_ri=gc_ri, f1_hl=f1_hl, f1_h=f1_h,
                ta_r=bl(tw_r.T), ta_i=bl(tw_i.T),
                tb_r=bl(tw_r), tb_i=bl(tw_i),
                k2n=k2n, n2l=n2l, n2c=n2c, r=r, rc=rc, rl=rl)


def _stack_ri(m_r, m_i):
    rows = m_r.shape[0]
    g = -(-rows // SUBLANES) * SUBLANES
    pad = lambda a: jnp.pad(a, ((0, g - rows), (0, 0)))
    return jnp.concatenate([pad(m_r), pad(m_i)], axis=0)


def _cdot6(m_ref, x, rows):
    s = jnp.dot(m_ref[...], x, preferred_element_type=F32, precision=HIGHEST)
    g = m_ref.shape[0] // 2
    return s[0:rows], s[g:g + rows]


def _keep_bf16_bits(x):
    bits = pltpu.bitcast(x, jnp.uint32) & jnp.uint32(0xFFFF0000)
    return pltpu.bitcast(bits, F32)


def _split_f32(x):
    hi = _keep_bf16_bits(x)
    return hi, _keep_bf16_bits(x - hi)


def _cdot3f(fhl_ref, fh_ref, x_hi, x_lo, rows):
    s = jnp.dot(fhl_ref[...], x_hi, preferred_element_type=F32)
    t = jnp.dot(fh_ref[...], x_lo, preferred_element_type=F32)
    g = fh_ref.shape[0] // 2
    return (s[0:rows] + s[2 * g:2 * g + rows] + t[0:rows],
            s[g:g + rows] + s[3 * g:3 * g + rows] + t[g:g + rows])


def _cdot3(fhl_ref, fh_ref, x, rows):
    x_hi = x.astype(BF16)
    x_lo = (x - x_hi.astype(F32)).astype(BF16)
    s = jnp.dot(fhl_ref[...], x_hi, preferred_element_type=F32)
    t = jnp.dot(fh_ref[...], x_lo, preferred_element_type=F32)
    g = fh_ref.shape[0] // 2
    return (s[0:rows] + s[2 * g:2 * g + rows] + t[0:rows],
            s[g:g + rows] + s[3 * g:3 * g + rows] + t[g:g + rows])


FFT_NB1 = 8
FFT_CC = 1024


def _rows(ref, lead, rsel, n):
    blk = ref[lead + (rsel, slice(n, n + 1), slice(None))]
    return blk.reshape(blk.shape[0], blk.shape[2])


def _fft_a_kernel(x_ref, flhl_ref, flh_ref, fc_ref, tr_ref, ti_ref, o_ref, xh, xl, *, packed, cc, rc, n2l, n2c):
    hi, lo = _split_f32(x_ref[...])
    xh[...] = hi
    xl[...] = lo
    lat = lambda lead, rsel, n: _cdot3f(flhl_ref, flh_ref, _rows(xh, lead, rsel, n), _rows(xl, lead, rsel, n), n2l)
    ctx = lambda lead, rsel, n: _cdot6(fc_ref, _rows(x_ref, lead, rsel, n), n2c)
    segs = ((slice(0, n2l), slice(rc, None), lat), (slice(n2l, n2l + n2c), slice(0, rc), ctx))
    for n in range(FFT_NB1):
        for ksel, rsel, cdot in segs:
            if packed:
                rr, ir = cdot((0,), rsel, n)
                ri, ii = cdot((1,), rsel, n)
                a_r, a_i = rr - ii, ir + ri
            else:
                a_r, a_i = cdot((), rsel, n)
            tr, ti = tr_ref[n, ksel, :], ti_ref[n, ksel, :]
            o_r, o_i = [], []
            for l in range(cc // LANES):
                cr, ci = a_r[:, l * LANES:(l + 1) * LANES], a_i[:, l * LANES:(l + 1) * LANES]
                o_r.append(cr * tr - ci * ti)
                o_i.append(cr * ti + ci * tr)
            rows = a_r.shape[0]
            o_ref[0, ksel, n:n + 1, :] = jnp.concatenate(o_r, axis=1).reshape(rows, 1, cc)
            o_ref[1, ksel, n:n + 1, :] = jnp.concatenate(o_i, axis=1).reshape(rows, 1, cc)


def _fft_stage_a(xv, tb, packed, c):
    k2n, r = tb["k2n"], tb["r"]
    cc = min(c, FFT_CC)
    nnb, ncc = FFT_N1 // FFT_NB1, c // cc
    mats = [tb["fl_hl"], tb["fl_h"], tb["fc_ri"]]
    mat_specs = [pl.BlockSpec(m.shape, lambda *a: (0, 0)) for m in mats]
    if packed:
        p = xv.shape[1]
        grid = (p, nnb, ncc)
        x_spec = pl.BlockSpec((2, None, r, FFT_NB1, cc), lambda pi, nb, cb: (0, pi, 0, nb, cb))
        t_spec = pl.BlockSpec((FFT_NB1, k2n, LANES), lambda pi, nb, cb: (nb, 0, 0))
        o_spec = pl.BlockSpec((None, 2, k2n, FFT_NB1, cc), lambda pi, nb, cb: (pi, 0, 0, nb, cb))
        o_shape = jax.ShapeDtypeStruct((p, 2, k2n, FFT_N1, c), F32)
        sem = ("parallel", "parallel", "parallel")
    else:
        grid = (nnb, ncc)
        x_spec = pl.BlockSpec((r, FFT_NB1, cc), lambda nb, cb: (0, nb, cb))
        t_spec = pl.BlockSpec((FFT_NB1, k2n, LANES), lambda nb, cb: (nb, 0, 0))
        o_spec = pl.BlockSpec((2, k2n, FFT_NB1, cc), lambda nb, cb: (0, 0, nb, cb))
        o_shape = jax.ShapeDtypeStruct((2, k2n, FFT_N1, c), F32)
        sem = ("parallel", "parallel")
    return pl.pallas_call(
        functools.partial(_fft_a_kernel, packed=packed, cc=cc, rc=tb["rc"], n2l=tb["n2l"], n2c=tb["n2c"]),
        grid=grid,
        in_specs=[x_spec] + mat_specs + [t_spec, t_spec],
        out_specs=o_spec,
        out_shape=o_shape,
        scratch_shapes=[pltpu.VMEM(((2,) if packed else ()) + (r, FFT_NB1, cc), F32)] * 2,
        compiler_params=_cparams(sem),
        name="hyena_fft_a",
    )(xv, *mats, tb["ta_r"], tb["ta_i"])


def _fft_bf_kernel(a_ref, fhl_ref, fh_ref, o_ref):
    rr, ir = _cdot3(fhl_ref, fh_ref, a_ref[0], FFT_N1)
    ri, ii = _cdot3(fhl_ref, fh_ref, a_ref[1], FFT_N1)
    o_ref[0] = rr - ii
    o_ref[1] = ir + ri


def _fft_stage_b_filter(af, tb, c4):
    k2n = tb["k2n"]
    cc = min(c4, FFT_CC)
    mhl = pl.BlockSpec((4 * FFT_N1, FFT_N1), lambda k, cb: (0, 0))
    mh = pl.BlockSpec((2 * FFT_N1, FFT_N1), lambda k, cb: (0, 0))
    blk = pl.BlockSpec((2, None, FFT_N1, cc), lambda k, cb: (0, k, 0, cb))
    return pl.pallas_call(
        _fft_bf_kernel,
        grid=(k2n, c4 // cc),
        in_specs=[blk, mhl, mh],
        out_specs=blk,
        out_shape=jax.ShapeDtypeStruct(af.shape, F32),
        compiler_params=_cparams(("parallel", "parallel")),
        name="hyena_fft_b_filter",
    )(af, tb["f1_hl"], tb["f1_h"])


def _fft_mid_kernel(a_ref, h0_ref, h1_ref, s0_ref, s1_ref, fhl_ref, fh_ref, tr_ref, ti_ref, o_ref, *, n2l, cc):
    k2 = pl.program_id(1)
    rr, ir = _cdot3(fhl_ref, fh_ref, a_ref[0], FFT_N1)
    ri, ii = _cdot3(fhl_ref, fh_ref, a_ref[1], FFT_N1)
    xr = rr - ii
    xi = ir + ri
    norm = s0_ref[...] + s1_ref[...]
    inv = 1.0 / jnp.where(k2 >= n2l, norm[0:1, :], norm[1:2, :])
    kr = (h0_ref[0] + h1_ref[0]) * inv
    ki = (h0_ref[1] - h1_ref[1]) * inv
    yr = xr * kr - xi * ki
    yi = xr * ki + xi * kr
    r_yr, i_yr = _cdot3(fhl_ref, fh_ref, yr, FFT_N1)
    r_yi, i_yi = _cdot3(fhl_ref, fh_ref, yi, FFT_N1)
    br = r_yr + i_yi
    bi = r_yi - i_yr
    tr, ti = tr_ref[...], ti_ref[...]
    for l in range(cc // LANES):
        sl = slice(l * LANES, (l + 1) * LANES)
        cr, ci = br[:, sl], bi[:, sl]
        o_ref[0, :, sl] = cr * tr + ci * ti
        o_ref[1, :, sl] = ci * tr - cr * ti


def _fft_mid(a, hspec, sums, tb, order, c):
    p = a.shape[0]
    k2n, n2l = tb["k2n"], tb["n2l"]
    cc = min(c, FFT_CC)
    ncc = c // cc
    mhl = pl.BlockSpec((4 * FFT_N1, FFT_N1), lambda pi, k, cb: (0, 0))
    mh = pl.BlockSpec((2 * FFT_N1, FFT_N1), lambda pi, k, cb: (0, 0))
    blk = pl.BlockSpec((None, 2, None, FFT_N1, cc), lambda pi, k, cb: (pi, 0, k, 0, cb))
    hs = lambda side: pl.BlockSpec((2, None, FFT_N1, cc), lambda pi, k, cb: (0, k, 0, (2 * order + side) * ncc + cb))
    ss = lambda side: pl.BlockSpec((SUBLANES, cc), lambda pi, k, cb: (0, (2 * order + side) * ncc + cb))
    tw = pl.BlockSpec((None, FFT_N1, LANES), lambda pi, k, cb: (k, 0, 0))
    return pl.pallas_call(
        functools.partial(_fft_mid_kernel, n2l=n2l, cc=cc),
        grid=(p, k2n, ncc),
        in_specs=[blk, hs(0), hs(1), ss(0), ss(1), mhl, mh, tw, tw],
        out_specs=blk,
        out_shape=jax.ShapeDtypeStruct(a.shape, F32),
        compiler_params=_cparams(("parallel", "parallel", "parallel")),
        name="hyena_fft_mid",
    )(a, hspec, hspec, sums, sums, tb["f1_hl"], tb["f1_h"], tb["tb_r"], tb["tb_i"])


def _fft_inv_kernel(b_ref, glhl_ref, glh_ref, gc_ref, g_ref, v_ref, skip_ref, o_ref, bh, bl, ybuf,
                    *, cc, rc, rl, n2l, n2c):
    hi, lo = _split_f32(b_ref[...])
    bh[...] = hi
    bl[...] = lo
    lat = lambda lead, ksel, n: _cdot3f(glhl_ref, glh_ref, _rows(bh, lead, ksel, n), _rows(bl, lead, ksel, n), rl)
    ctx = lambda lead, ksel, n: _cdot6(gc_ref, _rows(b_ref, lead, ksel, n), rc)
    segs = ((slice(0, n2l), slice(rc, rc + rl), lat), (slice(n2l, n2l + n2c), slice(0, rc), ctx))
    for n in range(FFT_NB1):
        for ksel, rsel, cdot in segs:
            r_br, i_br = cdot((0,), ksel, n)
            r_bi, i_bi = cdot((1,), ksel, n)
            yr = r_br + i_bi
            yi = r_bi - i_br
            ybuf[0, rsel, n:n + 1, :] = yr.reshape(yr.shape[0], 1, cc)
            ybuf[1, rsel, n:n + 1, :] = yi.reshape(yi.shape[0], 1, cc)
    o_ref[...] = (g_ref[...] * (ybuf[...] + skip_ref[...] * v_ref[...])).astype(o_ref.dtype)


def _fft_inverse_gate(bv, gv, vv, skip, tb, c):
    p = bv.shape[0]
    k2n, r = tb["k2n"], tb["r"]
    cc = min(c, FFT_CC)
    mats = [tb["gl_hl"], tb["gl_h"], tb["gc_ri"]]
    mat_specs = [pl.BlockSpec(m.shape, lambda pi, nb, cb: (0, 0)) for m in mats]
    xs = pl.BlockSpec((2, None, r, FFT_NB1, cc), lambda pi, nb, cb: (0, pi, 0, nb, cb))
    return pl.pallas_call(
        functools.partial(_fft_inv_kernel, cc=cc, rc=tb["rc"], rl=tb["rl"], n2l=tb["n2l"], n2c=tb["n2c"]),
        grid=(p, FFT_N1 // FFT_NB1, c // cc),
        in_specs=[pl.BlockSpec((None, 2, k2n, FFT_NB1, cc), lambda pi, nb, cb: (pi, 0, 0, nb, cb))] + mat_specs
        + [xs, xs, pl.BlockSpec((1, cc), lambda pi, nb, cb: (0, cb))],
        out_specs=xs,
        out_shape=jax.ShapeDtypeStruct(gv.shape, F32),
        scratch_shapes=[pltpu.VMEM((2, k2n, FFT_NB1, cc), F32), pltpu.VMEM((2, k2n, FFT_NB1, cc), F32),
                        pltpu.VMEM((2, r, FFT_NB1, cc), F32)],
        compiler_params=_cparams(("parallel", "parallel", "parallel")),
        name="hyena_fft_inv",
    )(bv, *mats, gv, vv, skip.reshape(1, c))


def _hyena_mix(u0, conv_w, conv_b, fw1, fb1, fw2, fb2, fw3, fb3, ffreq, fw4, skip, ctx_len):
    b, s, c3 = u0.shape
    c = c3 // 3
    seq = s - ctx_len
    assert b % 2 == 0 and seq % FFT_N1 == 0 and ctx_len % FFT_N1 == 0
    p = b // 2
    r = s // FFT_N1
    g1, g2, v = _dwconv3(u0, conv_w, conv_b, ctx_len)
    hw, sums = _hyena_filters(seq, ctx_len, fw1, fb1, fw2, fb2, fw3, fb3, ffreq, fw4, c)
    tb = _fft_tables(seq, ctx_len)
    af = _fft_stage_a(hw.reshape(r, FFT_N1, 4 * c), tb, False, 4 * c)
    hspec = _fft_stage_b_filter(af, tb, 4 * c)
    view = lambda t: t.reshape(2, p, r, FFT_N1, c)
    z = v
    for order, gate in ((0, g1), (1, g2)):
        a = _fft_stage_a(view(z), tb, True, c)
        bm = _fft_mid(a, hspec, sums, tb, order, c)
        z = _fft_inverse_gate(bm, view(gate), view(z), skip[order], tb, c).reshape(b, s, c)
    return z


def _router_kernel(h_ref, r_ref, g_ref, hb_ref, *, ne):
    hb_ref[...] = h_ref[...].astype(BF16)
    logits = jnp.dot(h_ref[...], r_ref[...], preferred_element_type=F32, precision=HIGHEST)
    lane = lax.broadcasted_iota(jnp.int32, logits.shape, 1)
    neg = -jnp.inf
    logits = jnp.where(lane < ne, logits, neg)
    m1 = jnp.max(logits, axis=1, keepdims=True)
    i1 = jnp.min(jnp.where(logits == m1, lane, LANES), axis=1, keepdims=True)
    rest = jnp.where(lane == i1, neg, logits)
    m2 = jnp.max(rest, axis=1, keepdims=True)
    i2 = jnp.min(jnp.where(rest == m2, lane, LANES), axis=1, keepdims=True)
    e2 = jnp.exp(m2 - m1)
    w1 = 1.0 / (1.0 + e2)
    g_ref[...] = jnp.where(lane == i1, w1, 0.0) + jnp.where(lane == i2, e2 * w1, 0.0)


def _router(h2, router):
    b, s, d = h2.shape
    ne = router.shape[1]
    tm = _pick_tile(s, (544, 384, 256, 128))
    rp = jnp.pad(router.astype(F32), ((0, 0), (0, LANES - ne)))
    return pl.pallas_call(
        functools.partial(_router_kernel, ne=ne),
        grid=(b, s // tm),
        in_specs=[pl.BlockSpec((None, tm, d), lambda bi, j: (bi, j, 0)), pl.BlockSpec((d, LANES), lambda bi, j: (0, 0))],
        out_specs=[pl.BlockSpec((None, tm, LANES), lambda bi, j: (bi, j, 0)),
                   pl.BlockSpec((None, tm, d), lambda bi, j: (bi, j, 0))],
        out_shape=[jax.ShapeDtypeStruct((b, s, LANES), F32), jax.ShapeDtypeStruct((b, s, d), BF16)],
        compiler_params=_cparams(("parallel", "parallel")),
        name="moe_router",
    )(h2, rp)


def _moe_sparse_kernel(h_ref, x_ref, mod_ref, gate_ref, wg_ref, wu_ref, wd_ref, o_ref, dest, cnt, xe, ye,
                       *, tb, ffn_rows, main_tiles, rb, ctx_len, ne, nf):
    j, e, f = pl.program_id(1), pl.program_id(2), pl.program_id(3)
    half = tb // 2

    @pl.when(jnp.logical_and(e == 0, f == 0))
    def _():
        tri = (lax.broadcasted_iota(jnp.int32, (rb, rb), 0) > lax.broadcasted_iota(jnp.int32, (rb, rb), 1))
        tri = jnp.where(tri, 1.0, 0.0).astype(BF16)
        carry = jnp.zeros((1, LANES), F32)
        for k in range(tb // rb):
            routed = gate_ref[k * rb:(k + 1) * rb, :] > 0.0
            r01 = jnp.where(routed, 1.0, 0.0)
            before = jnp.dot(tri, r01.astype(BF16), preferred_element_type=F32) + carry
            dest[k * rb:(k + 1) * rb, :] = jnp.where(routed, before, -1.0)
            carry = carry + jnp.sum(r01, axis=0, keepdims=True)
        cnt[...] = jnp.broadcast_to(carry, cnt.shape)
        o_ref[...] = jnp.zeros_like(o_ref)

    lane1 = lax.broadcasted_iota(jnp.int32, (1, LANES), 1)
    n_e = jnp.sum(jnp.where(lane1 == e, cnt[0:1, :], 0.0))
    n_tiles = jnp.ceil(n_e * (1.0 / LANES)).astype(jnp.int32)

    lane_h = lax.broadcasted_iota(jnp.int32, (half, LANES), 1)

    def expert_col(ref, rows):
        return jnp.sum(jnp.where(lane_h == e, ref[rows, :], 0.0), axis=1, keepdims=True)

    def one_hot(rows, first, width):
        dcol = expert_col(dest, rows) - first
        lanef = lane_h.astype(F32)
        pieces = [jnp.where(dcol == lanef + float(k * LANES), 1.0, 0.0).astype(BF16) for k in range(width // LANES)]
        return jnp.concatenate(pieces, axis=1)

    def gather(first, width, dst):
        acc = None
        for hb in range(2):
            rows = slice(hb * half, (hb + 1) * half)
            part = lax.dot_general(one_hot(rows, first, width), h_ref[rows, :], (((0,), (0,)), ((), ())),
                                   preferred_element_type=F32)
            acc = part if acc is None else acc + part
        xe[dst, :] = acc.astype(BF16)
        ye[dst, :] = jnp.zeros((width, ye.shape[1]), F32)

    def expert_ffn(dst):
        x = xe[dst, :]
        a = jnp.dot(x, wg_ref[...], preferred_element_type=F32)
        u = jnp.dot(x, wu_ref[...], preferred_element_type=F32)
        act = (a * jax.nn.sigmoid(a) * u).astype(BF16)
        ye[dst, :] += jnp.dot(act, wd_ref[...], preferred_element_type=F32)

    def scatter(first, width, dst):
        y = ye[dst, :].astype(BF16)
        for hb in range(2):
            rows = slice(hb * half, (hb + 1) * half)
            back = jnp.dot(one_hot(rows, first, width), y, preferred_element_type=F32)
            o_ref[rows, :] += expert_col(gate_ref, rows) * back

    main_rows = main_tiles * LANES
    active = n_tiles > 0
    n_extra = jnp.maximum(n_tiles - main_tiles, 0)

    def extra(fn):
        def body(c, carry):
            off = pl.multiple_of(main_rows + c * LANES, LANES)
            fn(off.astype(F32), LANES, pl.ds(off, LANES))
            return carry
        lax.fori_loop(0, n_extra, body, 0)

    def main_ffn():
        below = None
        for i, rows_c in enumerate(ffn_rows):
            sel = n_e <= float(rows_c) if i + 1 < len(ffn_rows) else n_e > float(below)
            if below is not None and i + 1 < len(ffn_rows):
                sel = jnp.logical_and(sel, n_e > float(below))
            pl.when(sel)(functools.partial(expert_ffn, slice(0, rows_c)))
            below = rows_c

    @pl.when(jnp.logical_and(active, f == 0))
    def _():
        gather(0.0, main_rows, slice(0, main_rows))
        extra(gather)

    @pl.when(active)
    def _():
        main_ffn()
        extra(lambda first, width, dst: expert_ffn(dst))

    @pl.when(jnp.logical_and(active, f == nf - 1))
    def _():
        scatter(0.0, main_rows, slice(0, main_rows))
        extra(scatter)

    @pl.when(jnp.logical_and(e == ne - 1, f == nf - 1))
    def _():
        x = x_ref[...]
        is_ctx = _row_is_ctx(j, tb, ctx_len, x.shape)
        o_ref[...] = x + _sel_mod(mod_ref, is_ctx, 5) * o_ref[...]


def _moe_sparse(hb16, x_all, mods_i, gate, wg, wu, wd, ctx_len):
    b, s, d = x_all.shape
    ne, _, f = wg.shape
    tb = _pick_tile(s, (1088, 768, 512, 256))
    max_tiles = -(-tb // LANES)
    ch = max_tiles * LANES
    mean_tiles = -(-tb * TOP_K // (ne * LANES))
    min_tiles, main_tiles = max(mean_tiles - 1, 1), min(mean_tiles + 1, max_tiles)
    half = LANES // 2
    ffn_rows = tuple(sorted({min_tiles * LANES, min_tiles * LANES + half, mean_tiles * LANES, main_tiles * LANES}))
    rb = max(r for r in range(SUBLANES, 257, SUBLANES) if tb % r == 0)
    fc = _pick_tile(f, (896, 512, 256, 128))
    nf = f // fc
    row = lambda bi, j, e, k: (bi, j, 0)
    return pl.pallas_call(
        functools.partial(_moe_sparse_kernel, tb=tb, ffn_rows=ffn_rows, main_tiles=main_tiles, rb=rb,
                          ctx_len=ctx_len, ne=ne, nf=nf),
        grid=(b, s // tb, ne, nf),
        in_specs=[
            pl.BlockSpec((None, tb, d), row),
            pl.BlockSpec((None, tb, d), row, pipeline_mode=pl.Buffered(1)),
            pl.BlockSpec((None, 2, 8, d), lambda bi, j, e, k: (bi, 0, 0, 0)),
            pl.BlockSpec((None, tb, LANES), row),
            pl.BlockSpec((None, d, fc), lambda bi, j, e, k: (e, 0, k)),
            pl.BlockSpec((None, d, fc), lambda bi, j, e, k: (e, 0, k)),
            pl.BlockSpec((None, fc, d), lambda bi, j, e, k: (e, k, 0)),
        ],
        out_specs=pl.BlockSpec((None, tb, d), row),
        out_shape=jax.ShapeDtypeStruct((b, s, d), F32),
        scratch_shapes=[pltpu.VMEM((tb, LANES), F32), pltpu.VMEM((SUBLANES, LANES), F32),
                        pltpu.VMEM((ch, d), BF16), pltpu.VMEM((ch, d), F32)],
        compiler_params=_cparams(("parallel", "parallel", "arbitrary", "arbitrary")),
        name="moe_experts",
    )(hb16, x_all, mods_i, gate, wg.astype(BF16), wu.astype(BF16), wd.astype(BF16))


def _final_norm_kernel(x_ref, g_ref, o_ref):
    x = x_ref[...]
    ms = jnp.mean(x * x, axis=-1, keepdims=True)
    o_ref[...] = (x * lax.rsqrt(ms + EPS)) * g_ref[...]


def _final_norm(x_all, g, ctx_len):
    b, s, d = x_all.shape
    seq = s - ctx_len
    tm = _pick_tile(math.gcd(seq, ctx_len), (512, 256, 128))
    off = ctx_len // tm
    return pl.pallas_call(
        _final_norm_kernel,
        grid=(b, seq // tm),
        in_specs=[pl.BlockSpec((None, tm, d), lambda bi, j: (bi, j + off, 0)), pl.BlockSpec((1, d), lambda bi, j: (0, 0))],
        out_specs=pl.BlockSpec((None, tm, d), lambda bi, j: (bi, j, 0)),
        out_shape=jax.ShapeDtypeStruct((b, seq, d), F32),
        compiler_params=_cparams(("parallel", "parallel")),
        name="final_norm",
    )(x_all, g.reshape(1, d))


def kernel(x, c, ctx, c_ctx, ada_w, ada_b, norm_mix, norm_ffn, norm_final,
           lru_w_in, lru_conv_w, lru_conv_b, lru_w_a, lru_b_a, lru_w_x, lru_b_x, lru_lambda, lru_w_out,
           attn_w_qkv, attn_sinks, attn_w_o,
           hy_w_in, hy_b_in, hy_conv_w, hy_conv_b, hy_f_w1, hy_f_b1, hy_f_w2, hy_f_b2, hy_f_w3, hy_f_b3,
           hy_f_freq, hy_f_w4, hy_skip, hy_w_out, hy_b_out,
           ffn_w_gate, ffn_w_up, ffn_w_down,
           moe_router, moe_w_gate, moe_w_up, moe_w_down):
    depth = ada_w.shape[0]
    ctx_len, seq, d = ctx.shape[1], x.shape[1], x.shape[2]
    x_all = jnp.concatenate([ctx, x], axis=1)
    mods = _mods(c, c_ctx, ada_w, ada_b)
    zero_d = jnp.zeros((d,), F32)
    for i in range(depth):
        j = i // N_MIXERS
        moe = i % 2 == 1
        h_dtype = F32 if moe else BF16
        if i % N_MIXERS == 0:
            gu = _norm_mm(x_all, mods[i], norm_mix[i], lru_w_in[j], jnp.zeros((lru_w_in.shape[2],), F32), ctx_len, F32)
            hs_f, hs_b = _lru_scan(gu, lru_conv_w[j], lru_conv_b[j], lru_w_a[j], lru_b_a[j], lru_w_x[j], lru_b_x[j],
                                   lru_lambda[j], ctx_len)
            x_all, h2 = _lru_mm_res(hs_f, hs_b, gu, x_all, mods[i], norm_ffn[i], lru_w_out[j], ctx_len, h_dtype)
        elif i % N_MIXERS == 1:
            cos_t, sin_t = _rope_tables(seq, ctx_len)
            qkv = _qkv_proj(x_all, mods[i], norm_mix[i], attn_w_qkv[j], cos_t, sin_t, ctx_len)
            o = _attention(qkv, attn_sinks[j], ctx_len)
            x_all, h2 = _mm_res(o, x_all, mods[i], norm_ffn[i], attn_w_o[j], zero_d, ctx_len, h_dtype)
        else:
            u0 = _norm_mm(x_all, mods[i], norm_mix[i], hy_w_in[j], hy_b_in[j], ctx_len, F32)
            z = _hyena_mix(u0, hy_conv_w[j], hy_conv_b[j], hy_f_w1[j], hy_f_b1[j], hy_f_w2[j], hy_f_b2[j],
                           hy_f_w3[j], hy_f_b3[j], hy_f_freq[j], hy_f_w4[j], hy_skip[j], ctx_len)
            x_all, h2 = _mm_res(z, x_all, mods[i], norm_ffn[i], hy_w_out[j], hy_b_out[j], ctx_len, h_dtype)
        if moe:
            gate, hb16 = _router(h2, moe_router[i // 2])
            x_all = _moe_sparse(hb16, x_all, mods[i], gate, moe_w_gate[i // 2], moe_w_up[i // 2], moe_w_down[i // 2],
                                ctx_len)
        else:
            x_all = _ffn_dense(h2, x_all, mods[i], ffn_w_gate[i // 2], ffn_w_up[i // 2], ffn_w_down[i // 2], ctx_len)
    return _final_norm(x_all, norm_final, ctx_len)
```

```python
import functools
import math

import jax
import jax.numpy as jnp
import numpy as np
from jax import lax
from jax.experimental import pallas as pl
from jax.experimental.pallas import tpu as pltpu

F32 = jnp.float32
BF16 = jnp.bfloat16
HIGHEST = lax.Precision.HIGHEST

N_MIXERS = 3
EPS = 1e-6
GRID_W = 64
RNN_BLOCKS = 8
LRU_C = 8.0
N_HEADS = 16
N_KV_HEADS = 4
HEAD_DIM = 64
WINDOW = 128
ROPE_BASE = 10000.0
FILTER_BANDS = 16
DECAY_TARGET = 1e-2
FAST_DECAY_PCT = 0.3
SLOW_DECAY_PCT = 1.5
N_EXPERTS = 8
TOP_K = 2

LANES = 128
SUBLANES = 8
VMEM_LIMIT = 56 * 1024 * 1024
FFT_N1 = 128


def _cparams(sem, vmem=VMEM_LIMIT):
    return pltpu.CompilerParams(dimension_semantics=sem, vmem_limit_bytes=vmem)


def _pick_tile(n, cands):
    for c in cands:
        if n % c == 0:
            return c
    raise ValueError(f"no tile for {n} in {cands}")


def _row_is_ctx(j, tm, ctx_len, shape):
    rows = j * tm + lax.broadcasted_iota(jnp.int32, shape, 0)
    return rows < ctx_len


def _sel_mod(mod_ref, is_ctx, idx):
    return jnp.where(is_ctx, mod_ref[0, idx:idx + 1, :], mod_ref[1, idx:idx + 1, :])


def _adanorm(x, g, shift, scale):
    ms = jnp.mean(x * x, axis=-1, keepdims=True)
    y = x * lax.rsqrt(ms + EPS)
    return (y * g) * (1.0 + scale) + shift


def _mods(c, c_ctx, ada_w, ada_b):
    depth, d, d6 = ada_w.shape
    b = c.shape[0]
    rows = ((b + 1 + SUBLANES - 1) // SUBLANES) * SUBLANES
    cond = jnp.concatenate([c, c_ctx[None, :], jnp.zeros((rows - b - 1, d), F32)], axis=0)
    tn = _pick_tile(d6, (1024, 512, 256, 128))

    def kern(c_ref, w_ref, b_ref, o_ref):
        x = c_ref[...]
        s = x * jax.nn.sigmoid(x)
        o_ref[...] = jnp.dot(s, w_ref[...], preferred_element_type=F32, precision=HIGHEST) + b_ref[...]

    out = pl.pallas_call(
        kern,
        grid=(depth, d6 // tn),
        in_specs=[
            pl.BlockSpec((rows, d), lambda i, n: (0, 0)),
            pl.BlockSpec((None, d, tn), lambda i, n: (i, 0, n)),
            pl.BlockSpec((None, 1, tn), lambda i, n: (i, 0, n)),
        ],
        out_specs=pl.BlockSpec((None, rows, tn), lambda i, n: (i, 0, n)),
        out_shape=jax.ShapeDtypeStruct((depth, rows, d6), F32),
        compiler_params=_cparams(("parallel", "parallel")),
        name="mods",
    )(cond, ada_w, ada_b.reshape(depth, 1, d6))
    m = out.reshape(depth, rows, 6, d)
    lat = m[:, :b]
    ctx = jnp.broadcast_to(m[:, b:b + 1], lat.shape)
    both = jnp.stack([ctx, lat], axis=2)
    return jnp.pad(both, ((0, 0), (0, 0), (0, 0), (0, 2), (0, 0)))


def _norm_mm_kernel(x_ref, mod_ref, g_ref, w_ref, b_ref, o_ref, *, tm, ctx_len):
    j = pl.program_id(1)
    x = x_ref[...]
    is_ctx = _row_is_ctx(j, tm, ctx_len, x.shape)
    h = _adanorm(x, g_ref[...], _sel_mod(mod_ref, is_ctx, 0), _sel_mod(mod_ref, is_ctx, 1))
    acc = jnp.dot(h.astype(BF16), w_ref[...], preferred_element_type=F32) + b_ref[...]
    o_ref[...] = acc.astype(o_ref.dtype)


def _norm_mm(x_all, mods_i, g, w, bias, ctx_len, out_dtype):
    b, s, d = x_all.shape
    n = w.shape[1]
    tm = _pick_tile(s, (544, 384, 256, 128))
    return pl.pallas_call(
        functools.partial(_norm_mm_kernel, tm=tm, ctx_len=ctx_len),
        grid=(b, s // tm),
        in_specs=[
            pl.BlockSpec((None, tm, d), lambda bi, j: (bi, j, 0)),
            pl.BlockSpec((None, 2, 8, d), lambda bi, j: (bi, 0, 0, 0)),
            pl.BlockSpec((1, d), lambda bi, j: (0, 0)),
            pl.BlockSpec((d, n), lambda bi, j: (0, 0)),
            pl.BlockSpec((1, n), lambda bi, j: (0, 0)),
        ],
        out_specs=pl.BlockSpec((None, tm, n), lambda bi, j: (bi, j, 0)),
        out_shape=jax.ShapeDtypeStruct((b, s, n), out_dtype),
        compiler_params=_cparams(("parallel", "parallel")),
        name="norm_mm",
    )(x_all, mods_i, g.reshape(1, d), w.astype(BF16), bias.reshape(1, n).astype(F32))


def _mm_res_body(y_bf16, x_ref, mod_ref, g_ref, w_ref, b_ref, xo_ref, ho_ref, tm, ctx_len):
    j = pl.program_id(1)
    x = x_ref[...]
    is_ctx = _row_is_ctx(j, tm, ctx_len, x.shape)
    acc = jnp.dot(y_bf16, w_ref[...], preferred_element_type=F32) + b_ref[...]
    x_new = x + _sel_mod(mod_ref, is_ctx, 2) * acc
    xo_ref[...] = x_new
    h2 = _adanorm(x_new, g_ref[...], _sel_mod(mod_ref, is_ctx, 3), _sel_mod(mod_ref, is_ctx, 4))
    ho_ref[...] = h2.astype(ho_ref.dtype)


def _mm_res_kernel(y_ref, x_ref, mod_ref, g_ref, w_ref, b_ref, xo_ref, ho_ref, *, tm, ctx_len):
    _mm_res_body(y_ref[...].astype(BF16), x_ref, mod_ref, g_ref, w_ref, b_ref, xo_ref, ho_ref, tm, ctx_len)


def _lru_mm_res_kernel(hf_ref, hb_ref, gate_ref, x_ref, mod_ref, g_ref, w_ref, b_ref, xo_ref, ho_ref, *, tm, ctx_len):
    y = (hf_ref[...] + hb_ref[...]) * jax.nn.gelu(gate_ref[...])
    _mm_res_body(y.astype(BF16), x_ref, mod_ref, g_ref, w_ref, b_ref, xo_ref, ho_ref, tm, ctx_len)


def _mm_res_call(kern, ys, y_specs, x_all, mods_i, g_ffn, w, bias, ctx_len, tm, h_dtype):
    b, s, d = x_all.shape
    k = w.shape[0]
    return pl.pallas_call(
        functools.partial(kern, tm=tm, ctx_len=ctx_len),
        grid=(b, s // tm),
        in_specs=y_specs + [
            pl.BlockSpec((None, tm, d), lambda bi, j: (bi, j, 0)),
            pl.BlockSpec((None, 2, 8, d), lambda bi, j: (bi, 0, 0, 0)),
            pl.BlockSpec((1, d), lambda bi, j: (0, 0)),
            pl.BlockSpec((k, d), lambda bi, j: (0, 0)),
            pl.BlockSpec((1, d), lambda bi, j: (0, 0)),
        ],
        out_specs=[
            pl.BlockSpec((None, tm, d), lambda bi, j: (bi, j, 0)),
            pl.BlockSpec((None, tm, d), lambda bi, j: (bi, j, 0)),
        ],
        out_shape=[jax.ShapeDtypeStruct((b, s, d), F32), jax.ShapeDtypeStruct((b, s, d), h_dtype)],
        compiler_params=_cparams(("parallel", "parallel")),
        name="mm_res",
    )(*ys, x_all, mods_i, g_ffn.reshape(1, d), w.astype(BF16), bias.reshape(1, d).astype(F32))


def _mm_res(y, x_all, mods_i, g_ffn, w, bias, ctx_len, h_dtype=BF16):
    b, s, d = x_all.shape
    k = w.shape[0]
    tm = _pick_tile(s, (544, 384, 256, 128))
    spec = [pl.BlockSpec((None, tm, k), lambda bi, j: (bi, j, 0))]
    return _mm_res_call(_mm_res_kernel, [y], spec, x_all, mods_i, g_ffn, w, bias, ctx_len, tm, h_dtype)


def _lru_mm_res(hs_f, hs_b, gu, x_all, mods_i, g_ffn, w, ctx_len, h_dtype=BF16):
    b, s, d = x_all.shape
    k = w.shape[0]
    tm = _pick_tile(s, (544, 384, 256, 128))
    specs = [
        pl.BlockSpec((None, tm, k), lambda bi, j: (bi, j, 0)),
        pl.BlockSpec((None, tm, k), lambda bi, j: (bi, j, 0)),
        pl.BlockSpec((None, tm, k), lambda bi, j: (bi, j, 0)),
    ]
    return _mm_res_call(_lru_mm_res_kernel, [hs_f, hs_b, gu], specs, x_all, mods_i, g_ffn, w,
                        jnp.zeros((d,), F32), ctx_len, tm, h_dtype)


def _ffn_kernel(h_ref, x_ref, mod_ref, wg_ref, wu_ref, wd_ref, o_ref, *, tm, ctx_len):
    j = pl.program_id(1)
    h = h_ref[...]
    a = jnp.dot(h, wg_ref[...], preferred_element_type=F32)
    u = jnp.dot(h, wu_ref[...], preferred_element_type=F32)
    act = (a * jax.nn.sigmoid(a) * u).astype(BF16)
    y = jnp.dot(act, wd_ref[...], preferred_element_type=F32)
    x = x_ref[...]
    is_ctx = _row_is_ctx(j, tm, ctx_len, x.shape)
    o_ref[...] = x + _sel_mod(mod_ref, is_ctx, 5) * y


def _ffn_dense(h2, x_all, mods_i, wg, wu, wd, ctx_len):
    b, s, d = x_all.shape
    f = wg.shape[1]
    tm = _pick_tile(s, (544, 384, 256, 128))
    resident = dict(pipeline_mode=pl.Buffered(1))
    return pl.pallas_call(
        functools.partial(_ffn_kernel, tm=tm, ctx_len=ctx_len),
        grid=(b, s // tm),
        in_specs=[
            pl.BlockSpec((None, tm, d), lambda bi, j: (bi, j, 0)),
            pl.BlockSpec((None, tm, d), lambda bi, j: (bi, j, 0)),
            pl.BlockSpec((None, 2, 8, d), lambda bi, j: (bi, 0, 0, 0)),
            pl.BlockSpec((d, f), lambda bi, j: (0, 0), **resident),
            pl.BlockSpec((d, f), lambda bi, j: (0, 0), **resident),
            pl.BlockSpec((f, d), lambda bi, j: (0, 0), **resident),
        ],
        out_specs=pl.BlockSpec((None, tm, d), lambda bi, j: (bi, j, 0)),
        out_shape=jax.ShapeDtypeStruct((b, s, d), F32),
        compiler_params=_cparams(("parallel", "parallel")),
        name="ffn_dense",
    )(h2, x_all, mods_i, wg.astype(BF16), wu.astype(BF16), wd.astype(BF16))


def _lru_bwd_tile(j, nt, nct):
    return jnp.where(j < nct, nct - 1 - j, nt - 1 - (j - nct))


def _lru_scan_kernel(um_f, up_f, un_f, um_b, up_b, un_b, cw_ref, cb_ref, wax_f, wax_b, ba_ref, bx_ref, lam_ref,
                     hsf_ref, hsb_ref, ext, af_s, bf_s, ab_s, bb_s, carry, *, tt, nt, nct, wl, nb):
    j = pl.program_id(1)

    @pl.when(j == 0)
    def _():
        carry[...] = jnp.zeros_like(carry)

    def coeffs(um, up, un, t, d, wax_ref, a_s, b_s):
        prev_ok = jnp.logical_and(t != 0, t != nct)
        next_ok = jnp.logical_and(t != nct - 1, t != nt - 1)
        ext[:, SUBLANES:SUBLANES + tt, :] = um[...]
        ext[:, 0:SUBLANES, :] = jnp.where(prev_ok, up[...], 0.0)
        ext[:, SUBLANES + tt:2 * SUBLANES + tt, :] = jnp.where(next_ok, un[...], 0.0)
        uc = cb_ref[...].reshape(1, 1, wl)
        for k in range(4):
            uc = uc + cw_ref[k:k + 1, :].reshape(1, 1, wl) * ext[:, SUBLANES - 2 + k:SUBLANES - 2 + k + tt, :]
        sp = jax.nn.softplus(-lam_ref[d:d + 1, :])
        for hb in range(wl // LANES):
            sl = slice(hb * LANES, (hb + 1) * LANES)
            ub = uc[:, :, sl].reshape(nb * tt, LANES)
            pre = jnp.dot(ub.astype(BF16), wax_ref[hb], preferred_element_type=F32)
            r = jax.nn.sigmoid(pre[:, :LANES] + ba_ref[d:d + 1, sl])
            i = jax.nn.sigmoid(pre[:, LANES:] + bx_ref[d:d + 1, sl])
            log_a = (-LRU_C) * r * sp[:, sl]
            a = jnp.exp(log_a)
            b = jnp.sqrt(-jnp.tanh(log_a) * (a * a + 1.0)) * (i * ub)
            a_s[:, :, sl] = a.reshape(nb, tt, LANES)
            b_s[:, :, sl] = b.reshape(nb, tt, LANES)

    tb = _lru_bwd_tile(j, nt, nct)
    coeffs(um_f, up_f, un_f, j, 0, wax_f, af_s, bf_s)
    coeffs(um_b, up_b, un_b, tb, 1, wax_b, ab_s, bb_s)

    def step(t, c):
        hf, hb = c
        hf = af_s[:, pl.ds(t, 1), :].reshape(nb, wl) * hf + bf_s[:, pl.ds(t, 1), :].reshape(nb, wl)
        hsf_ref[:, pl.ds(t, 1), :] = hf.reshape(nb, 1, wl)
        r = tt - 1 - t
        hb = ab_s[:, pl.ds(r, 1), :].reshape(nb, wl) * hb + bb_s[:, pl.ds(r, 1), :].reshape(nb, wl)
        hsb_ref[:, pl.ds(r, 1), :] = hb.reshape(nb, 1, wl)
        return hf, hb

    hf, hb = lax.fori_loop(0, tt, step, (carry[0], carry[1]), unroll=8)
    carry[0] = hf
    carry[1] = hb


def _lru_scan(gu, conv_w, conv_b, w_a, b_a, w_x, b_x, lam, ctx_len):
    nb, s, d2 = gu.shape
    d = d2 // 2
    tt = 128
    assert s % tt == 0 and ctx_len % tt == 0 and d % (RNN_BLOCKS * LANES) == 0 and d // RNN_BLOCKS == LANES
    nt, nct = s // tt, ctx_len // tt
    wl = 4 * LANES
    ncg = d // wl
    coff = d // wl
    r8 = tt // SUBLANES
    last8 = s // SUBLANES - 1
    wax = jnp.concatenate([w_a, w_x], axis=-1).astype(BF16)

    def main_f(cg, j): return (0, j, coff + cg)
    def prev_f(cg, j): return (0, jnp.maximum(j * r8 - 1, 0), coff + cg)
    def next_f(cg, j): return (0, jnp.minimum((j + 1) * r8, last8), coff + cg)
    def tb_(j): return _lru_bwd_tile(j, nt, nct)
    def main_b(cg, j): return (0, tb_(j), coff + cg)
    def prev_b(cg, j): return (0, jnp.maximum(tb_(j) * r8 - 1, 0), coff + cg)
    def next_b(cg, j): return (0, jnp.minimum((tb_(j) + 1) * r8, last8), coff + cg)

    big = lambda im: pl.BlockSpec((nb, tt, wl), im)
    halo = lambda im: pl.BlockSpec((nb, SUBLANES, wl), im)
    vec2 = pl.BlockSpec((2, wl), lambda cg, j: (0, cg))
    return pl.pallas_call(
        functools.partial(_lru_scan_kernel, tt=tt, nt=nt, nct=nct, wl=wl, nb=nb),
        grid=(ncg, nt),
        in_specs=[
            big(main_f), halo(prev_f), halo(next_f), big(main_b), halo(prev_b), halo(next_b),
            pl.BlockSpec((4, wl), lambda cg, j: (0, cg)),
            pl.BlockSpec((1, wl), lambda cg, j: (0, cg)),
            pl.BlockSpec((None, wl // LANES, LANES, 2 * LANES), lambda cg, j: (0, cg, 0, 0)),
            pl.BlockSpec((None, wl // LANES, LANES, 2 * LANES), lambda cg, j: (1, cg, 0, 0)),
            vec2, vec2, vec2,
        ],
        out_specs=[
            pl.BlockSpec((nb, tt, wl), lambda cg, j: (0, j, cg)),
            pl.BlockSpec((nb, tt, wl), lambda cg, j: (0, tb_(j), cg)),
        ],
        out_shape=[jax.ShapeDtypeStruct((nb, s, d), F32), jax.ShapeDtypeStruct((nb, s, d), F32)],
        scratch_shapes=[
            pltpu.VMEM((nb, tt + 2 * SUBLANES, wl), F32),
            pltpu.VMEM((nb, tt, wl), F32), pltpu.VMEM((nb, tt, wl), F32),
            pltpu.VMEM((nb, tt, wl), F32), pltpu.VMEM((nb, tt, wl), F32),
            pltpu.VMEM((2, nb, wl), F32),
        ],
        compiler_params=_cparams(("parallel", "arbitrary")),
        name="lru_scan",
    )(gu, gu, gu, gu, gu, gu, conv_w, conv_b.reshape(1, d), wax, wax, b_a, b_x, lam)


def _rope_tables(seq, ctx_len):
    q = HEAD_DIM // 4
    inv_freq = ROPE_BASE ** (-jnp.arange(q, dtype=F32) / q)
    pos = jnp.arange(seq, dtype=jnp.int32)
    row = (pos // GRID_W).astype(F32)[:, None] * inv_freq
    col = (pos % GRID_W).astype(F32)[:, None] * inv_freq
    ang = jnp.concatenate([row, row, col, col], axis=-1)
    sign = jnp.concatenate([-jnp.ones((q,), F32), jnp.ones((q,), F32)] * 2)
    cos = jnp.concatenate([jnp.ones((ctx_len, HEAD_DIM), F32), jnp.cos(ang)], axis=0)
    sin = jnp.concatenate([jnp.zeros((ctx_len, HEAD_DIM), F32), jnp.sin(ang) * sign], axis=0)
    return jnp.tile(cos, (1, 2)), jnp.tile(sin, (1, 2))


def _qkv_kernel(x_ref, mod_ref, g_ref, w_ref, cos_ref, sin_ref, o_ref, *, tm, ctx_len, n_q, n_rope):
    j = pl.program_id(1)
    x = x_ref[...]
    is_ctx = _row_is_ctx(j, tm, ctx_len, x.shape)
    h = _adanorm(x, g_ref[...], _sel_mod(mod_ref, is_ctx, 0), _sel_mod(mod_ref, is_ctx, 1))
    acc = jnp.dot(h.astype(BF16), w_ref[...], preferred_element_type=F32)
    cos = cos_ref[...]
    sin = sin_ref[...]
    lane = lax.broadcasted_iota(jnp.int32, (tm, LANES), 1)
    first_half = (lane % (HEAD_DIM // 2)) < (HEAD_DIM // 4)
    q_scale = HEAD_DIM ** -0.5
    for c in range(acc.shape[1] // LANES):
        v = acc[:, c * LANES:(c + 1) * LANES]
        if c < n_rope:
            partner = jnp.where(first_half, pltpu.roll(v, LANES - HEAD_DIM // 4, 1), pltpu.roll(v, HEAD_DIM // 4, 1))
            v = v * cos + partner * sin
            if c < n_q:
                v = v * q_scale
        o_ref[:, c * LANES:(c + 1) * LANES] = v.astype(o_ref.dtype)


def _qkv_proj(x_all, mods_i, g, w, cos_t, sin_t, ctx_len):
    b, s, d = x_all.shape
    n = w.shape[1]
    tm = _pick_tile(s, (544, 384, 256, 128))
    n_q = N_HEADS * HEAD_DIM // LANES
    n_rope = (N_HEADS + N_KV_HEADS) * HEAD_DIM // LANES
    return pl.pallas_call(
        functools.partial(_qkv_kernel, tm=tm, ctx_len=ctx_len, n_q=n_q, n_rope=n_rope),
        grid=(b, s // tm),
        in_specs=[
            pl.BlockSpec((None, tm, d), lambda bi, j: (bi, j, 0)),
            pl.BlockSpec((None, 2, 8, d), lambda bi, j: (bi, 0, 0, 0)),
            pl.BlockSpec((1, d), lambda bi, j: (0, 0)),
            pl.BlockSpec((d, n), lambda bi, j: (0, 0)),
            pl.BlockSpec((tm, LANES), lambda bi, j: (j, 0)),
            pl.BlockSpec((tm, LANES), lambda bi, j: (j, 0)),
        ],
        out_specs=pl.BlockSpec((None, tm, n), lambda bi, j: (bi, j, 0)),
        out_shape=jax.ShapeDtypeStruct((b, s, n), BF16),
        compiler_params=_cparams(("parallel", "parallel")),
        name="qkv_proj",
    )(x_all, mods_i, g.reshape(1, d), w.astype(BF16), cos_t, sin_t)


def _attn_kernel(q_ref, kp_ref, km_ref, kn_ref, kc_ref, vp_ref, vm_ref, vn_ref, vc_ref, sink_ref, o_ref,
                 *, tq, nct, seq, ctx_len):
    j = pl.program_id(1)
    k_all = jnp.concatenate([kp_ref[...], km_ref[...], kn_ref[...], kc_ref[...]], axis=0)
    v_all = jnp.concatenate([vp_ref[...], vm_ref[...], vn_ref[...], vc_ref[...]], axis=0)
    n_loc = 2 * tq
    n_keys = n_loc + ctx_len
    row = lax.broadcasted_iota(jnp.int32, (tq, n_keys), 0)
    col = lax.broadcasted_iota(jnp.int32, (tq, n_keys), 1)
    p0 = (j - nct) * tq
    kpos = p0 - tq // 2 + col
    band = jnp.abs(row - (col - tq // 2)) <= WINDOW
    in_seq = jnp.logical_and(kpos >= 0, kpos < seq)
    valid_loc = jnp.logical_and(jnp.logical_and(band, in_seq), j >= nct)
    valid = jnp.logical_or(col >= n_loc, valid_loc)
    group = N_HEADS // N_KV_HEADS
    outs = []
    for h in range(N_HEADS):
        kvh = h // group
        qh = q_ref[:, h * HEAD_DIM:(h + 1) * HEAD_DIM]
        kk = k_all[:, kvh * HEAD_DIM:(kvh + 1) * HEAD_DIM]
        vv = v_all[:, kvh * HEAD_DIM:(kvh + 1) * HEAD_DIM]
        s = lax.dot_general(qh, kk, (((1,), (1,)), ((), ())), preferred_element_type=F32)
        s = jnp.where(valid, s, -1e30)
        sink = sink_ref[h:h + 1, 0:1]
        m = jnp.maximum(jnp.max(s, axis=1, keepdims=True), sink)
        p = jnp.exp(s - m)
        denom = jnp.sum(p, axis=1, keepdims=True) + jnp.exp(sink - m)
        o = jnp.dot(p.astype(BF16), vv, preferred_element_type=F32)
        outs.append((o / denom).astype(o_ref.dtype))
    o_ref[...] = jnp.concatenate(outs, axis=1)


def _attention(qkv, sinks, ctx_len):
    b, s, _ = qkv.shape
    seq = s - ctx_len
    tq = 2 * WINDOW
    half = tq // 2
    assert ctx_len % tq == 0 and seq % tq == 0
    nct = ctx_len // tq
    dq = N_HEADS * HEAD_DIM
    dkv = N_KV_HEADS * HEAD_DIM
    kcol, vcol = dq // dkv, dq // dkv + 1
    last_half = s // half - 1
    sink_b = jnp.broadcast_to(sinks.astype(F32)[:, None], (N_HEADS, LANES))

    def prev(col): return lambda bi, j: (bi, jnp.maximum(2 * j - 1, 0), col)
    def main(col): return lambda bi, j: (bi, j, col)
    def nxt(col): return lambda bi, j: (bi, jnp.minimum(2 * j + 2, last_half), col)
    def ctx(col): return lambda bi, j: (bi, 0, col)
    kv_specs = lambda col: [
        pl.BlockSpec((None, half, dkv), prev(col)), pl.BlockSpec((None, tq, dkv), main(col)),
        pl.BlockSpec((None, half, dkv), nxt(col)), pl.BlockSpec((None, ctx_len, dkv), ctx(col)),
    ]
    return pl.pallas_call(
        functools.partial(_attn_kernel, tq=tq, nct=nct, seq=seq, ctx_len=ctx_len),
        grid=(b, s // tq),
        in_specs=[pl.BlockSpec((None, tq, dq), lambda bi, j: (bi, j, 0))] + kv_specs(kcol) + kv_specs(vcol)
        + [pl.BlockSpec((N_HEADS, LANES), lambda bi, j: (0, 0))],
        out_specs=pl.BlockSpec((None, tq, dq), lambda bi, j: (bi, j, 0)),
        out_shape=jax.ShapeDtypeStruct((b, s, dq), BF16),
        compiler_params=_cparams(("parallel", "parallel")),
        name="swa_attention",
    )(qkv, *([qkv] * 8), sink_b)


def _dwconv3_kernel(um, up, un, cw_ref, cb_ref, g1_ref, g2_ref, v_ref, ext, *, tt, nt, nct, c):
    t = pl.program_id(1)
    prev_ok = jnp.logical_and(t != 0, t != nct)
    next_ok = jnp.logical_and(t != nct - 1, t != nt - 1)
    ext[SUBLANES:SUBLANES + tt, :] = um[...]
    ext[0:SUBLANES, :] = jnp.where(prev_ok, up[...], 0.0)
    ext[SUBLANES + tt:2 * SUBLANES + tt, :] = jnp.where(next_ok, un[...], 0.0)
    y = cb_ref[...]
    for k in range(3):
        y = y + cw_ref[k:k + 1, :] * ext[SUBLANES - 1 + k:SUBLANES - 1 + k + tt, :]
    g1_ref[...] = y[:, 0:c]
    g2_ref[...] = y[:, c:2 * c]
    v_ref[...] = y[:, 2 * c:3 * c]


def _dwconv3(u0, conv_w, conv_b, ctx_len):
    b, s, c3 = u0.shape
    c = c3 // 3
    tt = 256
    assert s % tt == 0 and ctx_len % tt == 0
    nt, nct = s // tt, ctx_len // tt
    r8 = tt // SUBLANES
    last8 = s // SUBLANES - 1
    out = jax.ShapeDtypeStruct((b, s, c), F32)
    ospec = pl.BlockSpec((None, tt, c), lambda bi, t: (bi, t, 0))
    return pl.pallas_call(
        functools.partial(_dwconv3_kernel, tt=tt, nt=nt, nct=nct, c=c),
        grid=(b, nt),
        in_specs=[
            pl.BlockSpec((None, tt, c3), lambda bi, t: (bi, t, 0)),
            pl.BlockSpec((None, SUBLANES, c3), lambda bi, t: (bi, jnp.maximum(t * r8 - 1, 0), 0)),
            pl.BlockSpec((None, SUBLANES, c3), lambda bi, t: (bi, jnp.minimum((t + 1) * r8, last8), 0)),
            pl.BlockSpec((3, c3), lambda bi, t: (0, 0)),
            pl.BlockSpec((1, c3), lambda bi, t: (0, 0)),
        ],
        out_specs=[ospec, ospec, ospec],
        out_shape=[out, out, out],
        scratch_shapes=[pltpu.VMEM((tt + 2 * SUBLANES, c3), F32)],
        compiler_params=_cparams(("parallel", "parallel")),
        name="hyena_dwconv",
    )(u0, u0, u0, conv_w, conv_b.reshape(1, c3))


def _filter_features(seq, ctx_len):
    def feats(length):
        t = jnp.linspace(0.0, 1.0, length, dtype=F32)[:, None]
        omega = (2.0 * math.pi / length) * jnp.arange(length, dtype=F32)[:, None]
        bands = jnp.linspace(1e-4, FILTER_BANDS - 1, FILTER_BANDS, dtype=F32)[None, :]
        return jnp.concatenate([t, jnp.cos(bands * omega), -jnp.sin(bands * omega)], axis=-1)
    z = jnp.concatenate([feats(ctx_len), feats(seq)], axis=0)
    return jnp.pad(z, ((0, 0), (0, 40 - z.shape[1])))


def _filter_kernel(z_ref, w1_ref, b1_ref, w2_ref, b2_ref, w3_ref, b3_ref, fr_ref, w4_ref, dl_ref, hw_ref, sum_ref,
                   *, tm, ctx_len, c):
    j = pl.program_id(0)

    @pl.when(j == 0)
    def _():
        sum_ref[...] = jnp.zeros_like(sum_ref)

    z = z_ref[...]
    fr = fr_ref[...]
    dot = functools.partial(jnp.dot, preferred_element_type=F32, precision=HIGHEST)
    hid = jnp.sin(fr * (dot(z, w1_ref[...]) + b1_ref[...]))
    hid = jnp.sin(fr * (dot(hid, w2_ref[...]) + b2_ref[...]))
    hid = jnp.sin(fr * (dot(hid, w3_ref[...]) + b3_ref[...]))
    h = dot(hid, w4_ref[...])
    window = jnp.exp(-z[:, 0:1] * jnp.abs(dl_ref[...]))
    rows = j * tm + lax.broadcasted_iota(jnp.int32, (tm, c), 0)
    first = jnp.logical_or(rows == 0, rows == ctx_len)
    is_ctx = j * tm < ctx_len
    srow = jnp.where(is_ctx, 0, 1)
    for q in range(4):
        hq = h[:, q * c:(q + 1) * c] * window
        if q % 2 == 1:
            hq = jnp.where(first, 0.0, hq)
        hw_ref[:, q * c:(q + 1) * c] = hq
        part = jnp.sum(jnp.abs(hq), axis=0, keepdims=True)
        for r in range(2):
            sum_ref[r:r + 1, q * c:(q + 1) * c] += jnp.where(srow == r, part, 0.0)


def _hyena_filters(seq, ctx_len, w1, b1, w2, b2, w3, b3, freq, w4, c):
    s = seq + ctx_len
    z = _filter_features(seq, ctx_len)
    tm = 256
    assert ctx_len % tm == 0 and seq % tm == 0
    fw = w1.shape[1]
    w1p = jnp.pad(w1.astype(F32), ((0, 40 - w1.shape[0]), (0, 0)))
    deltas = jnp.linspace(math.log(DECAY_TARGET) / SLOW_DECAY_PCT, math.log(DECAY_TARGET) / FAST_DECAY_PCT, c,
                          dtype=F32)[None, :]
    full = lambda shape: pl.BlockSpec(shape, lambda j: tuple(0 for _ in shape))
    return pl.pallas_call(
        functools.partial(_filter_kernel, tm=tm, ctx_len=ctx_len, c=c),
        grid=(s // tm,),
        in_specs=[
            pl.BlockSpec((tm, 40), lambda j: (j, 0)),
            full((40, fw)), full((1, fw)), full((fw, fw)), full((1, fw)), full((fw, fw)), full((1, fw)),
            full((1, fw)), full((fw, 4 * c)), full((1, c)),
        ],
        out_specs=[pl.BlockSpec((tm, 4 * c), lambda j: (j, 0)), pl.BlockSpec((SUBLANES, 4 * c), lambda j: (0, 0))],
        out_shape=[jax.ShapeDtypeStruct((s, 4 * c), F32), jax.ShapeDtypeStruct((SUBLANES, 4 * c), F32)],
        compiler_params=_cparams(("arbitrary",)),
        name="hyena_filter_mlp",
    )(z, w1p, b1.reshape(1, fw), w2, b2.reshape(1, fw), w3, b3.reshape(1, fw), freq.reshape(1, fw), w4, deltas)


def _stack_hl(m_r, m_i):
    rows = m_r.shape[0]
    g = -(-rows // SUBLANES) * SUBLANES
    pad = lambda a: jnp.pad(a, ((0, g - rows), (0, 0)))
    hi = lambda a: pad(a.astype(BF16))
    lo = lambda a: pad((a - a.astype(BF16).astype(F32)).astype(BF16))
    hl = jnp.concatenate([hi(m_r), hi(m_i), lo(m_r), lo(m_i)], axis=0)
    return hl, hl[:2 * g]


def _fft_tables(seq, ctx_len):
    n1 = FFT_N1
    rc, rl = ctx_len // n1, seq // n1
    n2c, n2l = 2 * rc, 2 * rl
    k2n = n2l + n2c
    r = rc + rl

    def cis(num, den):
        ang = (-2.0 * math.pi / den) * (num % den).astype(F32)
        return jnp.cos(ang), jnp.sin(ang)

    ar = jnp.arange
    fl_r, fl_i = cis(ar(n2l)[:, None] * ar(rl)[None, :], n2l)
    fc_r, fc_i = cis(ar(n2c)[:, None] * ar(rc)[None, :], n2c)
    sl, sc = 1.0 / (n1 * n2l), 1.0 / (n1 * n2c)
    fl_hl, fl_h = (m.astype(F32) for m in _stack_hl(fl_r, fl_i))
    gl_hl, gl_h = (m.astype(F32) for m in _stack_hl(fl_r.T * sl, fl_i.T * sl))
    fc_ri, gc_ri = _stack_ri(fc_r, fc_i), _stack_ri(fc_r.T * sc, fc_i.T * sc)
    tl_r, tl_i = cis(ar(n2l)[:, None] * ar(n1)[None, :], n1 * n2l)
    tc_r, tc_i = cis(ar(n2c)[:, None] * ar(n1)[None, :], n1 * n2c)
    tw_r, tw_i = jnp.concatenate([tl_r, tc_r], 0), jnp.concatenate([tl_i, tc_i], 0)
    bl = lambda a: jnp.broadcast_to(a[:, :, None], a.shape + (LANES,))
    f1_r, f1_i = cis(ar(n1)[:, None] * ar(n1)[None, :], n1)
    f1_hl, f1_h = _stack_hl(f1_r, f1_i)
    return dict(fl_hl=fl_hl, fl_h=fl_h, fc_ri=fc_ri, gl_hl=gl_hl, gl_h=gl_h, gc_ri=gc_ri, f1_hl=f1_hl, f1_h=f1_h,
                ta_r=bl(tw_r.T), ta_i=bl(tw_i.T),
                tb_r=bl(tw_r), tb_i=bl(tw_i),
                k2n=k2n, n2l=n2l, n2c=n2c, r=r, rc=rc, rl=rl)


def _stack_ri(m_r, m_i):
    rows = m_r.shape[0]
    g = -(-rows // SUBLANES) * SUBLANES
    pad = lambda a: jnp.pad(a, ((0, g - rows), (0, 0)))
    return jnp.concatenate([pad(m_r), pad(m_i)], axis=0)


def _cdot6(m_ref, x, rows):
    s = jnp.dot(m_ref[...], x, preferred_element_type=F32, precision=HIGHEST)
    g = m_ref.shape[0] // 2
    return s[0:rows], s[g:g + rows]


def _keep_bf16_bits(x):
    bits = pltpu.bitcast(x, jnp.uint32) & jnp.uint32(0xFFFF0000)
    return pltpu.bitcast(bits, F32)


def _split_f32(x):
    hi = _keep_bf16_bits(x)
    return hi, _keep_bf16_bits(x - hi)


def _cdot3f(fhl_ref, fh_ref, x_hi, x_lo, rows):
    s = jnp.dot(fhl_ref[...], x_hi, preferred_element_type=F32)
    t = jnp.dot(fh_ref[...], x_lo, preferred_element_type=F32)
    g = fh_ref.shape[0] // 2
    return (s[0:rows] + s[2 * g:2 * g + rows] + t[0:rows],
            s[g:g + rows] + s[3 * g:3 * g + rows] + t[g:g + rows])


def _cdot3(fhl_ref, fh_ref, x, rows):
    x_hi = x.astype(BF16)
    x_lo = (x - x_hi.astype(F32)).astype(BF16)
    s = jnp.dot(fhl_ref[...], x_hi, preferred_element_type=F32)
    t = jnp.dot(fh_ref[...], x_lo, preferred_element_type=F32)
    g = fh_ref.shape[0] // 2
    return (s[0:rows] + s[2 * g:2 * g + rows] + t[0:rows],
            s[g:g + rows] + s[3 * g:3 * g + rows] + t[g:g + rows])


FFT_NB1 = 8
FFT_CC = 1024


def _rows(ref, lead, rsel, n):
    blk = ref[lead + (rsel, slice(n, n + 1), slice(None))]
    return blk.reshape(blk.shape[0], blk.shape[2])


def _fft_a_kernel(x_ref, flhl_ref, flh_ref, fc_ref, tr_ref, ti_ref, o_ref, xh, xl, *, packed, cc, rc, n2l, n2c):
    hi, lo = _split_f32(x_ref[...])
    xh[...] = hi
    xl[...] = lo
    lat = lambda lead, rsel, n: _cdot3f(flhl_ref, flh_ref, _rows(xh, lead, rsel, n), _rows(xl, lead, rsel, n), n2l)
    ctx = lambda lead, rsel, n: _cdot6(fc_ref, _rows(x_ref, lead, rsel, n), n2c)
    segs = ((slice(0, n2l), slice(rc, None), lat), (slice(n2l, n2l + n2c), slice(0, rc), ctx))
    for n in range(FFT_NB1):
        for ksel, rsel, cdot in segs:
            if packed:
                rr, ir = cdot((0,), rsel, n)
                ri, ii = cdot((1,), rsel, n)
                a_r, a_i = rr - ii, ir + ri
            else:
                a_r, a_i = cdot((), rsel, n)
            tr, ti = tr_ref[n, ksel, :], ti_ref[n, ksel, :]
            o_r, o_i = [], []
            for l in range(cc // LANES):
                cr, ci = a_r[:, l * LANES:(l + 1) * LANES], a_i[:, l * LANES:(l + 1) * LANES]
                o_r.append(cr * tr - ci * ti)
                o_i.append(cr * ti + ci * tr)
            rows = a_r.shape[0]
            o_ref[0, ksel, n:n + 1, :] = jnp.concatenate(o_r, axis=1).reshape(rows, 1, cc)
            o_ref[1, ksel, n:n + 1, :] = jnp.concatenate(o_i, axis=1).reshape(rows, 1, cc)


def _fft_stage_a(xv, tb, packed, c):
    k2n, r = tb["k2n"], tb["r"]
    cc = min(c, FFT_CC)
    nnb, ncc = FFT_N1 // FFT_NB1, c // cc
    mats = [tb["fl_hl"], tb["fl_h"], tb["fc_ri"]]
    mat_specs = [pl.BlockSpec(m.shape, lambda *a: (0, 0)) for m in mats]
    if packed:
        p = xv.shape[1]
        grid = (p, nnb, ncc)
        x_spec = pl.BlockSpec((2, None, r, FFT_NB1, cc), lambda pi, nb, cb: (0, pi, 0, nb, cb))
        t_spec = pl.BlockSpec((FFT_NB1, k2n, LANES), lambda pi, nb, cb: (nb, 0, 0))
        o_spec = pl.BlockSpec((None, 2, k2n, FFT_NB1, cc), lambda pi, nb, cb: (pi, 0, 0, nb, cb))
        o_shape = jax.ShapeDtypeStruct((p, 2, k2n, FFT_N1, c), F32)
        sem = ("parallel", "parallel", "parallel")
    else:
        grid = (nnb, ncc)
        x_spec = pl.BlockSpec((r, FFT_NB1, cc), lambda nb, cb: (0, nb, cb))
        t_spec = pl.BlockSpec((FFT_NB1, k2n, LANES), lambda nb, cb: (nb, 0, 0))
        o_spec = pl.BlockSpec((2, k2n, FFT_NB1, cc), lambda nb, cb: (0, 0, nb, cb))
        o_shape = jax.ShapeDtypeStruct((2, k2n, FFT_N1, c), F32)
        sem = ("parallel", "parallel")
    return pl.pallas_call(
        functools.partial(_fft_a_kernel, packed=packed, cc=cc, rc=tb["rc"], n2l=tb["n2l"], n2c=tb["n2c"]),
        grid=grid,
        in_specs=[x_spec] + mat_specs + [t_spec, t_spec],
        out_specs=o_spec,
        out_shape=o_shape,
        scratch_shapes=[pltpu.VMEM(((2,) if packed else ()) + (r, FFT_NB1, cc), F32)] * 2,
        compiler_params=_cparams(sem),
        name="hyena_fft_a",
    )(xv, *mats, tb["ta_r"], tb["ta_i"])


def _fft_bf_kernel(a_ref, fhl_ref, fh_ref, o_ref):
    rr, ir = _cdot3(fhl_ref, fh_ref, a_ref[0], FFT_N1)
    ri, ii = _cdot3(fhl_ref, fh_ref, a_ref[1], FFT_N1)
    o_ref[0] = rr - ii
    o_ref[1] = ir + ri


def _fft_stage_b_filter(af, tb, c4):
    k2n = tb["k2n"]
    cc = min(c4, FFT_CC)
    mhl = pl.BlockSpec((4 * FFT_N1, FFT_N1), lambda k, cb: (0, 0))
    mh = pl.BlockSpec((2 * FFT_N1, FFT_N1), lambda k, cb: (0, 0))
    blk = pl.BlockSpec((2, None, FFT_N1, cc), lambda k, cb: (0, k, 0, cb))
    return pl.pallas_call(
        _fft_bf_kernel,
        grid=(k2n, c4 // cc),
        in_specs=[blk, mhl, mh],
        out_specs=blk,
        out_shape=jax.ShapeDtypeStruct(af.shape, F32),
        compiler_params=_cparams(("parallel", "parallel")),
        name="hyena_fft_b_filter",
    )(af, tb["f1_hl"], tb["f1_h"])


def _fft_mid_kernel(a_ref, h0_ref, h1_ref, s0_ref, s1_ref, fhl_ref, fh_ref, tr_ref, ti_ref, o_ref, *, n2l, cc):
    k2 = pl.program_id(1)
    rr, ir = _cdot3(fhl_ref, fh_ref, a_ref[0], FFT_N1)
    ri, ii = _cdot3(fhl_ref, fh_ref, a_ref[1], FFT_N1)
    xr = rr - ii
    xi = ir + ri
    norm = s0_ref[...] + s1_ref[...]
    inv = 1.0 / jnp.where(k2 >= n2l, norm[0:1, :], norm[1:2, :])
    kr = (h0_ref[0] + h1_ref[0]) * inv
    ki = (h0_ref[1] - h1_ref[1]) * inv
    yr = xr * kr - xi * ki
    yi = xr * ki + xi * kr
    r_yr, i_yr = _cdot3(fhl_ref, fh_ref, yr, FFT_N1)
    r_yi, i_yi = _cdot3(fhl_ref, fh_ref, yi, FFT_N1)
    br = r_yr + i_yi
    bi = r_yi - i_yr
    tr, ti = tr_ref[...], ti_ref[...]
    for l in range(cc // LANES):
        sl = slice(l * LANES, (l + 1) * LANES)
        cr, ci = br[:, sl], bi[:, sl]
        o_ref[0, :, sl] = cr * tr + ci * ti
        o_ref[1, :, sl] = ci * tr - cr * ti


def _fft_mid(a, hspec, sums, tb, order, c):
    p = a.shape[0]
    k2n, n2l = tb["k2n"], tb["n2l"]
    cc = min(c, FFT_CC)
    ncc = c // cc
    mhl = pl.BlockSpec((4 * FFT_N1, FFT_N1), lambda pi, k, cb: (0, 0))
    mh = pl.BlockSpec((2 * FFT_N1, FFT_N1), lambda pi, k, cb: (0, 0))
    blk = pl.BlockSpec((None, 2, None, FFT_N1, cc), lambda pi, k, cb: (pi, 0, k, 0, cb))
    hs = lambda side: pl.BlockSpec((2, None, FFT_N1, cc), lambda pi, k, cb: (0, k, 0, (2 * order + side) * ncc + cb))
    ss = lambda side: pl.BlockSpec((SUBLANES, cc), lambda pi, k, cb: (0, (2 * order + side) * ncc + cb))
    tw = pl.BlockSpec((None, FFT_N1, LANES), lambda pi, k, cb: (k, 0, 0))
    return pl.pallas_call(
        functools.partial(_fft_mid_kernel, n2l=n2l, cc=cc),
        grid=(p, k2n, ncc),
        in_specs=[blk, hs(0), hs(1), ss(0), ss(1), mhl, mh, tw, tw],
        out_specs=blk,
        out_shape=jax.ShapeDtypeStruct(a.shape, F32),
        compiler_params=_cparams(("parallel", "parallel", "parallel")),
        name="hyena_fft_mid",
    )(a, hspec, hspec, sums, sums, tb["f1_hl"], tb["f1_h"], tb["tb_r"], tb["tb_i"])


def _fft_inv_kernel(b_ref, glhl_ref, glh_ref, gc_ref, g_ref, v_ref, skip_ref, o_ref, bh, bl, ybuf,
                    *, cc, rc, rl, n2l, n2c):
    hi, lo = _split_f32(b_ref[...])
    bh[...] = hi
    bl[...] = lo
    lat = lambda lead, ksel, n: _cdot3f(glhl_ref, glh_ref, _rows(bh, lead, ksel, n), _rows(bl, lead, ksel, n), rl)
    ctx = lambda lead, ksel, n: _cdot6(gc_ref, _rows(b_ref, lead, ksel, n), rc)
    segs = ((slice(0, n2l), slice(rc, rc + rl), lat), (slice(n2l, n2l + n2c), slice(0, rc), ctx))
    for n in range(FFT_NB1):
        for ksel, rsel, cdot in segs:
            r_br, i_br = cdot((0,), ksel, n)
            r_bi, i_bi = cdot((1,), ksel, n)
            yr = r_br + i_bi
            yi = r_bi - i_br
            ybuf[0, rsel, n:n + 1, :] = yr.reshape(yr.shape[0], 1, cc)
            ybuf[1, rsel, n:n + 1, :] = yi.reshape(yi.shape[0], 1, cc)
    o_ref[...] = (g_ref[...] * (ybuf[...] + skip_ref[...] * v_ref[...])).astype(o_ref.dtype)


def _fft_inverse_gate(bv, gv, vv, skip, tb, c):
    p = bv.shape[0]
    k2n, r = tb["k2n"], tb["r"]
    cc = min(c, FFT_CC)
    mats = [tb["gl_hl"], tb["gl_h"], tb["gc_ri"]]
    mat_specs = [pl.BlockSpec(m.shape, lambda pi, nb, cb: (0, 0)) for m in mats]
    xs = pl.BlockSpec((2, None, r, FFT_NB1, cc), lambda pi, nb, cb: (0, pi, 0, nb, cb))
    return pl.pallas_call(
        functools.partial(_fft_inv_kernel, cc=cc, rc=tb["rc"], rl=tb["rl"], n2l=tb["n2l"], n2c=tb["n2c"]),
        grid=(p, FFT_N1 // FFT_NB1, c // cc),
        in_specs=[pl.BlockSpec((None, 2, k2n, FFT_NB1, cc), lambda pi, nb, cb: (pi, 0, 0, nb, cb))] + mat_specs
        + [xs, xs, pl.BlockSpec((1, cc), lambda pi, nb, cb: (0, cb))],
        out_specs=xs,
        out_shape=jax.ShapeDtypeStruct(gv.shape, F32),
        scratch_shapes=[pltpu.VMEM((2, k2n, FFT_NB1, cc), F32), pltpu.VMEM((2, k2n, FFT_NB1, cc), F32),
                        pltpu.VMEM((2, r, FFT_NB1, cc), F32)],
        compiler_params=_cparams(("parallel", "parallel", "parallel")),
        name="hyena_fft_inv",
    )(bv, *mats, gv, vv, skip.reshape(1, c))


def _hyena_mix(u0, conv_w, conv_b, fw1, fb1, fw2, fb2, fw3, fb3, ffreq, fw4, skip, ctx_len):
    b, s, c3 = u0.shape
    c = c3 // 3
    seq = s - ctx_len
    assert b % 2 == 0 and seq % FFT_N1 == 0 and ctx_len % FFT_N1 == 0
    p = b // 2
    r = s // FFT_N1
    g1, g2, v = _dwconv3(u0, conv_w, conv_b, ctx_len)
    hw, sums = _hyena_filters(seq, ctx_len, fw1, fb1, fw2, fb2, fw3, fb3, ffreq, fw4, c)
    tb = _fft_tables(seq, ctx_len)
    af = _fft_stage_a(hw.reshape(r, FFT_N1, 4 * c), tb, False, 4 * c)
    hspec = _fft_stage_b_filter(af, tb, 4 * c)
    view = lambda t: t.reshape(2, p, r, FFT_N1, c)
    z = v
    for order, gate in ((0, g1), (1, g2)):
        a = _fft_stage_a(view(z), tb, True, c)
        bm = _fft_mid(a, hspec, sums, tb, order, c)
        z = _fft_inverse_gate(bm, view(gate), view(z), skip[order], tb, c).reshape(b, s, c)
    return z


def _router_kernel(h_ref, r_ref, g_ref, hb_ref, *, ne):
    h = h_ref[...]
    h_hi = h.astype(BF16)
    hb_ref[...] = h_hi
    h_lo = (h - h_hi.astype(F32)).astype(BF16)
    s = jnp.dot(h_hi, r_ref[...], preferred_element_type=F32)
    logits = s[:, :LANES] + s[:, LANES:] + jnp.dot(h_lo, r_ref[:, :LANES], preferred_element_type=F32)
    lane = lax.broadcasted_iota(jnp.int32, logits.shape, 1)
    neg = -jnp.inf
    logits = jnp.where(lane < ne, logits, neg)
    m1 = jnp.max(logits, axis=1, keepdims=True)
    i1 = jnp.min(jnp.where(logits == m1, lane, LANES), axis=1, keepdims=True)
    rest = jnp.where(lane == i1, neg, logits)
    m2 = jnp.max(rest, axis=1, keepdims=True)
    i2 = jnp.min(jnp.where(rest == m2, lane, LANES), axis=1, keepdims=True)
    e2 = jnp.exp(m2 - m1)
    w1 = 1.0 / (1.0 + e2)
    g_ref[...] = jnp.where(lane == i1, w1, 0.0) + jnp.where(lane == i2, e2 * w1, 0.0)


def _router(h2, router):
    b, s, d = h2.shape
    ne = router.shape[1]
    tm = _pick_tile(s, (544, 384, 256, 128))
    rp = jnp.pad(router.astype(F32), ((0, 0), (0, LANES - ne)))
    rp_hi = rp.astype(BF16)
    rp = jnp.concatenate([rp_hi, (rp - rp_hi.astype(F32)).astype(BF16)], axis=1)
    return pl.pallas_call(
        functools.partial(_router_kernel, ne=ne),
        grid=(b, s // tm),
        in_specs=[pl.BlockSpec((None, tm, d), lambda bi, j: (bi, j, 0)),
                  pl.BlockSpec((d, 2 * LANES), lambda bi, j: (0, 0))],
        out_specs=[pl.BlockSpec((None, tm, LANES), lambda bi, j: (bi, j, 0)),
                   pl.BlockSpec((None, tm, d), lambda bi, j: (bi, j, 0))],
        out_shape=[jax.ShapeDtypeStruct((b, s, LANES), F32), jax.ShapeDtypeStruct((b, s, d), BF16)],
        compiler_params=_cparams(("parallel", "parallel")),
        name="moe_router",
    )(h2, rp)


def _moe_sparse_kernel(h_ref, x_ref, mod_ref, gate_ref, wg_ref, wu_ref, wd_ref, o_ref, dest, cnt, xe, ye,
                       *, tb, ffn_rows, main_tiles, rb, ctx_len, ne, nf):
    j, e, f = pl.program_id(1), pl.program_id(2), pl.program_id(3)
    half = tb // 2

    @pl.when(jnp.logical_and(e == 0, f == 0))
    def _():
        tri = (lax.broadcasted_iota(jnp.int32, (rb, rb), 0) > lax.broadcasted_iota(jnp.int32, (rb, rb), 1))
        tri = jnp.where(tri, 1.0, 0.0).astype(BF16)
        carry = jnp.zeros((1, LANES), F32)
        for k in range(tb // rb):
            routed = gate_ref[k * rb:(k + 1) * rb, :] > 0.0
            r01 = jnp.where(routed, 1.0, 0.0)
            before = jnp.dot(tri, r01.astype(BF16), preferred_element_type=F32) + carry
            dest[k * rb:(k + 1) * rb, :] = jnp.where(routed, before, -1.0)
            carry = carry + jnp.sum(r01, axis=0, keepdims=True)
        cnt[...] = jnp.broadcast_to(carry, cnt.shape)
        o_ref[...] = jnp.zeros_like(o_ref)

    lane1 = lax.broadcasted_iota(jnp.int32, (1, LANES), 1)
    n_e = jnp.sum(jnp.where(lane1 == e, cnt[0:1, :], 0.0))
    n_tiles = jnp.ceil(n_e * (1.0 / LANES)).astype(jnp.int32)

    lane_h = lax.broadcasted_iota(jnp.int32, (half, LANES), 1)

    def expert_col(ref, rows):
        return jnp.sum(jnp.where(lane_h == e, ref[rows, :], 0.0), axis=1, keepdims=True)

    def one_hot(rows, first, width):
        dcol = expert_col(dest, rows) - first
        lanef = lane_h.astype(F32)
        pieces = [jnp.where(dcol == lanef + float(k * LANES), 1.0, 0.0).astype(BF16) for k in range(width // LANES)]
        return jnp.concatenate(pieces, axis=1)

    def gather(first, width, dst):
        acc = None
        for hb in range(2):
            rows = slice(hb * half, (hb + 1) * half)
            part = lax.dot_general(one_hot(rows, first, width), h_ref[rows, :], (((0,), (0,)), ((), ())),
                                   preferred_element_type=F32)
            acc = part if acc is None else acc + part
        xe[dst, :] = acc.astype(BF16)
        ye[dst, :] = jnp.zeros((width, ye.shape[1]), F32)

    def expert_ffn(dst):
        x = xe[dst, :]
        a = jnp.dot(x, wg_ref[...], preferred_element_type=F32)
        u = jnp.dot(x, wu_ref[...], preferred_element_type=F32)
        act = (a * jax.nn.sigmoid(a) * u).astype(BF16)
        ye[dst, :] += jnp.dot(act, wd_ref[...], preferred_element_type=F32)

    def scatter(first, width, dst):
        y = ye[dst, :].astype(BF16)
        for hb in range(2):
            rows = slice(hb * half, (hb + 1) * half)
            back = jnp.dot(one_hot(rows, first, width), y, preferred_element_type=F32)
            o_ref[rows, :] += expert_col(gate_ref, rows) * back

    main_rows = main_tiles * LANES
    active = n_tiles > 0
    n_extra = jnp.maximum(n_tiles - main_tiles, 0)

    def extra(fn):
        def body(c, carry):
            off = pl.multiple_of(main_rows + c * LANES, LANES)
            fn(off.astype(F32), LANES, pl.ds(off, LANES))
            return carry
        lax.fori_loop(0, n_extra, body, 0)

    def main_ffn():
        below = None
        for i, rows_c in enumerate(ffn_rows):
            sel = n_e <= float(rows_c) if i + 1 < len(ffn_rows) else n_e > float(below)
            if below is not None and i + 1 < len(ffn_rows):
                sel = jnp.logical_and(sel, n_e > float(below))
            pl.when(sel)(functools.partial(expert_ffn, slice(0, rows_c)))
            below = rows_c

    @pl.when(jnp.logical_and(active, f == 0))
    def _():
        gather(0.0, main_rows, slice(0, main_rows))
        extra(gather)

    @pl.when(active)
    def _():
        main_ffn()
        extra(lambda first, width, dst: expert_ffn(dst))

    @pl.when(jnp.logical_and(active, f == nf - 1))
    def _():
        scatter(0.0, main_rows, slice(0, main_rows))
        extra(scatter)

    @pl.when(jnp.logical_and(e == ne - 1, f == nf - 1))
    def _():
        x = x_ref[...]
        is_ctx = _row_is_ctx(j, tb, ctx_len, x.shape)
        o_ref[...] = x + _sel_mod(mod_ref, is_ctx, 5) * o_ref[...]


def _moe_sparse(hb16, x_all, mods_i, gate, wg, wu, wd, ctx_len):
    b, s, d = x_all.shape
    ne, _, f = wg.shape
    tb = _pick_tile(s, (1088, 768, 512, 256))
    max_tiles = -(-tb // LANES)
    ch = max_tiles * LANES
    mean_tiles = -(-tb * TOP_K // (ne * LANES))
    min_tiles, main_tiles = max(mean_tiles - 1, 1), min(mean_tiles + 1, max_tiles)
    half = LANES // 2
    ffn_rows = tuple(sorted({min_tiles * LANES, min_tiles * LANES + half, mean_tiles * LANES, main_tiles * LANES}))
    rb = max(r for r in range(SUBLANES, 257, SUBLANES) if tb % r == 0)
    fc = _pick_tile(f, (896, 512, 256, 128))
    nf = f // fc
    row = lambda bi, j, e, k: (bi, j, 0)
    return pl.pallas_call(
        functools.partial(_moe_sparse_kernel, tb=tb, ffn_rows=ffn_rows, main_tiles=main_tiles, rb=rb,
                          ctx_len=ctx_len, ne=ne, nf=nf),
        grid=(b, s // tb, ne, nf),
        in_specs=[
            pl.BlockSpec((None, tb, d), row),
            pl.BlockSpec((None, tb, d), row, pipeline_mode=pl.Buffered(1)),
            pl.BlockSpec((None, 2, 8, d), lambda bi, j, e, k: (bi, 0, 0, 0)),
            pl.BlockSpec((None, tb, LANES), row),
            pl.BlockSpec((None, d, fc), lambda bi, j, e, k: (e, 0, k)),
            pl.BlockSpec((None, d, fc), lambda bi, j, e, k: (e, 0, k)),
            pl.BlockSpec((None, fc, d), lambda bi, j, e, k: (e, k, 0)),
        ],
        out_specs=pl.BlockSpec((None, tb, d), row),
        out_shape=jax.ShapeDtypeStruct((b, s, d), F32),
        scratch_shapes=[pltpu.VMEM((tb, LANES), F32), pltpu.VMEM((SUBLANES, LANES), F32),
                        pltpu.VMEM((ch, d), BF16), pltpu.VMEM((ch, d), F32)],
        compiler_params=_cparams(("parallel", "parallel", "arbitrary", "arbitrary")),
        name="moe_experts",
    )(hb16, x_all, mods_i, gate, wg.astype(BF16), wu.astype(BF16), wd.astype(BF16))


def _final_norm_kernel(x_ref, g_ref, o_ref):
    x = x_ref[...]
    ms = jnp.mean(x * x, axis=-1, keepdims=True)
    o_ref[...] = (x * lax.rsqrt(ms + EPS)) * g_ref[...]


def _final_norm(x_all, g, ctx_len):
    b, s, d = x_all.shape
    seq = s - ctx_len
    tm = _pick_tile(math.gcd(seq, ctx_len), (512, 256, 128))
    off = ctx_len // tm
    return pl.pallas_call(
        _final_norm_kernel,
        grid=(b, seq // tm),
        in_specs=[pl.BlockSpec((None, tm, d), lambda bi, j: (bi, j + off, 0)), pl.BlockSpec((1, d), lambda bi, j: (0, 0))],
        out_specs=pl.BlockSpec((None, tm, d), lambda bi, j: (bi, j, 0)),
        out_shape=jax.ShapeDtypeStruct((b, seq, d), F32),
        compiler_params=_cparams(("parallel", "parallel")),
        name="final_norm",
    )(x_all, g.reshape(1, d))


def kernel(x, c, ctx, c_ctx, ada_w, ada_b, norm_mix, norm_ffn, norm_final,
           lru_w_in, lru_conv_w, lru_conv_b, lru_w_a, lru_b_a, lru_w_x, lru_b_x, lru_lambda, lru_w_out,
           attn_w_qkv, attn_sinks, attn_w_o,
           hy_w_in, hy_b_in, hy_conv_w, hy_conv_b, hy_f_w1, hy_f_b1, hy_f_w2, hy_f_b2, hy_f_w3, hy_f_b3,
           hy_f_freq, hy_f_w4, hy_skip, hy_w_out, hy_b_out,
           ffn_w_gate, ffn_w_up, ffn_w_down,
           moe_router, moe_w_gate, moe_w_up, moe_w_down):
    depth = ada_w.shape[0]
    ctx_len, seq, d = ctx.shape[1], x.shape[1], x.shape[2]
    x_all = jnp.concatenate([ctx, x], axis=1)
    mods = _mods(c, c_ctx, ada_w, ada_b)
    zero_d = jnp.zeros((d,), F32)
    for i in range(depth):
        j = i // N_MIXERS
        moe = i % 2 == 1
        h_dtype = F32 if moe else BF16
        if i % N_MIXERS == 0:
            gu = _norm_mm(x_all, mods[i], norm_mix[i], lru_w_in[j], jnp.zeros((lru_w_in.shape[2],), F32), ctx_len, F32)
            hs_f, hs_b = _lru_scan(gu, lru_conv_w[j], lru_conv_b[j], lru_w_a[j], lru_b_a[j], lru_w_x[j], lru_b_x[j],
                                   lru_lambda[j], ctx_len)
            x_all, h2 = _lru_mm_res(hs_f, hs_b, gu, x_all, mods[i], norm_ffn[i], lru_w_out[j], ctx_len, h_dtype)
        elif i % N_MIXERS == 1:
            cos_t, sin_t = _rope_tables(seq, ctx_len)
            qkv = _qkv_proj(x_all, mods[i], norm_mix[i], attn_w_qkv[j], cos_t, sin_t, ctx_len)
            o = _attention(qkv, attn_sinks[j], ctx_len)
            x_all, h2 = _mm_res(o, x_all, mods[i], norm_ffn[i], attn_w_o[j], zero_d, ctx_len, h_dtype)
        else:
            u0 = _norm_mm(x_all, mods[i], norm_mix[i], hy_w_in[j], hy_b_in[j], ctx_len, F32)
            z = _hyena_mix(u0, hy_conv_w[j], hy_conv_b[j], hy_f_w1[j], hy_f_b1[j], hy_f_w2[j], hy_f_b2[j],
                           hy_f_w3[j], hy_f_b3[j], hy_f_freq[j], hy_f_w4[j], hy_skip[j], ctx_len)
            x_all, h2 = _mm_res(z, x_all, mods[i], norm_ffn[i], hy_w_out[j], hy_b_out[j], ctx_len, h_dtype)
        if moe:
            gate, hb16 = _router(h2, moe_router[i // 2])
            x_all = _moe_sparse(hb16, x_all, mods[i], gate, moe_w_gate[i // 2], moe_w_up[i // 2], moe_w_down[i // 2],
                                ctx_len)
        else:
            x_all = _ffn_dense(h2, x_all, mods[i], ffn_w_gate[i // 2], ffn_w_up[i // 2], ffn_w_down[i // 2], ctx_len)
    return _final_norm(x_all, norm_final, ctx_len)
```

```python
import functools
import math

import jax
import jax.numpy as jnp
import numpy as np
from jax import lax
from jax.experimental import pallas as pl
from jax.experimental.pallas import tpu as pltpu

F32 = jnp.float32
BF16 = jnp.bfloat16
HIGHEST = lax.Precision.HIGHEST

N_MIXERS = 3
EPS = 1e-6
GRID_W = 64
RNN_BLOCKS = 8
LRU_C = 8.0
N_HEADS = 16
N_KV_HEADS = 4
HEAD_DIM = 64
WINDOW = 128
ROPE_BASE = 10000.0
FILTER_BANDS = 16
DECAY_TARGET = 1e-2
FAST_DECAY_PCT = 0.3
SLOW_DECAY_PCT = 1.5
N_EXPERTS = 8
TOP_K = 2

LANES = 128
SUBLANES = 8
VMEM_LIMIT = 56 * 1024 * 1024
FFT_N1 = 128


def _cparams(sem, vmem=VMEM_LIMIT):
    return pltpu.CompilerParams(dimension_semantics=sem, vmem_limit_bytes=vmem)


def _pick_tile(n, cands):
    for c in cands:
        if n % c == 0:
            return c
    raise ValueError(f"no tile for {n} in {cands}")


def _row_is_ctx(j, tm, ctx_len, shape):
    rows = j * tm + lax.broadcasted_iota(jnp.int32, shape, 0)
    return rows < ctx_len


def _sel_mod(mod_ref, is_ctx, idx):
    return jnp.where(is_ctx, mod_ref[0, idx:idx + 1, :], mod_ref[1, idx:idx + 1, :])


def _adanorm(x, g, shift, scale):
    ms = jnp.mean(x * x, axis=-1, keepdims=True)
    y = x * lax.rsqrt(ms + EPS)
    return (y * g) * (1.0 + scale) + shift


def _mods(c, c_ctx, ada_w, ada_b):
    depth, d, d6 = ada_w.shape
    b = c.shape[0]
    rows = ((b + 1 + SUBLANES - 1) // SUBLANES) * SUBLANES
    cond = jnp.concatenate([c, c_ctx[None, :], jnp.zeros((rows - b - 1, d), F32)], axis=0)
    tn = _pick_tile(d6, (1024, 512, 256, 128))

    def kern(c_ref, w_ref, b_ref, o_ref):
        x = c_ref[...]
        s = x * jax.nn.sigmoid(x)
        o_ref[...] = jnp.dot(s, w_ref[...], preferred_element_type=F32, precision=HIGHEST) + b_ref[...]

    out = pl.pallas_call(
        kern,
        grid=(depth, d6 // tn),
        in_specs=[
            pl.BlockSpec((rows, d), lambda i, n: (0, 0)),
            pl.BlockSpec((None, d, tn), lambda i, n: (i, 0, n)),
            pl.BlockSpec((None, 1, tn), lambda i, n: (i, 0, n)),
        ],
        out_specs=pl.BlockSpec((None, rows, tn), lambda i, n: (i, 0, n)),
        out_shape=jax.ShapeDtypeStruct((depth, rows, d6), F32),
        compiler_params=_cparams(("parallel", "parallel")),
        name="mods",
    )(cond, ada_w, ada_b.reshape(depth, 1, d6))
    m = out.reshape(depth, rows, 6, d)
    lat = m[:, :b]
    ctx = jnp.broadcast_to(m[:, b:b + 1], lat.shape)
    both = jnp.stack([ctx, lat], axis=2)
    return jnp.pad(both, ((0, 0), (0, 0), (0, 0), (0, 2), (0, 0)))


def _norm_mm_kernel(x_ref, mod_ref, g_ref, w_ref, b_ref, o_ref, *, tm, ctx_len):
    j = pl.program_id(1)
    x = x_ref[...]
    is_ctx = _row_is_ctx(j, tm, ctx_len, x.shape)
    h = _adanorm(x, g_ref[...], _sel_mod(mod_ref, is_ctx, 0), _sel_mod(mod_ref, is_ctx, 1))
    acc = jnp.dot(h.astype(BF16), w_ref[...], preferred_element_type=F32) + b_ref[...]
    o_ref[...] = acc.astype(o_ref.dtype)


def _norm_mm(x_all, mods_i, g, w, bias, ctx_len, out_dtype):
    b, s, d = x_all.shape
    n = w.shape[1]
    tm = _pick_tile(s, (544, 384, 256, 128))
    return pl.pallas_call(
        functools.partial(_norm_mm_kernel, tm=tm, ctx_len=ctx_len),
        grid=(b, s // tm),
        in_specs=[
            pl.BlockSpec((None, tm, d), lambda bi, j: (bi, j, 0)),
            pl.BlockSpec((None, 2, 8, d), lambda bi, j: (bi, 0, 0, 0)),
            pl.BlockSpec((1, d), lambda bi, j: (0, 0)),
            pl.BlockSpec((d, n), lambda bi, j: (0, 0)),
            pl.BlockSpec((1, n), lambda bi, j: (0, 0)),
        ],
        out_specs=pl.BlockSpec((None, tm, n), lambda bi, j: (bi, j, 0)),
        out_shape=jax.ShapeDtypeStruct((b, s, n), out_dtype),
        compiler_params=_cparams(("parallel", "parallel")),
        name="norm_mm",
    )(x_all, mods_i, g.reshape(1, d), w.astype(BF16), bias.reshape(1, n).astype(F32))


def _mm_res_body(y_bf16, x_ref, mod_ref, g_ref, w_ref, b_ref, rest, tm, ctx_len):
    r_ref, gate_ref = (rest[0], rest[3]) if len(rest) == 4 else (None, None)
    xo_ref, ho_ref = rest[-3:-1] if r_ref is not None else rest
    j = pl.program_id(1)
    x = x_ref[...]
    is_ctx = _row_is_ctx(j, tm, ctx_len, x.shape)
    acc = jnp.dot(y_bf16, w_ref[...], preferred_element_type=F32) + b_ref[...]
    x_new = x + _sel_mod(mod_ref, is_ctx, 2) * acc
    xo_ref[...] = x_new
    h2 = _adanorm(x_new, g_ref[...], _sel_mod(mod_ref, is_ctx, 3), _sel_mod(mod_ref, is_ctx, 4))
    if r_ref is None:
        ho_ref[...] = h2.astype(ho_ref.dtype)
    else:
        h_hi, gate_ref[...] = _route(h2, r_ref)
        ho_ref[...] = h_hi


def _mm_res_kernel(y_ref, x_ref, mod_ref, g_ref, w_ref, b_ref, *rest, tm, ctx_len):
    _mm_res_body(y_ref[...].astype(BF16), x_ref, mod_ref, g_ref, w_ref, b_ref, rest, tm, ctx_len)


def _lru_mm_res_kernel(hf_ref, hb_ref, gate_ref, x_ref, mod_ref, g_ref, w_ref, b_ref, *rest, tm, ctx_len):
    y = (hf_ref[...] + hb_ref[...]) * jax.nn.gelu(gate_ref[...])
    _mm_res_body(y.astype(BF16), x_ref, mod_ref, g_ref, w_ref, b_ref, rest, tm, ctx_len)


def _mm_res_call(kern, ys, y_specs, x_all, mods_i, g_ffn, w, bias, ctx_len, tm, router):
    b, s, d = x_all.shape
    k = w.shape[0]
    row = lambda n: pl.BlockSpec((None, tm, n), lambda bi, j: (bi, j, 0))
    extra_in, extra_args, extra_out, extra_shape = [], [], [], []
    if router is not None:
        extra_in = [pl.BlockSpec((d, 2 * LANES), lambda bi, j: (0, 0))]
        extra_args = [_router_operand(router)]
        extra_out = [row(LANES)]
        extra_shape = [jax.ShapeDtypeStruct((b, s, LANES), F32)]
    return pl.pallas_call(
        functools.partial(kern, tm=tm, ctx_len=ctx_len),
        grid=(b, s // tm),
        in_specs=y_specs + [
            row(d),
            pl.BlockSpec((None, 2, 8, d), lambda bi, j: (bi, 0, 0, 0)),
            pl.BlockSpec((1, d), lambda bi, j: (0, 0)),
            pl.BlockSpec((k, d), lambda bi, j: (0, 0)),
            pl.BlockSpec((1, d), lambda bi, j: (0, 0)),
        ] + extra_in,
        out_specs=[row(d), row(d)] + extra_out,
        out_shape=[jax.ShapeDtypeStruct((b, s, d), F32), jax.ShapeDtypeStruct((b, s, d), BF16)] + extra_shape,
        compiler_params=_cparams(("parallel", "parallel")),
        name="mm_res",
    )(*ys, x_all, mods_i, g_ffn.reshape(1, d), w.astype(BF16), bias.reshape(1, d).astype(F32), *extra_args)


def _mm_res(y, x_all, mods_i, g_ffn, w, bias, ctx_len, router=None):
    b, s, d = x_all.shape
    k = w.shape[0]
    tm = _pick_tile(s, (544, 384, 256, 128))
    spec = [pl.BlockSpec((None, tm, k), lambda bi, j: (bi, j, 0))]
    return _mm_res_call(_mm_res_kernel, [y], spec, x_all, mods_i, g_ffn, w, bias, ctx_len, tm, router)


def _lru_mm_res(hs_f, hs_b, gu, x_all, mods_i, g_ffn, w, ctx_len, router=None):
    b, s, d = x_all.shape
    k = w.shape[0]
    tm = _pick_tile(s, (544, 384, 256, 128))
    specs = [
        pl.BlockSpec((None, tm, k), lambda bi, j: (bi, j, 0)),
        pl.BlockSpec((None, tm, k), lambda bi, j: (bi, j, 0)),
        pl.BlockSpec((None, tm, k), lambda bi, j: (bi, j, 0)),
    ]
    return _mm_res_call(_lru_mm_res_kernel, [hs_f, hs_b, gu], specs, x_all, mods_i, g_ffn, w,
                        jnp.zeros((d,), F32), ctx_len, tm, router)


def _ffn_kernel(h_ref, x_ref, mod_ref, wg_ref, wu_ref, wd_ref, o_ref, *, tm, ctx_len):
    j = pl.program_id(1)
    h = h_ref[...]
    a = jnp.dot(h, wg_ref[...], preferred_element_type=F32)
    u = jnp.dot(h, wu_ref[...], preferred_element_type=F32)
    act = (a * jax.nn.sigmoid(a) * u).astype(BF16)
    y = jnp.dot(act, wd_ref[...], preferred_element_type=F32)
    x = x_ref[...]
    is_ctx = _row_is_ctx(j, tm, ctx_len, x.shape)
    o_ref[...] = x + _sel_mod(mod_ref, is_ctx, 5) * y


def _ffn_dense(h2, x_all, mods_i, wg, wu, wd, ctx_len):
    b, s, d = x_all.shape
    f = wg.shape[1]
    tm = _pick_tile(s, (544, 384, 256, 128))
    resident = dict(pipeline_mode=pl.Buffered(1))
    return pl.pallas_call(
        functools.partial(_ffn_kernel, tm=tm, ctx_len=ctx_len),
        grid=(b, s // tm),
        in_specs=[
            pl.BlockSpec((None, tm, d), lambda bi, j: (bi, j, 0)),
            pl.BlockSpec((None, tm, d), lambda bi, j: (bi, j, 0)),
            pl.BlockSpec((None, 2, 8, d), lambda bi, j: (bi, 0, 0, 0)),
            pl.BlockSpec((d, f), lambda bi, j: (0, 0), **resident),
            pl.BlockSpec((d, f), lambda bi, j: (0, 0), **resident),
            pl.BlockSpec((f, d), lambda bi, j: (0, 0), **resident),
        ],
        out_specs=pl.BlockSpec((None, tm, d), lambda bi, j: (bi, j, 0)),
        out_shape=jax.ShapeDtypeStruct((b, s, d), F32),
        compiler_params=_cparams(("parallel", "parallel")),
        name="ffn_dense",
    )(h2, x_all, mods_i, wg.astype(BF16), wu.astype(BF16), wd.astype(BF16))


def _lru_bwd_tile(j, nt, nct):
    return jnp.where(j < nct, nct - 1 - j, nt - 1 - (j - nct))


def _lru_scan_kernel(um_f, up_f, un_f, um_b, up_b, un_b, cw_ref, cb_ref, wax_f, wax_b, ba_ref, bx_ref, lam_ref,
                     hsf_ref, hsb_ref, ext, af_s, bf_s, ab_s, bb_s, carry, *, tt, nt, nct, wl, nb):
    j = pl.program_id(1)

    @pl.when(j == 0)
    def _():
        carry[...] = jnp.zeros_like(carry)

    def coeffs(um, up, un, t, d, wax_ref, a_s, b_s):
        prev_ok = jnp.logical_and(t != 0, t != nct)
        next_ok = jnp.logical_and(t != nct - 1, t != nt - 1)
        ext[:, SUBLANES:SUBLANES + tt, :] = um[...]
        ext[:, 0:SUBLANES, :] = jnp.where(prev_ok, up[...], 0.0)
        ext[:, SUBLANES + tt:2 * SUBLANES + tt, :] = jnp.where(next_ok, un[...], 0.0)
        uc = cb_ref[...].reshape(1, 1, wl)
        for k in range(4):
            uc = uc + cw_ref[k:k + 1, :].reshape(1, 1, wl) * ext[:, SUBLANES - 2 + k:SUBLANES - 2 + k + tt, :]
        sp = jax.nn.softplus(-lam_ref[d:d + 1, :])
        for hb in range(wl // LANES):
            sl = slice(hb * LANES, (hb + 1) * LANES)
            ub = uc[:, :, sl].reshape(nb * tt, LANES)
            pre = jnp.dot(ub.astype(BF16), wax_ref[hb], preferred_element_type=F32)
            r = jax.nn.sigmoid(pre[:, :LANES] + ba_ref[d:d + 1, sl])
            i = jax.nn.sigmoid(pre[:, LANES:] + bx_ref[d:d + 1, sl])
            log_a = (-LRU_C) * r * sp[:, sl]
            a = jnp.exp(log_a)
            b = jnp.sqrt(-jnp.tanh(log_a) * (a * a + 1.0)) * (i * ub)
            a_s[:, :, sl] = a.reshape(nb, tt, LANES)
            b_s[:, :, sl] = b.reshape(nb, tt, LANES)

    tb = _lru_bwd_tile(j, nt, nct)
    coeffs(um_f, up_f, un_f, j, 0, wax_f, af_s, bf_s)
    coeffs(um_b, up_b, un_b, tb, 1, wax_b, ab_s, bb_s)

    def step(t, c):
        hf, hb = c
        hf = af_s[:, pl.ds(t, 1), :].reshape(nb, wl) * hf + bf_s[:, pl.ds(t, 1), :].reshape(nb, wl)
        hsf_ref[:, pl.ds(t, 1), :] = hf.reshape(nb, 1, wl)
        r = tt - 1 - t
        hb = ab_s[:, pl.ds(r, 1), :].reshape(nb, wl) * hb + bb_s[:, pl.ds(r, 1), :].reshape(nb, wl)
        hsb_ref[:, pl.ds(r, 1), :] = hb.reshape(nb, 1, wl)
        return hf, hb

    hf, hb = lax.fori_loop(0, tt, step, (carry[0], carry[1]), unroll=8)
    carry[0] = hf
    carry[1] = hb


def _lru_scan(gu, conv_w, conv_b, w_a, b_a, w_x, b_x, lam, ctx_len):
    nb, s, d2 = gu.shape
    d = d2 // 2
    tt = 128
    assert s % tt == 0 and ctx_len % tt == 0 and d % (RNN_BLOCKS * LANES) == 0 and d // RNN_BLOCKS == LANES
    nt, nct = s // tt, ctx_len // tt
    wl = 4 * LANES
    ncg = d // wl
    coff = d // wl
    r8 = tt // SUBLANES
    last8 = s // SUBLANES - 1
    wax = jnp.concatenate([w_a, w_x], axis=-1).astype(BF16)

    def main_f(cg, j): return (0, j, coff + cg)
    def prev_f(cg, j): return (0, jnp.maximum(j * r8 - 1, 0), coff + cg)
    def next_f(cg, j): return (0, jnp.minimum((j + 1) * r8, last8), coff + cg)
    def tb_(j): return _lru_bwd_tile(j, nt, nct)
    def main_b(cg, j): return (0, tb_(j), coff + cg)
    def prev_b(cg, j): return (0, jnp.maximum(tb_(j) * r8 - 1, 0), coff + cg)
    def next_b(cg, j): return (0, jnp.minimum((tb_(j) + 1) * r8, last8), coff + cg)

    big = lambda im: pl.BlockSpec((nb, tt, wl), im)
    halo = lambda im: pl.BlockSpec((nb, SUBLANES, wl), im)
    vec2 = pl.BlockSpec((2, wl), lambda cg, j: (0, cg))
    return pl.pallas_call(
        functools.partial(_lru_scan_kernel, tt=tt, nt=nt, nct=nct, wl=wl, nb=nb),
        grid=(ncg, nt),
        in_specs=[
            big(main_f), halo(prev_f), halo(next_f), big(main_b), halo(prev_b), halo(next_b),
            pl.BlockSpec((4, wl), lambda cg, j: (0, cg)),
            pl.BlockSpec((1, wl), lambda cg, j: (0, cg)),
            pl.BlockSpec((None, wl // LANES, LANES, 2 * LANES), lambda cg, j: (0, cg, 0, 0)),
            pl.BlockSpec((None, wl // LANES, LANES, 2 * LANES), lambda cg, j: (1, cg, 0, 0)),
            vec2, vec2, vec2,
        ],
        out_specs=[
            pl.BlockSpec((nb, tt, wl), lambda cg, j: (0, j, cg)),
            pl.BlockSpec((nb, tt, wl), lambda cg, j: (0, tb_(j), cg)),
        ],
        out_shape=[jax.ShapeDtypeStruct((nb, s, d), F32), jax.ShapeDtypeStruct((nb, s, d), F32)],
        scratch_shapes=[
            pltpu.VMEM((nb, tt + 2 * SUBLANES, wl), F32),
            pltpu.VMEM((nb, tt, wl), F32), pltpu.VMEM((nb, tt, wl), F32),
            pltpu.VMEM((nb, tt, wl), F32), pltpu.VMEM((nb, tt, wl), F32),
            pltpu.VMEM((2, nb, wl), F32),
        ],
        compiler_params=_cparams(("parallel", "arbitrary")),
        name="lru_scan",
    )(gu, gu, gu, gu, gu, gu, conv_w, conv_b.reshape(1, d), wax, wax, b_a, b_x, lam)


def _rope_tables(seq, ctx_len):
    q = HEAD_DIM // 4
    inv_freq = ROPE_BASE ** (-jnp.arange(q, dtype=F32) / q)
    pos = jnp.arange(seq, dtype=jnp.int32)
    row = (pos // GRID_W).astype(F32)[:, None] * inv_freq
    col = (pos % GRID_W).astype(F32)[:, None] * inv_freq
    ang = jnp.concatenate([row, row, col, col], axis=-1)
    sign = jnp.concatenate([-jnp.ones((q,), F32), jnp.ones((q,), F32)] * 2)
    cos = jnp.concatenate([jnp.ones((ctx_len, HEAD_DIM), F32), jnp.cos(ang)], axis=0)
    sin = jnp.concatenate([jnp.zeros((ctx_len, HEAD_DIM), F32), jnp.sin(ang) * sign], axis=0)
    return jnp.tile(cos, (1, 2)), jnp.tile(sin, (1, 2))


def _qkv_kernel(x_ref, mod_ref, g_ref, w_ref, cos_ref, sin_ref, o_ref, *, tm, ctx_len, n_q, n_rope):
    j = pl.program_id(1)
    x = x_ref[...]
    is_ctx = _row_is_ctx(j, tm, ctx_len, x.shape)
    h = _adanorm(x, g_ref[...], _sel_mod(mod_ref, is_ctx, 0), _sel_mod(mod_ref, is_ctx, 1))
    acc = jnp.dot(h.astype(BF16), w_ref[...], preferred_element_type=F32)
    cos = cos_ref[...]
    sin = sin_ref[...]
    lane = lax.broadcasted_iota(jnp.int32, (tm, LANES), 1)
    first_half = (lane % (HEAD_DIM // 2)) < (HEAD_DIM // 4)
    q_scale = HEAD_DIM ** -0.5
    for c in range(acc.shape[1] // LANES):
        v = acc[:, c * LANES:(c + 1) * LANES]
        if c < n_rope:
            partner = jnp.where(first_half, pltpu.roll(v, LANES - HEAD_DIM // 4, 1), pltpu.roll(v, HEAD_DIM // 4, 1))
            v = v * cos + partner * sin
            if c < n_q:
                v = v * q_scale
        o_ref[:, c * LANES:(c + 1) * LANES] = v.astype(o_ref.dtype)


def _qkv_proj(x_all, mods_i, g, w, cos_t, sin_t, ctx_len):
    b, s, d = x_all.shape
    n = w.shape[1]
    tm = _pick_tile(s, (544, 384, 256, 128))
    n_q = N_HEADS * HEAD_DIM // LANES
    n_rope = (N_HEADS + N_KV_HEADS) * HEAD_DIM // LANES
    return pl.pallas_call(
        functools.partial(_qkv_kernel, tm=tm, ctx_len=ctx_len, n_q=n_q, n_rope=n_rope),
        grid=(b, s // tm),
        in_specs=[
            pl.BlockSpec((None, tm, d), lambda bi, j: (bi, j, 0)),
            pl.BlockSpec((None, 2, 8, d), lambda bi, j: (bi, 0, 0, 0)),
            pl.BlockSpec((1, d), lambda bi, j: (0, 0)),
            pl.BlockSpec((d, n), lambda bi, j: (0, 0)),
            pl.BlockSpec((tm, LANES), lambda bi, j: (j, 0)),
            pl.BlockSpec((tm, LANES), lambda bi, j: (j, 0)),
        ],
        out_specs=pl.BlockSpec((None, tm, n), lambda bi, j: (bi, j, 0)),
        out_shape=jax.ShapeDtypeStruct((b, s, n), BF16),
        compiler_params=_cparams(("parallel", "parallel")),
        name="qkv_proj",
    )(x_all, mods_i, g.reshape(1, d), w.astype(BF16), cos_t, sin_t)


def _attn_kernel(q_ref, kp_ref, km_ref, kn_ref, kc_ref, vp_ref, vm_ref, vn_ref, vc_ref, sink_ref, o_ref,
                 *, tq, nct, seq, ctx_len):
    j = pl.program_id(1)
    k_all = jnp.concatenate([kp_ref[...], km_ref[...], kn_ref[...], kc_ref[...]], axis=0)
    v_all = jnp.concatenate([vp_ref[...], vm_ref[...], vn_ref[...], vc_ref[...]], axis=0)
    n_loc = 2 * tq
    n_keys = n_loc + ctx_len
    row = lax.broadcasted_iota(jnp.int32, (tq, n_keys), 0)
    col = lax.broadcasted_iota(jnp.int32, (tq, n_keys), 1)
    p0 = (j - nct) * tq
    kpos = p0 - tq // 2 + col
    band = jnp.abs(row - (col - tq // 2)) <= WINDOW
    in_seq = jnp.logical_and(kpos >= 0, kpos < seq)
    valid_loc = jnp.logical_and(jnp.logical_and(band, in_seq), j >= nct)
    valid = jnp.logical_or(col >= n_loc, valid_loc)
    group = N_HEADS // N_KV_HEADS
    outs = []
    for h in range(N_HEADS):
        kvh = h // group
        qh = q_ref[:, h * HEAD_DIM:(h + 1) * HEAD_DIM]
        kk = k_all[:, kvh * HEAD_DIM:(kvh + 1) * HEAD_DIM]
        vv = v_all[:, kvh * HEAD_DIM:(kvh + 1) * HEAD_DIM]
        s = lax.dot_general(qh, kk, (((1,), (1,)), ((), ())), preferred_element_type=F32)
        s = jnp.where(valid, s, -1e30)
        sink = sink_ref[h:h + 1, 0:1]
        m = jnp.maximum(jnp.max(s, axis=1, keepdims=True), sink)
        p = jnp.exp(s - m)
        denom = jnp.sum(p, axis=1, keepdims=True) + jnp.exp(sink - m)
        o = jnp.dot(p.astype(BF16), vv, preferred_element_type=F32)
        outs.append((o / denom).astype(o_ref.dtype))
    o_ref[...] = jnp.concatenate(outs, axis=1)


def _attention(qkv, sinks, ctx_len):
    b, s, _ = qkv.shape
    seq = s - ctx_len
    tq = 2 * WINDOW
    half = tq // 2
    assert ctx_len % tq == 0 and seq % tq == 0
    nct = ctx_len // tq
    dq = N_HEADS * HEAD_DIM
    dkv = N_KV_HEADS * HEAD_DIM
    kcol, vcol = dq // dkv, dq // dkv + 1
    last_half = s // half - 1
    sink_b = jnp.broadcast_to(sinks.astype(F32)[:, None], (N_HEADS, LANES))

    def prev(col): return lambda bi, j: (bi, jnp.maximum(2 * j - 1, 0), col)
    def main(col): return lambda bi, j: (bi, j, col)
    def nxt(col): return lambda bi, j: (bi, jnp.minimum(2 * j + 2, last_half), col)
    def ctx(col): return lambda bi, j: (bi, 0, col)
    kv_specs = lambda col: [
        pl.BlockSpec((None, half, dkv), prev(col)), pl.BlockSpec((None, tq, dkv), main(col)),
        pl.BlockSpec((None, half, dkv), nxt(col)), pl.BlockSpec((None, ctx_len, dkv), ctx(col)),
    ]
    return pl.pallas_call(
        functools.partial(_attn_kernel, tq=tq, nct=nct, seq=seq, ctx_len=ctx_len),
        grid=(b, s // tq),
        in_specs=[pl.BlockSpec((None, tq, dq), lambda bi, j: (bi, j, 0))] + kv_specs(kcol) + kv_specs(vcol)
        + [pl.BlockSpec((N_HEADS, LANES), lambda bi, j: (0, 0))],
        out_specs=pl.BlockSpec((None, tq, dq), lambda bi, j: (bi, j, 0)),
        out_shape=jax.ShapeDtypeStruct((b, s, dq), BF16),
        compiler_params=_cparams(("parallel", "parallel")),
        name="swa_attention",
    )(qkv, *([qkv] * 8), sink_b)


def _dwconv3_kernel(um, up, un, cw_ref, cb_ref, g1_ref, g2_ref, v_ref, ext, *, tt, nt, nct, c):
    t = pl.program_id(1)
    prev_ok = jnp.logical_and(t != 0, t != nct)
    next_ok = jnp.logical_and(t != nct - 1, t != nt - 1)
    ext[SUBLANES:SUBLANES + tt, :] = um[...]
    ext[0:SUBLANES, :] = jnp.where(prev_ok, up[...], 0.0)
    ext[SUBLANES + tt:2 * SUBLANES + tt, :] = jnp.where(next_ok, un[...], 0.0)
    y = cb_ref[...]
    for k in range(3):
        y = y + cw_ref[k:k + 1, :] * ext[SUBLANES - 1 + k:SUBLANES - 1 + k + tt, :]
    g1_ref[...] = y[:, 0:c]
    g2_ref[...] = y[:, c:2 * c]
    v_ref[...] = y[:, 2 * c:3 * c]


def _dwconv3(u0, conv_w, conv_b, ctx_len):
    b, s, c3 = u0.shape
    c = c3 // 3
    tt = 256
    assert s % tt == 0 and ctx_len % tt == 0
    nt, nct = s // tt, ctx_len // tt
    r8 = tt // SUBLANES
    last8 = s // SUBLANES - 1
    out = jax.ShapeDtypeStruct((b, s, c), F32)
    ospec = pl.BlockSpec((None, tt, c), lambda bi, t: (bi, t, 0))
    return pl.pallas_call(
        functools.partial(_dwconv3_kernel, tt=tt, nt=nt, nct=nct, c=c),
        grid=(b, nt),
        in_specs=[
            pl.BlockSpec((None, tt, c3), lambda bi, t: (bi, t, 0)),
            pl.BlockSpec((None, SUBLANES, c3), lambda bi, t: (bi, jnp.maximum(t * r8 - 1, 0), 0)),
            pl.BlockSpec((None, SUBLANES, c3), lambda bi, t: (bi, jnp.minimum((t + 1) * r8, last8), 0)),
            pl.BlockSpec((3, c3), lambda bi, t: (0, 0)),
            pl.BlockSpec((1, c3), lambda bi, t: (0, 0)),
        ],
        out_specs=[ospec, ospec, ospec],
        out_shape=[out, out, out],
        scratch_shapes=[pltpu.VMEM((tt + 2 * SUBLANES, c3), F32)],
        compiler_params=_cparams(("parallel", "parallel")),
        name="hyena_dwconv",
    )(u0, u0, u0, conv_w, conv_b.reshape(1, c3))


def _filter_features(seq, ctx_len):
    def feats(length):
        t = jnp.linspace(0.0, 1.0, length, dtype=F32)[:, None]
        omega = (2.0 * math.pi / length) * jnp.arange(length, dtype=F32)[:, None]
        bands = jnp.linspace(1e-4, FILTER_BANDS - 1, FILTER_BANDS, dtype=F32)[None, :]
        return jnp.concatenate([t, jnp.cos(bands * omega), -jnp.sin(bands * omega)], axis=-1)
    z = jnp.concatenate([feats(ctx_len), feats(seq)], axis=0)
    return jnp.pad(z, ((0, 0), (0, 40 - z.shape[1])))


def _filter_kernel(z_ref, w1_ref, b1_ref, w2_ref, b2_ref, w3_ref, b3_ref, fr_ref, w4_ref, dl_ref, hw_ref, sum_ref,
                   *, tm, ctx_len, c):
    j = pl.program_id(0)

    @pl.when(j == 0)
    def _():
        sum_ref[...] = jnp.zeros_like(sum_ref)

    z = z_ref[...]
    fr = fr_ref[...]
    dot = functools.partial(jnp.dot, preferred_element_type=F32, precision=HIGHEST)
    hid = jnp.sin(fr * (dot(z, w1_ref[...]) + b1_ref[...]))
    hid = jnp.sin(fr * (dot(hid, w2_ref[...]) + b2_ref[...]))
    hid = jnp.sin(fr * (dot(hid, w3_ref[...]) + b3_ref[...]))
    h = dot(hid, w4_ref[...])
    window = jnp.exp(-z[:, 0:1] * jnp.abs(dl_ref[...]))
    rows = j * tm + lax.broadcasted_iota(jnp.int32, (tm, c), 0)
    first = jnp.logical_or(rows == 0, rows == ctx_len)
    is_ctx = j * tm < ctx_len
    srow = jnp.where(is_ctx, 0, 1)
    for q in range(4):
        hq = h[:, q * c:(q + 1) * c] * window
        if q % 2 == 1:
            hq = jnp.where(first, 0.0, hq)
        hw_ref[:, q * c:(q + 1) * c] = hq
        part = jnp.sum(jnp.abs(hq), axis=0, keepdims=True)
        for r in range(2):
            sum_ref[r:r + 1, q * c:(q + 1) * c] += jnp.where(srow == r, part, 0.0)


def _hyena_filters(seq, ctx_len, w1, b1, w2, b2, w3, b3, freq, w4, c):
    s = seq + ctx_len
    z = _filter_features(seq, ctx_len)
    tm = 256
    assert ctx_len % tm == 0 and seq % tm == 0
    fw = w1.shape[1]
    w1p = jnp.pad(w1.astype(F32), ((0, 40 - w1.shape[0]), (0, 0)))
    deltas = jnp.linspace(math.log(DECAY_TARGET) / SLOW_DECAY_PCT, math.log(DECAY_TARGET) / FAST_DECAY_PCT, c,
                          dtype=F32)[None, :]
    full = lambda shape: pl.BlockSpec(shape, lambda j: tuple(0 for _ in shape))
    return pl.pallas_call(
        functools.partial(_filter_kernel, tm=tm, ctx_len=ctx_len, c=c),
        grid=(s // tm,),
        in_specs=[
            pl.BlockSpec((tm, 40), lambda j: (j, 0)),
            full((40, fw)), full((1, fw)), full((fw, fw)), full((1, fw)), full((fw, fw)), full((1, fw)),
            full((1, fw)), full((fw, 4 * c)), full((1, c)),
        ],
        out_specs=[pl.BlockSpec((tm, 4 * c), lambda j: (j, 0)), pl.BlockSpec((SUBLANES, 4 * c), lambda j: (0, 0))],
        out_shape=[jax.ShapeDtypeStruct((s, 4 * c), F32), jax.ShapeDtypeStruct((SUBLANES, 4 * c), F32)],
        compiler_params=_cparams(("arbitrary",)),
        name="hyena_filter_mlp",
    )(z, w1p, b1.reshape(1, fw), w2, b2.reshape(1, fw), w3, b3.reshape(1, fw), freq.reshape(1, fw), w4, deltas)


def _stack_hl(m_r, m_i):
    rows = m_r.shape[0]
    g = -(-rows // SUBLANES) * SUBLANES
    pad = lambda a: jnp.pad(a, ((0, g - rows), (0, 0)))
    hi = lambda a: pad(a.astype(BF16))
    lo = lambda a: pad((a - a.astype(BF16).astype(F32)).astype(BF16))
    hl = jnp.concatenate([hi(m_r), hi(m_i), lo(m_r), lo(m_i)], axis=0)
    return hl, hl[:2 * g]


def _fft_tables(seq, ctx_len):
    n1 = FFT_N1
    rc, rl = ctx_len // n1, seq // n1
    n2c, n2l = 2 * rc, 2 * rl
    k2n = n2l + n2c
    r = rc + rl

    def cis(num, den):
        ang = (-2.0 * math.pi / den) * (num % den).astype(F32)
        return jnp.cos(ang), jnp.sin(ang)

    ar = jnp.arange
    fl_r, fl_i = cis(ar(n2l)[:, None] * ar(rl)[None, :], n2l)
    fc_r, fc_i = cis(ar(n2c)[:, None] * ar(rc)[None, :], n2c)
    sl, sc = 1.0 / (n1 * n2l), 1.0 / (n1 * n2c)
    fl_hl, fl_h = (m.astype(F32) for m in _stack_hl(fl_r, fl_i))
    gl_hl, gl_h = (m.astype(F32) for m in _stack_hl(fl_r.T * sl, fl_i.T * sl))
    fc_ri, gc_ri = _stack_ri(fc_r, fc_i), _stack_ri(fc_r.T * sc, fc_i.T * sc)
    tl_r, tl_i = cis(ar(n2l)[:, None] * ar(n1)[None, :], n1 * n2l)
    tc_r, tc_i = cis(ar(n2c)[:, None] * ar(n1)[None, :], n1 * n2c)
    tw_r, tw_i = jnp.concatenate([tl_r, tc_r], 0), jnp.concatenate([tl_i, tc_i], 0)
    bl = lambda a: jnp.broadcast_to(a[:, :, None], a.shape + (LANES,))
    f1_r, f1_i = cis(ar(n1)[:, None] * ar(n1)[None, :], n1)
    f1_hl, f1_h = _stack_hl(f1_r, f1_i)
    return dict(fl_hl=fl_hl, fl_h=fl_h, fc_ri=fc_ri, gl_hl=gl_hl, gl_h=gl_h, gc_ri=gc_ri, f1_hl=f1_hl, f1_h=f1_h,
                ta_r=bl(tw_r.T), ta_i=bl(tw_i.T),
                tb_r=bl(tw_r), tb_i=bl(tw_i),
                k2n=k2n, n2l=n2l, n2c=n2c, r=r, rc=rc, rl=rl)


def _stack_ri(m_r, m_i):
    rows = m_r.shape[0]
    g = -(-rows // SUBLANES) * SUBLANES
    pad = lambda a: jnp.pad(a, ((0, g - rows), (0, 0)))
    return jnp.concatenate([pad(m_r), pad(m_i)], axis=0)


def _cdot6(m_ref, x, rows):
    s = jnp.dot(m_ref[...], x, preferred_element_type=F32, precision=HIGHEST)
    g = m_ref.shape[0] // 2
    return s[0:rows], s[g:g + rows]


def _keep_bf16_bits(x):
    bits = pltpu.bitcast(x, jnp.uint32) & jnp.uint32(0xFFFF0000)
    return pltpu.bitcast(bits, F32)


def _split_f32(x):
    hi = _keep_bf16_bits(x)
    return hi, _keep_bf16_bits(x - hi)


def _cdot3f(fhl_ref, fh_ref, x_hi, x_lo, rows):
    s = jnp.dot(fhl_ref[...], x_hi, preferred_element_type=F32)
    t = jnp.dot(fh_ref[...], x_lo, preferred_element_type=F32)
    g = fh_ref.shape[0] // 2
    return (s[0:rows] + s[2 * g:2 * g + rows] + t[0:rows],
            s[g:g + rows] + s[3 * g:3 * g + rows] + t[g:g + rows])


def _cdot3(fhl_ref, fh_ref, x, rows):
    x_hi = x.astype(BF16)
    x_lo = (x - x_hi.astype(F32)).astype(BF16)
    s = jnp.dot(fhl_ref[...], x_hi, preferred_element_type=F32)
    t = jnp.dot(fh_ref[...], x_lo, preferred_element_type=F32)
    g = fh_ref.shape[0] // 2
    return (s[0:rows] + s[2 * g:2 * g + rows] + t[0:rows],
            s[g:g + rows] + s[3 * g:3 * g + rows] + t[g:g + rows])


FFT_NB1 = 8
FFT_CC = 1024


def _rows(ref, lead, rsel, n):
    blk = ref[lead + (rsel, slice(n, n + 1), slice(None))]
    return blk.reshape(blk.shape[0], blk.shape[2])


def _fft_a_kernel(x_ref, flhl_ref, flh_ref, fc_ref, tr_ref, ti_ref, o_ref, xh, xl, *, packed, cc, rc, n2l, n2c):
    hi, lo = _split_f32(x_ref[...])
    xh[...] = hi
    xl[...] = lo
    lat = lambda lead, rsel, n: _cdot3f(flhl_ref, flh_ref, _rows(xh, lead, rsel, n), _rows(xl, lead, rsel, n), n2l)
    ctx = lambda lead, rsel, n: _cdot6(fc_ref, _rows(x_ref, lead, rsel, n), n2c)
    segs = ((slice(0, n2l), slice(rc, None), lat), (slice(n2l, n2l + n2c), slice(0, rc), ctx))
    for n in range(FFT_NB1):
        for ksel, rsel, cdot in segs:
            if packed:
                rr, ir = cdot((0,), rsel, n)
                ri, ii = cdot((1,), rsel, n)
                a_r, a_i = rr - ii, ir + ri
            else:
                a_r, a_i = cdot((), rsel, n)
            tr, ti = tr_ref[n, ksel, :], ti_ref[n, ksel, :]
            o_r, o_i = [], []
            for l in range(cc // LANES):
                cr, ci = a_r[:, l * LANES:(l + 1) * LANES], a_i[:, l * LANES:(l + 1) * LANES]
                o_r.append(cr * tr - ci * ti)
                o_i.append(cr * ti + ci * tr)
            rows = a_r.shape[0]
            o_ref[0, ksel, n:n + 1, :] = jnp.concatenate(o_r, axis=1).reshape(rows, 1, cc)
            o_ref[1, ksel, n:n + 1, :] = jnp.concatenate(o_i, axis=1).reshape(rows, 1, cc)


def _fft_stage_a(xv, tb, packed, c):
    k2n, r = tb["k2n"], tb["r"]
    cc = min(c, FFT_CC)
    nnb, ncc = FFT_N1 // FFT_NB1, c // cc
    mats = [tb["fl_hl"], tb["fl_h"], tb["fc_ri"]]
    mat_specs = [pl.BlockSpec(m.shape, lambda *a: (0, 0)) for m in mats]
    if packed:
        p = xv.shape[1]
        grid = (p, nnb, ncc)
        x_spec = pl.BlockSpec((2, None, r, FFT_NB1, cc), lambda pi, nb, cb: (0, pi, 0, nb, cb))
        t_spec = pl.BlockSpec((FFT_NB1, k2n, LANES), lambda pi, nb, cb: (nb, 0, 0))
        o_spec = pl.BlockSpec((None, 2, k2n, FFT_NB1, cc), lambda pi, nb, cb: (pi, 0, 0, nb, cb))
        o_shape = jax.ShapeDtypeStruct((p, 2, k2n, FFT_N1, c), F32)
        sem = ("parallel", "parallel", "parallel")
    else:
        grid = (nnb, ncc)
        x_spec = pl.BlockSpec((r, FFT_NB1, cc), lambda nb, cb: (0, nb, cb))
        t_spec = pl.BlockSpec((FFT_NB1, k2n, LANES), lambda nb, cb: (nb, 0, 0))
        o_spec = pl.BlockSpec((2, k2n, FFT_NB1, cc), lambda nb, cb: (0, 0, nb, cb))
        o_shape = jax.ShapeDtypeStruct((2, k2n, FFT_N1, c), F32)
        sem = ("parallel", "parallel")
    return pl.pallas_call(
        functools.partial(_fft_a_kernel, packed=packed, cc=cc, rc=tb["rc"], n2l=tb["n2l"], n2c=tb["n2c"]),
        grid=grid,
        in_specs=[x_spec] + mat_specs + [t_spec, t_spec],
        out_specs=o_spec,
        out_shape=o_shape,
        scratch_shapes=[pltpu.VMEM(((2,) if packed else ()) + (r, FFT_NB1, cc), F32)] * 2,
        compiler_params=_cparams(sem),
        name="hyena_fft_a",
    )(xv, *mats, tb["ta_r"], tb["ta_i"])


def _fft_bf_kernel(a_ref, fhl_ref, fh_ref, o_ref):
    rr, ir = _cdot3(fhl_ref, fh_ref, a_ref[0], FFT_N1)
    ri, ii = _cdot3(fhl_ref, fh_ref, a_ref[1], FFT_N1)
    o_ref[0] = rr - ii
    o_ref[1] = ir + ri


def _fft_stage_b_filter(af, tb, c4):
    k2n = tb["k2n"]
    cc = min(c4, FFT_CC)
    mhl = pl.BlockSpec((4 * FFT_N1, FFT_N1), lambda k, cb: (0, 0))
    mh = pl.BlockSpec((2 * FFT_N1, FFT_N1), lambda k, cb: (0, 0))
    blk = pl.BlockSpec((2, None, FFT_N1, cc), lambda k, cb: (0, k, 0, cb))
    return pl.pallas_call(
        _fft_bf_kernel,
        grid=(k2n, c4 // cc),
        in_specs=[blk, mhl, mh],
        out_specs=blk,
        out_shape=jax.ShapeDtypeStruct(af.shape, F32),
        compiler_params=_cparams(("parallel", "parallel")),
        name="hyena_fft_b_filter",
    )(af, tb["f1_hl"], tb["f1_h"])


def _fft_mid_kernel(a_ref, h0_ref, h1_ref, s0_ref, s1_ref, fhl_ref, fh_ref, tr_ref, ti_ref, o_ref, *, n2l, cc):
    k2 = pl.program_id(1)
    rr, ir = _cdot3(fhl_ref, fh_ref, a_ref[0], FFT_N1)
    ri, ii = _cdot3(fhl_ref, fh_ref, a_ref[1], FFT_N1)
    xr = rr - ii
    xi = ir + ri
    norm = s0_ref[...] + s1_ref[...]
    inv = 1.0 / jnp.where(k2 >= n2l, norm[0:1, :], norm[1:2, :])
    kr = (h0_ref[0] + h1_ref[0]) * inv
    ki = (h0_ref[1] - h1_ref[1]) * inv
    yr = xr * kr - xi * ki
    yi = xr * ki + xi * kr
    r_yr, i_yr = _cdot3(fhl_ref, fh_ref, yr, FFT_N1)
    r_yi, i_yi = _cdot3(fhl_ref, fh_ref, yi, FFT_N1)
    br = r_yr + i_yi
    bi = r_yi - i_yr
    tr, ti = tr_ref[...], ti_ref[...]
    for l in range(cc // LANES):
        sl = slice(l * LANES, (l + 1) * LANES)
        cr, ci = br[:, sl], bi[:, sl]
        o_ref[0, :, sl] = cr * tr + ci * ti
        o_ref[1, :, sl] = ci * tr - cr * ti


def _fft_mid(a, hspec, sums, tb, order, c):
    p = a.shape[0]
    k2n, n2l = tb["k2n"], tb["n2l"]
    cc = min(c, FFT_CC)
    ncc = c // cc
    mhl = pl.BlockSpec((4 * FFT_N1, FFT_N1), lambda pi, k, cb: (0, 0))
    mh = pl.BlockSpec((2 * FFT_N1, FFT_N1), lambda pi, k, cb: (0, 0))
    blk = pl.BlockSpec((None, 2, None, FFT_N1, cc), lambda pi, k, cb: (pi, 0, k, 0, cb))
    hs = lambda side: pl.BlockSpec((2, None, FFT_N1, cc), lambda pi, k, cb: (0, k, 0, (2 * order + side) * ncc + cb))
    ss = lambda side: pl.BlockSpec((SUBLANES, cc), lambda pi, k, cb: (0, (2 * order + side) * ncc + cb))
    tw = pl.BlockSpec((None, FFT_N1, LANES), lambda pi, k, cb: (k, 0, 0))
    return pl.pallas_call(
        functools.partial(_fft_mid_kernel, n2l=n2l, cc=cc),
        grid=(p, k2n, ncc),
        in_specs=[blk, hs(0), hs(1), ss(0), ss(1), mhl, mh, tw, tw],
        out_specs=blk,
        out_shape=jax.ShapeDtypeStruct(a.shape, F32),
        compiler_params=_cparams(("parallel", "parallel", "parallel")),
        name="hyena_fft_mid",
    )(a, hspec, hspec, sums, sums, tb["f1_hl"], tb["f1_h"], tb["tb_r"], tb["tb_i"])


def _fft_inv_kernel(b_ref, glhl_ref, glh_ref, gc_ref, g_ref, v_ref, skip_ref, o_ref, bh, bl, ybuf,
                    *, cc, rc, rl, n2l, n2c):
    hi, lo = _split_f32(b_ref[...])
    bh[...] = hi
    bl[...] = lo
    lat = lambda lead, ksel, n: _cdot3f(glhl_ref, glh_ref, _rows(bh, lead, ksel, n), _rows(bl, lead, ksel, n), rl)
    ctx = lambda lead, ksel, n: _cdot6(gc_ref, _rows(b_ref, lead, ksel, n), rc)
    segs = ((slice(0, n2l), slice(rc, rc + rl), lat), (slice(n2l, n2l + n2c), slice(0, rc), ctx))
    for n in range(FFT_NB1):
        for ksel, rsel, cdot in segs:
            r_br, i_br = cdot((0,), ksel, n)
            r_bi, i_bi = cdot((1,), ksel, n)
            yr = r_br + i_bi
            yi = r_bi - i_br
            ybuf[0, rsel, n:n + 1, :] = yr.reshape(yr.shape[0], 1, cc)
            ybuf[1, rsel, n:n + 1, :] = yi.reshape(yi.shape[0], 1, cc)
    o_ref[...] = (g_ref[...] * (ybuf[...] + skip_ref[...] * v_ref[...])).astype(o_ref.dtype)


def _fft_inverse_gate(bv, gv, vv, skip, tb, c):
    p = bv.shape[0]
    k2n, r = tb["k2n"], tb["r"]
    cc = min(c, FFT_CC)
    mats = [tb["gl_hl"], tb["gl_h"], tb["gc_ri"]]
    mat_specs = [pl.BlockSpec(m.shape, lambda pi, nb, cb: (0, 0)) for m in mats]
    xs = pl.BlockSpec((2, None, r, FFT_NB1, cc), lambda pi, nb, cb: (0, pi, 0, nb, cb))
    return pl.pallas_call(
        functools.partial(_fft_inv_kernel, cc=cc, rc=tb["rc"], rl=tb["rl"], n2l=tb["n2l"], n2c=tb["n2c"]),
        grid=(p, FFT_N1 // FFT_NB1, c // cc),
        in_specs=[pl.BlockSpec((None, 2, k2n, FFT_NB1, cc), lambda pi, nb, cb: (pi, 0, 0, nb, cb))] + mat_specs
        + [xs, xs, pl.BlockSpec((1, cc), lambda pi, nb, cb: (0, cb))],
        out_specs=xs,
        out_shape=jax.ShapeDtypeStruct(gv.shape, F32),
        scratch_shapes=[pltpu.VMEM((2, k2n, FFT_NB1, cc), F32), pltpu.VMEM((2, k2n, FFT_NB1, cc), F32),
                        pltpu.VMEM((2, r, FFT_NB1, cc), F32)],
        compiler_params=_cparams(("parallel", "parallel", "parallel")),
        name="hyena_fft_inv",
    )(bv, *mats, gv, vv, skip.reshape(1, c))


def _hyena_mix(u0, conv_w, conv_b, fw1, fb1, fw2, fb2, fw3, fb3, ffreq, fw4, skip, ctx_len):
    b, s, c3 = u0.shape
    c = c3 // 3
    seq = s - ctx_len
    assert b % 2 == 0 and seq % FFT_N1 == 0 and ctx_len % FFT_N1 == 0
    p = b // 2
    r = s // FFT_N1
    g1, g2, v = _dwconv3(u0, conv_w, conv_b, ctx_len)
    hw, sums = _hyena_filters(seq, ctx_len, fw1, fb1, fw2, fb2, fw3, fb3, ffreq, fw4, c)
    tb = _fft_tables(seq, ctx_len)
    af = _fft_stage_a(hw.reshape(r, FFT_N1, 4 * c), tb, False, 4 * c)
    hspec = _fft_stage_b_filter(af, tb, 4 * c)
    view = lambda t: t.reshape(2, p, r, FFT_N1, c)
    z = v
    for order, gate in ((0, g1), (1, g2)):
        a = _fft_stage_a(view(z), tb, True, c)
        bm = _fft_mid(a, hspec, sums, tb, order, c)
        z = _fft_inverse_gate(bm, view(gate), view(z), skip[order], tb, c).reshape(b, s, c)
    return z


def _route(h, r_ref):
    ne = N_EXPERTS
    h_hi = h.astype(BF16)
    h_lo = (h - h_hi.astype(F32)).astype(BF16)
    s = jnp.dot(h_hi, r_ref[...], preferred_element_type=F32)
    logits = s[:, :LANES] + s[:, LANES:] + jnp.dot(h_lo, r_ref[:, :LANES], preferred_element_type=F32)
    lane = lax.broadcasted_iota(jnp.int32, logits.shape, 1)
    neg = -jnp.inf
    logits = jnp.where(lane < ne, logits, neg)
    m1 = jnp.max(logits, axis=1, keepdims=True)
    i1 = jnp.min(jnp.where(logits == m1, lane, LANES), axis=1, keepdims=True)
    rest = jnp.where(lane == i1, neg, logits)
    m2 = jnp.max(rest, axis=1, keepdims=True)
    i2 = jnp.min(jnp.where(rest == m2, lane, LANES), axis=1, keepdims=True)
    e2 = jnp.exp(m2 - m1)
    w1 = 1.0 / (1.0 + e2)
    return h_hi, jnp.where(lane == i1, w1, 0.0) + jnp.where(lane == i2, e2 * w1, 0.0)


def _router_operand(router):
    assert router.shape[1] == N_EXPERTS
    rp = jnp.pad(router.astype(F32), ((0, 0), (0, LANES - N_EXPERTS)))
    rp_hi = rp.astype(BF16)
    return jnp.concatenate([rp_hi, (rp - rp_hi.astype(F32)).astype(BF16)], axis=1)


def _moe_sparse_kernel(h_ref, x_ref, mod_ref, gate_ref, wg_ref, wu_ref, wd_ref, o_ref, dest, cnt, xe, ye,
                       *, tb, ffn_rows, main_tiles, rb, ctx_len, ne, nf):
    j, e, f = pl.program_id(1), pl.program_id(2), pl.program_id(3)
    half = tb // 2

    @pl.when(jnp.logical_and(e == 0, f == 0))
    def _():
        tri = (lax.broadcasted_iota(jnp.int32, (rb, rb), 0) > lax.broadcasted_iota(jnp.int32, (rb, rb), 1))
        tri = jnp.where(tri, 1.0, 0.0).astype(BF16)
        carry = jnp.zeros((1, LANES), F32)
        for k in range(tb // rb):
            routed = gate_ref[k * rb:(k + 1) * rb, :] > 0.0
            r01 = jnp.where(routed, 1.0, 0.0)
            before = jnp.dot(tri, r01.astype(BF16), preferred_element_type=F32) + carry
            dest[k * rb:(k + 1) * rb, :] = jnp.where(routed, before, -1.0)
            carry = carry + jnp.sum(r01, axis=0, keepdims=True)
        cnt[...] = jnp.broadcast_to(carry, cnt.shape)
        o_ref[...] = jnp.zeros_like(o_ref)

    lane1 = lax.broadcasted_iota(jnp.int32, (1, LANES), 1)
    n_e = jnp.sum(jnp.where(lane1 == e, cnt[0:1, :], 0.0))
    n_tiles = jnp.ceil(n_e * (1.0 / LANES)).astype(jnp.int32)

    lane_h = lax.broadcasted_iota(jnp.int32, (half, LANES), 1)

    def expert_col(ref, rows):
        return jnp.sum(jnp.where(lane_h == e, ref[rows, :], 0.0), axis=1, keepdims=True)

    def one_hot(rows, first, width):
        dcol = expert_col(dest, rows) - first
        lanef = lane_h.astype(F32)
        pieces = [jnp.where(dcol == lanef + float(k * LANES), 1.0, 0.0).astype(BF16) for k in range(width // LANES)]
        return jnp.concatenate(pieces, axis=1)

    def gather(first, width, dst):
        acc = None
        for hb in range(2):
            rows = slice(hb * half, (hb + 1) * half)
            part = lax.dot_general(one_hot(rows, first, width), h_ref[rows, :], (((0,), (0,)), ((), ())),
                                   preferred_element_type=F32)
            acc = part if acc is None else acc + part
        xe[dst, :] = acc.astype(BF16)
        ye[dst, :] = jnp.zeros((width, ye.shape[1]), F32)

    def expert_ffn(dst):
        x = xe[dst, :]
        a = jnp.dot(x, wg_ref[...], preferred_element_type=F32)
        u = jnp.dot(x, wu_ref[...], preferred_element_type=F32)
        act = (a * jax.nn.sigmoid(a) * u).astype(BF16)
        ye[dst, :] += jnp.dot(act, wd_ref[...], preferred_element_type=F32)

    def scatter(first, width, dst):
        y = ye[dst, :].astype(BF16)
        for hb in range(2):
            rows = slice(hb * half, (hb + 1) * half)
            back = jnp.dot(one_hot(rows, first, width), y, preferred_element_type=F32)
            o_ref[rows, :] += expert_col(gate_ref, rows) * back

    main_rows = main_tiles * LANES
    active = n_tiles > 0
    n_extra = jnp.maximum(n_tiles - main_tiles, 0)

    def extra(fn):
        def body(c, carry):
            off = pl.multiple_of(main_rows + c * LANES, LANES)
            fn(off.astype(F32), LANES, pl.ds(off, LANES))
            return carry
        lax.fori_loop(0, n_extra, body, 0)

    def main_ffn():
        below = None
        for i, rows_c in enumerate(ffn_rows):
            sel = n_e <= float(rows_c) if i + 1 < len(ffn_rows) else n_e > float(below)
            if below is not None and i + 1 < len(ffn_rows):
                sel = jnp.logical_and(sel, n_e > float(below))
            pl.when(sel)(functools.partial(expert_ffn, slice(0, rows_c)))
            below = rows_c

    @pl.when(jnp.logical_and(active, f == 0))
    def _():
        gather(0.0, main_rows, slice(0, main_rows))
        extra(gather)

    @pl.when(active)
    def _():
        main_ffn()
        extra(lambda first, width, dst: expert_ffn(dst))

    @pl.when(jnp.logical_and(active, f == nf - 1))
    def _():
        scatter(0.0, main_rows, slice(0, main_rows))
        extra(scatter)

    @pl.when(jnp.logical_and(e == ne - 1, f == nf - 1))
    def _():
        x = x_ref[...]
        is_ctx = _row_is_ctx(j, tb, ctx_len, x.shape)
        o_ref[...] = x + _sel_mod(mod_ref, is_ctx, 5) * o_ref[...]


def _moe_sparse(hb16, x_all, mods_i, gate, wg, wu, wd, ctx_len):
    b, s, d = x_all.shape
    ne, _, f = wg.shape
    tb = _pick_tile(s, (1088, 768, 512, 256))
    max_tiles = -(-tb // LANES)
    ch = max_tiles * LANES
    mean_tiles = -(-tb * TOP_K // (ne * LANES))
    min_tiles, main_tiles = max(mean_tiles - 1, 1), min(mean_tiles + 1, max_tiles)
    half = LANES // 2
    ffn_rows = tuple(sorted({min_tiles * LANES, min_tiles * LANES + half, mean_tiles * LANES, main_tiles * LANES}))
    rb = max(r for r in range(SUBLANES, 257, SUBLANES) if tb % r == 0)
    fc = _pick_tile(f, (896, 512, 256, 128))
    nf = f // fc
    row = lambda bi, j, e, k: (bi, j, 0)
    return pl.pallas_call(
        functools.partial(_moe_sparse_kernel, tb=tb, ffn_rows=ffn_rows, main_tiles=main_tiles, rb=rb,
                          ctx_len=ctx_len, ne=ne, nf=nf),
        grid=(b, s // tb, ne, nf),
        in_specs=[
            pl.BlockSpec((None, tb, d), row),
            pl.BlockSpec((None, tb, d), row, pipeline_mode=pl.Buffered(1)),
            pl.BlockSpec((None, 2, 8, d), lambda bi, j, e, k: (bi, 0, 0, 0)),
            pl.BlockSpec((None, tb, LANES), row),
            pl.BlockSpec((None, d, fc), lambda bi, j, e, k: (e, 0, k)),
            pl.BlockSpec((None, d, fc), lambda bi, j, e, k: (e, 0, k)),
            pl.BlockSpec((None, fc, d), lambda bi, j, e, k: (e, k, 0)),
        ],
        out_specs=pl.BlockSpec((None, tb, d), row),
        out_shape=jax.ShapeDtypeStruct((b, s, d), F32),
        scratch_shapes=[pltpu.VMEM((tb, LANES), F32), pltpu.VMEM((SUBLANES, LANES), F32),
                        pltpu.VMEM((ch, d), BF16), pltpu.VMEM((ch, d), F32)],
        compiler_params=_cparams(("parallel", "parallel", "arbitrary", "arbitrary")),
        name="moe_experts",
    )(hb16, x_all, mods_i, gate, wg.astype(BF16), wu.astype(BF16), wd.astype(BF16))


def _final_norm_kernel(x_ref, g_ref, o_ref):
    x = x_ref[...]
    ms = jnp.mean(x * x, axis=-1, keepdims=True)
    o_ref[...] = (x * lax.rsqrt(ms + EPS)) * g_ref[...]


def _final_norm(x_all, g, ctx_len):
    b, s, d = x_all.shape
    seq = s - ctx_len
    tm = _pick_tile(math.gcd(seq, ctx_len), (512, 256, 128))
    off = ctx_len // tm
    return pl.pallas_call(
        _final_norm_kernel,
        grid=(b, seq // tm),
        in_specs=[pl.BlockSpec((None, tm, d), lambda bi, j: (bi, j + off, 0)), pl.BlockSpec((1, d), lambda bi, j: (0, 0))],
        out_specs=pl.BlockSpec((None, tm, d), lambda bi, j: (bi, j, 0)),
        out_shape=jax.ShapeDtypeStruct((b, seq, d), F32),
        compiler_params=_cparams(("parallel", "parallel")),
        name="final_norm",
    )(x_all, g.reshape(1, d))


def kernel(x, c, ctx, c_ctx, ada_w, ada_b, norm_mix, norm_ffn, norm_final,
           lru_w_in, lru_conv_w, lru_conv_b, lru_w_a, lru_b_a, lru_w_x, lru_b_x, lru_lambda, lru_w_out,
           attn_w_qkv, attn_sinks, attn_w_o,
           hy_w_in, hy_b_in, hy_conv_w, hy_conv_b, hy_f_w1, hy_f_b1, hy_f_w2, hy_f_b2, hy_f_w3, hy_f_b3,
           hy_f_freq, hy_f_w4, hy_skip, hy_w_out, hy_b_out,
           ffn_w_gate, ffn_w_up, ffn_w_down,
           moe_router, moe_w_gate, moe_w_up, moe_w_down):
    depth = ada_w.shape[0]
    ctx_len, seq, d = ctx.shape[1], x.shape[1], x.shape[2]
    x_all = jnp.concatenate([ctx, x], axis=1)
    mods = _mods(c, c_ctx, ada_w, ada_b)
    zero_d = jnp.zeros((d,), F32)
    for i in range(depth):
        j = i // N_MIXERS
        moe = i % 2 == 1
        router = moe_router[i // 2] if moe else None
        if i % N_MIXERS == 0:
            gu = _norm_mm(x_all, mods[i], norm_mix[i], lru_w_in[j], jnp.zeros((lru_w_in.shape[2],), F32), ctx_len, F32)
            hs_f, hs_b = _lru_scan(gu, lru_conv_w[j], lru_conv_b[j], lru_w_a[j], lru_b_a[j], lru_w_x[j], lru_b_x[j],
                                   lru_lambda[j], ctx_len)
            x_all, h2, *gate = _lru_mm_res(hs_f, hs_b, gu, x_all, mods[i], norm_ffn[i], lru_w_out[j], ctx_len, router)
        elif i % N_MIXERS == 1:
            cos_t, sin_t = _rope_tables(seq, ctx_len)
            qkv = _qkv_proj(x_all, mods[i], norm_mix[i], attn_w_qkv[j], cos_t, sin_t, ctx_len)
            o = _attention(qkv, attn_sinks[j], ctx_len)
            x_all, h2, *gate = _mm_res(o, x_all, mods[i], norm_ffn[i], attn_w_o[j], zero_d, ctx_len, router)
        else:
            u0 = _norm_mm(x_all, mods[i], norm_mix[i], hy_w_in[j], hy_b_in[j], ctx_len, F32)
            z = _hyena_mix(u0, hy_conv_w[j], hy_conv_b[j], hy_f_w1[j], hy_f_b1[j], hy_f_w2[j], hy_f_b2[j],
                           hy_f_w3[j], hy_f_b3[j], hy_f_freq[j], hy_f_w4[j], hy_skip[j], ctx_len)
            x_all, h2, *gate = _mm_res(z, x_all, mods[i], norm_ffn[i], hy_w_out[j], hy_b_out[j], ctx_len, router)
        if moe:
            x_all = _moe_sparse(h2, x_all, mods[i], gate[0], moe_w_gate[i // 2], moe_w_up[i // 2], moe_w_down[i // 2],
                                ctx_len)
        else:
            x_all = _ffn_dense(h2, x_all, mods[i], ffn_w_gate[i // 2], ffn_w_up[i // 2], ffn_w_down[i // 2], ctx_len)
    return _final_norm(x_all, norm_final, ctx_len)
```
